```python
import math
import jax, jax.numpy as jnp
from jax import lax
import numpy as np

D_MODEL = 1024
BATCH = 32
SEQ = 256
DEPTH = 2
DEC_BATCH = 4
DEC_SEQ = 4096
PAST_LEN = 256

GRID_W = 64
HEAD_DIM = 64
D_POOL = D_MODEL // 4
D_ATTN = D_MODEL // 2
D_DELTA = D_MODEL // 4
D_MIX = D_POOL + D_ATTN + D_DELTA
POOL_WINDOWS = (2, 4, 8, 16)
N_POOL_GROUPS = 4
POOL_GROUP_DIM = D_POOL // N_POOL_GROUPS
N_Q_HEADS = D_ATTN // HEAD_DIM
N_KV_HEADS = 2
Q_PER_KV = N_Q_HEADS // N_KV_HEADS
D_KV = N_KV_HEADS * HEAD_DIM
N_DELTA_HEADS = D_DELTA // HEAD_DIM
CONV_K = 4
CHUNK = 64
Q_BLOCK = 128
ROPE_THETA = 10000.0
EPS = 1e-6
IN_WIDTHS = (D_POOL, D_POOL, D_ATTN, D_KV, D_KV, D_ATTN, 3 * D_DELTA, 2 * N_DELTA_HEADS, 2 * N_DELTA_HEADS, D_DELTA)
D_IN = sum(IN_WIDTHS)

kernel_name = "hybrid_pool_gqa_deltanet_diffusion_step"


def rmsnorm(x, g):
    xf = x.astype(jnp.float32)
    y = xf * lax.rsqrt(jnp.mean(xf * xf, axis=-1, keepdims=True) + EPS)
    return (y * g.astype(jnp.float32)).astype(x.dtype)


def l2norm(x):
    xf = x.astype(jnp.float32)
    return xf * lax.rsqrt(jnp.sum(xf * xf, axis=-1, keepdims=True) + EPS)


def centred_mean_minus_self(u, w):
    L = u.shape[1]
    uf = u.astype(jnp.float32)
    cs = jnp.concatenate([jnp.zeros_like(uf[:, :1]), jnp.cumsum(uf, axis=1)], axis=1)
    t = jnp.arange(L)
    lo = jnp.clip(t - w // 2, 0, L)
    hi = jnp.clip(t - w // 2 + w, 0, L)
    cnt = (hi - lo).astype(jnp.float32)[None, :, None]
    return ((cs[:, hi] - cs[:, lo]) / cnt).astype(u.dtype) - u


def pool_mixer(u, pool_w, pool_scale):
    B, L, _ = u.shape
    ug = u.reshape(B, L, N_POOL_GROUPS, POOL_GROUP_DIM)
    pooled = jnp.stack([centred_mean_minus_self(ug[:, :, g], POOL_WINDOWS[g]) for g in range(N_POOL_GROUPS)], axis=2)
    out = jnp.einsum('blgc,gcd->blgd', pooled, pool_w).reshape(B, L, D_POOL)
    return out * pool_scale


def axial_rope(num_tokens):
    rows = num_tokens // GRID_W
    row = jnp.repeat(jnp.arange(rows, dtype=jnp.float32), GRID_W)
    col = (jnp.arange(rows * GRID_W) % GRID_W).astype(jnp.float32)
    axis_dim = HEAD_DIM // 2
    inv = ROPE_THETA ** (-jnp.arange(0, axis_dim, 2, dtype=jnp.float32) / axis_dim)
    ang = jnp.concatenate([row[:, None] * inv, col[:, None] * inv], axis=-1)
    return jnp.cos(ang), jnp.sin(ang)


def apply_rope(x, cos, sin):
    xf = x.astype(jnp.float32).reshape(x.shape[:-1] + (HEAD_DIM // 2, 2))
    x1, x2 = xf[..., 0], xf[..., 1]
    c = cos[None, :, None, :]
    s = sin[None, :, None, :]
    out = jnp.stack([x1 * c - x2 * s, x1 * s + x2 * c], axis=-1)
    return out.reshape(x.shape).astype(x.dtype)


def block_attention(q, k, v):
    B, Lq = q.shape[:2]
    nb = Lq // Q_BLOCK
    qb = q.reshape(B, nb, Q_BLOCK, N_KV_HEADS, Q_PER_KV, HEAD_DIM).transpose(1, 0, 2, 3, 4, 5)
    scale = HEAD_DIM ** -0.5

    def one_block(qblk):
        s = jnp.einsum('bqkgd,bskd->bkgqs', qblk, k, preferred_element_type=jnp.float32) * scale
        p = jax.nn.softmax(s, axis=-1).astype(v.dtype)
        return jnp.einsum('bkgqs,bskd->bqkgd', p, v)

    o = lax.map(one_block, qb)
    return o.transpose(1, 0, 2, 3, 4, 5).reshape(B, Lq, D_ATTN)


def centred_depthwise_conv(x, w):
    C = x.shape[-1]
    return lax.conv_general_dilated(
        x, w[:, None, :].astype(x.dtype), window_strides=(1,),
        padding=[(CONV_K // 2, CONV_K - 1 - CONV_K // 2)],
        dimension_numbers=('NWC', 'WIO', 'NWC'), feature_group_count=C)


def gated_delta_chunked(q, k, v, g, beta, s0):
    B, L, H, DK = q.shape
    DV = v.shape[-1]
    n = L // CHUNK
    f32 = jnp.float32

    def chunks(t):
        t = t.astype(f32).reshape((B, n, CHUNK, H) + t.shape[3:])
        return jnp.moveaxis(jnp.moveaxis(t, 1, 0), 3, 2)

    q = chunks(q) * DK ** -0.5
    k = chunks(k)
    v = chunks(v)
    g = chunks(g)
    beta = chunks(beta)
    gc = jnp.cumsum(g, axis=-1)
    idx = jnp.arange(CHUNK)
    incl = idx[:, None] >= idx[None, :]
    strict = idx[:, None] > idx[None, :]
    diff = gc[..., :, None] - gc[..., None, :]
    decay = jnp.where(incl, jnp.exp(jnp.where(incl, diff, 0.0)), 0.0)
    kb = k * beta[..., None]
    lower = jnp.where(strict, jnp.einsum('nbhid,nbhjd->nbhij', kb, k) * decay, 0.0)
    a = lower + jnp.eye(CHUNK, dtype=f32)
    u = lax.linalg.triangular_solve(a, v * beta[..., None], left_side=True, lower=True, unit_diagonal=True)
    w = lax.linalg.triangular_solve(a, kb * jnp.exp(gc)[..., None], left_side=True, lower=True, unit_diagonal=True)
    attn = jnp.einsum('nbhid,nbhjd->nbhij', q, k) * decay
    qg = q * jnp.exp(gc)[..., None]
    g_last = gc[..., -1]
    kdec = k * jnp.exp(g_last[..., None] - gc)[..., None]

    def step(S, inp):
        w_i, u_i, qg_i, a_i, kd_i, gl_i = inp
        v_new = u_i - jnp.einsum('bhck,bhkv->bhcv', w_i, S)
        o = jnp.einsum('bhck,bhkv->bhcv', qg_i, S) + jnp.einsum('bhij,bhjv->bhiv', a_i, v_new)
        S = S * jnp.exp(gl_i)[..., None, None] + jnp.einsum('bhck,bhcv->bhkv', kd_i, v_new)
        return S, o

    S, o = lax.scan(step, s0.astype(f32), (w, u, qg, attn, kdec, g_last))
    o = jnp.swapaxes(jnp.moveaxis(o, 0, 1), 2, 3).reshape(B, L, H, DV)
    return o, S


def trunk_layer(x, cond, w_mod, b_mod, norm_pre, norm_post, w_in, w_out, pool_w, pool_scale,
                q_norm, k_norm, conv_w, a_log, dt_bias, o_norm, ctx_k=None, ctx_v=None, ctx_state=None):
    B, L, _ = x.shape
    latent = ctx_k is not None
    mod = (jnp.dot(jax.nn.silu(cond), w_mod) + b_mod)[:, None, :]
    shift, scale, gate = jnp.split(mod, 3, axis=-1)
    h = rmsnorm(x, norm_pre) * (1 + scale) + shift
    z = jnp.einsum('bld,de->ble', h, w_in)
    points = np.cumsum(IN_WIDTHS)[:-1].tolist()
    u_pool, g_pool, q, k, v, g_attn, qkv_d, b_d, a_d, g_delta = jnp.split(z, points, axis=-1)

    o_pool = pool_mixer(u_pool, pool_w, pool_scale)

    q = rmsnorm(q.reshape(B, L, N_Q_HEADS, HEAD_DIM), q_norm)
    k = rmsnorm(k.reshape(B, L, N_KV_HEADS, HEAD_DIM), k_norm)
    v = v.reshape(B, L, N_KV_HEADS, HEAD_DIM)
    if latent:
        cos, sin = axial_rope(L)
        q_r = apply_rope(q, cos, sin)
        k_all = jnp.concatenate([apply_rope(k, cos, sin), ctx_k.astype(k.dtype)], axis=1)
        v_all = jnp.concatenate([v, ctx_v.astype(v.dtype)], axis=1)
    else:
        q_r, k_all, v_all = q, k, v
    o_attn = block_attention(q_r, k_all, v_all)

    qkv_d = jax.nn.silu(centred_depthwise_conv(qkv_d, conv_w))
    qd, kd, vd = jnp.split(qkv_d, 3, axis=-1)
    qd = l2norm(qd.reshape(B, L, N_DELTA_HEADS, HEAD_DIM))
    kd = l2norm(kd.reshape(B, L, N_DELTA_HEADS, HEAD_DIM))
    vd = vd.reshape(B, L, N_DELTA_HEADS, HEAD_DIM)
    beta = jax.nn.sigmoid(b_d.astype(jnp.float32)).reshape(B, L, 2, N_DELTA_HEADS)
    g = -jnp.exp(a_log.astype(jnp.float32)) * jax.nn.softplus(
        a_d.astype(jnp.float32).reshape(B, L, 2, N_DELTA_HEADS) + dt_bias.astype(jnp.float32))
    if latent:
        s0 = ctx_state
    else:
        s0 = jnp.zeros((B, 2, N_DELTA_HEADS, HEAD_DIM, HEAD_DIM), jnp.float32)
    o_f, s_f = gated_delta_chunked(qd, kd, vd, g[:, :, 0], beta[:, :, 0], s0[:, 0])
    o_b, s_b = gated_delta_chunked(jnp.flip(qd, 1), jnp.flip(kd, 1), jnp.flip(vd, 1),
                                   jnp.flip(g[:, :, 1], 1), jnp.flip(beta[:, :, 1], 1), s0[:, 1])
    o_delta = rmsnorm(o_f + jnp.flip(o_b, 1), o_norm).reshape(B, L, D_DELTA).astype(x.dtype)

    gates = jax.nn.silu(jnp.concatenate([g_pool, g_attn, g_delta], axis=-1))
    branches = jnp.concatenate([o_pool.astype(x.dtype), o_attn.astype(x.dtype), o_delta], axis=-1)
    y = jnp.einsum('ble,ed->bld', gates * branches, w_out)
    x = x + gate * rmsnorm(y, norm_post)
    if latent:
        return x
    return x, k, v, jnp.stack([s_f, s_b], axis=1).astype(x.dtype)


def setup_inputs(seed: int = 0) -> dict:
    key = jax.random.key(seed)
    ks = jax.random.split(key, 24)
    f32 = jnp.float32

    def nrm(k, shape, s):
        return jax.random.normal(k, shape, f32) * s

    x_prompt = nrm(ks[0], (BATCH, SEQ, D_MODEL), 1.0)
    x_sample = nrm(ks[1], (DEC_BATCH, DEC_SEQ, D_MODEL), 1.0)
    cache_attn_k = nrm(ks[2], (DEC_BATCH, DEPTH, PAST_LEN, N_KV_HEADS, HEAD_DIM), 1.0)
    cache_attn_v = nrm(ks[3], (DEC_BATCH, DEPTH, PAST_LEN, N_KV_HEADS, HEAD_DIM), 1.0)
    state_delta = nrm(ks[4], (DEC_BATCH, DEPTH, 2, N_DELTA_HEADS, HEAD_DIM, HEAD_DIM), HEAD_DIM ** -0.5)
    c = nrm(ks[5], (DEC_BATCH, D_MODEL), 1.0)
    c_ctx = nrm(ks[6], (D_MODEL,), 1.0)
    w_mod = nrm(ks[7], (DEPTH, D_MODEL, 3 * D_MODEL), 0.5 * D_MODEL ** -0.5)
    b_mod = nrm(ks[8], (DEPTH, 3 * D_MODEL), 0.01)
    norm_pre = 1.0 + nrm(ks[9], (DEPTH, D_MODEL), 0.02)
    norm_post = 1.0 + nrm(ks[10], (DEPTH, D_MODEL), 0.02)
    w_in = nrm(ks[11], (DEPTH, D_MODEL, D_IN), D_MODEL ** -0.5)
    w_out = nrm(ks[12], (DEPTH, D_MIX, D_MODEL), D_MIX ** -0.5)
    pool_w = nrm(ks[13], (DEPTH, N_POOL_GROUPS, POOL_GROUP_DIM, POOL_GROUP_DIM), POOL_GROUP_DIM ** -0.5)
    pool_scale = 1.0 + nrm(ks[14], (DEPTH, D_POOL), 0.02)
    q_norm = 1.0 + nrm(ks[15], (DEPTH, HEAD_DIM), 0.02)
    k_norm = 1.0 + nrm(ks[16], (DEPTH, HEAD_DIM), 0.02)
    conv_w = nrm(ks[17], (DEPTH, CONV_K, 3 * D_DELTA), CONV_K ** -0.5)
    a_log = jnp.log(jax.random.uniform(ks[18], (DEPTH, 2, N_DELTA_HEADS), f32, 1.0, 16.0))
    dt = jnp.exp(jax.random.uniform(ks[19], (DEPTH, 2, N_DELTA_HEADS), f32, math.log(1e-3), math.log(1e-1)))
    dt_bias = dt + jnp.log(-jnp.expm1(-dt))
    o_norm = 1.0 + nrm(ks[20], (DEPTH, HEAD_DIM), 0.02)
    return {"x_prompt": x_prompt, "x_sample": x_sample, "cache_attn_k": cache_attn_k,
            "cache_attn_v": cache_attn_v, "state_delta": state_delta, "c": c, "c_ctx": c_ctx,
            "w_mod": w_mod, "b_mod": b_mod, "norm_pre": norm_pre, "norm_post": norm_post,
            "w_in": w_in, "w_out": w_out, "pool_w": pool_w, "pool_scale": pool_scale,
            "q_norm": q_norm, "k_norm": k_norm, "conv_w": conv_w, "a_log": a_log,
            "dt_bias": dt_bias, "o_norm": o_norm}


def reference(x_prompt, x_sample, cache_attn_k, cache_attn_v, state_delta, c, c_ctx,
              w_mod, b_mod, norm_pre, norm_post, w_in, w_out, pool_w, pool_scale,
              q_norm, k_norm, conv_w, a_log, dt_bias, o_norm):
    ctx_cond = c_ctx[None, :]
    hp = x_prompt
    hs = x_sample
    new_k, new_v, new_s = [], [], []
    for l in range(DEPTH):
        lw = (w_mod[l], b_mod[l], norm_pre[l], norm_post[l], w_in[l], w_out[l], pool_w[l], pool_scale[l],
              q_norm[l], k_norm[l], conv_w[l], a_log[l], dt_bias[l], o_norm[l])
        hp, k_l, v_l, s_l = trunk_layer(hp, ctx_cond, *lw)
        hs = trunk_layer(hs, c, *lw, ctx_k=cache_attn_k[:, l], ctx_v=cache_attn_v[:, l],
                         ctx_state=state_delta[:, l])
        new_k.append(k_l)
        new_v.append(v_l)
        new_s.append(s_l)
    new_attn_k = jnp.stack(new_k, axis=1)
    new_attn_v = jnp.stack(new_v, axis=1)
    new_state_delta = jnp.stack(new_s, axis=1)
    return (hp, hs, new_attn_k, new_attn_v, new_state_delta)
```

```python
import functools

import jax
import jax.numpy as jnp
from jax import lax
from jax.experimental import pallas as pl
from jax.experimental.pallas import tpu as pltpu

F32 = jnp.float32
BF16 = jnp.bfloat16

D_MODEL = 1024
DEPTH = 2
GRID_W = 64
HEAD_DIM = 64
D_POOL = 256
D_ATTN = 512
D_DELTA = 256
D_KV = 128
N_Q_HEADS = 8
N_DELTA_HEADS = 4
POOL_WINDOWS = (2, 4, 8, 16)
CHUNK = 64
ROPE_THETA = 10000.0
EPS = 1e-6
LANES = 128
HALO = 8
MASK_VALUE = -1e30

OFF_UP = 0
OFF_GATES = 256
OFF_QK = 1280
OFF_V = 1920
OFF_QKVD = 2048
OFF_BA = 2816
D_IN_PAD = 2944

VMEM_LIMIT = 56 * 1024 * 1024


def _cparams(*sem):
    return pltpu.CompilerParams(dimension_semantics=sem, vmem_limit_bytes=VMEM_LIMIT)


def _dot(a, b):
    return jnp.dot(a, b, preferred_element_type=F32)


def _dot_nt(a, b):
    return lax.dot_general(a, b, (((1,), (1,)), ((), ())), preferred_element_type=F32)


def _dot_tn(a, b):
    return lax.dot_general(a, b, (((0,), (0,)), ((), ())), preferred_element_type=F32)


def _split(x):
    hi = x.astype(BF16)
    return hi, (x - hi.astype(F32)).astype(BF16)


def _dot_split(a, b):
    (ah, al), (bh, bl) = a, b
    m = ah.shape[0]
    both = _dot(jnp.concatenate([ah, al], axis=0), bh)
    return both[0:m] + both[m:2 * m] + _dot(ah, bl)


def _silu(x):
    return x * jax.nn.sigmoid(x)


def _softplus(x):
    return jnp.maximum(x, 0.0) + jnp.log1p(jnp.exp(-jnp.abs(x)))


def _half_rms_scale(x, lo_mask):
    sq = x * x
    ss_lo = jnp.sum(jnp.where(lo_mask, sq, 0.0), axis=-1, keepdims=True)
    ss_hi = jnp.sum(jnp.where(lo_mask, 0.0, sq), axis=-1, keepdims=True)
    r_lo = lax.rsqrt(ss_lo * (1.0 / HEAD_DIM) + EPS)
    r_hi = lax.rsqrt(ss_hi * (1.0 / HEAD_DIM) + EPS)
    return jnp.where(lo_mask, r_lo, r_hi)


def _mod_kernel(c_ref, w_ref, b_ref, o_ref):
    s = _silu(c_ref[...])
    o_ref[0] = _dot(s.astype(BF16), w_ref[0].astype(BF16)) + b_ref[0]


def _modulation(conds, w_mod, b_mod):
    tn = 768
    return pl.pallas_call(
        _mod_kernel,
        grid=(DEPTH, 3 * D_MODEL // tn),
        in_specs=[pl.BlockSpec((8, D_MODEL), lambda l, j: (0, 0)),
                  pl.BlockSpec((1, D_MODEL, tn), lambda l, j: (l, 0, j)),
                  pl.BlockSpec((1, 1, tn), lambda l, j: (l, 0, j))],
        out_specs=pl.BlockSpec((1, 8, tn), lambda l, j: (l, 0, j)),
        out_shape=jax.ShapeDtypeStruct((DEPTH, 8, 3 * D_MODEL), F32),
        compiler_params=_cparams("parallel", "parallel"),
        name="modulation",
    )(conds, w_mod, b_mod.reshape(DEPTH, 1, 3 * D_MODEL))


def _inproj_kernel(rope, x_ref, mod_ref, gpre_ref, w_ref, qkn_ref, *rest):
    if rope:
        cos_ref, sin_ref, up_ref, gates_ref, q_ref, k2_ref, v2_ref, qkvd_ref, ba_ref = rest
    else:
        up_ref, gates_ref, q_ref, k2_ref, v2_ref, qkvd_ref, ba_ref, kn_ref, v_ref = rest
    x = x_ref[0]
    tm = x.shape[0]
    mod = mod_ref[0]
    shift = mod[:, 0:D_MODEL]
    scale = mod[:, D_MODEL:2 * D_MODEL]
    ms = jnp.mean(x * x, axis=-1, keepdims=True)
    h = (x * lax.rsqrt(ms + EPS)) * gpre_ref[...] * (1.0 + scale) + shift
    hb = h.astype(BF16)

    up_ref[0] = _dot(hb, w_ref[:, OFF_UP:OFF_UP + D_POOL])
    gates_ref[0] = _silu(_dot(hb, w_ref[:, OFF_GATES:OFF_GATES + D_MODEL]))
    qkvd_ref[0] = _dot(hb, w_ref[:, OFF_QKVD:OFF_QKVD + 3 * D_DELTA])
    ba_ref[0] = _dot(hb, w_ref[:, OFF_BA:OFF_BA + LANES])

    lane = lax.broadcasted_iota(jnp.int32, (tm, LANES), 1)
    lo_mask = lane < HEAD_DIM
    even = (lane % 2) == 0

    def head_norm_rope(blk, gain):
        y = blk * _half_rms_scale(blk, lo_mask) * gain
        if rope:
            swapped = jnp.where(even, pltpu.roll(y, LANES - 1, 1), pltpu.roll(y, 1, 1))
            return y, y * cos_ref[...] + swapped * sin_ref[...]
        return y, y

    zqk = _dot(hb, w_ref[:, OFF_QK:OFF_QK + D_ATTN + D_KV])
    for i in range(D_ATTN // LANES):
        _, qr = head_norm_rope(zqk[:, i * LANES:(i + 1) * LANES], qkn_ref[0:1, :])
        q_ref[0, :, i * LANES:(i + 1) * LANES] = (qr * (HEAD_DIM ** -0.5)).astype(BF16)
    kn, kr = head_norm_rope(zqk[:, D_ATTN:D_ATTN + D_KV], qkn_ref[1:2, :])
    k2_ref[0, :, 0:LANES] = kr.astype(BF16)
    k2_ref[0, :, LANES:2 * LANES] = pltpu.roll(kr, HEAD_DIM, 1).astype(BF16)
    v = _dot(hb, w_ref[:, OFF_V:OFF_V + D_KV])
    v2_ref[0, :, 0:LANES] = v.astype(BF16)
    v2_ref[0, :, LANES:2 * LANES] = pltpu.roll(v, HEAD_DIM, 1).astype(BF16)
    if not rope:
        kn_ref[0] = kn
        v_ref[0] = v


def _inproj(x, mod, gpre, w, qkn, rope_tabs, tm):
    B, L, _ = x.shape
    rope = rope_tabs is not None
    shared_mod = mod.shape[0] == 1
    row = lambda b, i: (b, i, 0)
    const2 = lambda b, i: (0, 0)
    in_specs = [pl.BlockSpec((1, tm, D_MODEL), row),
                pl.BlockSpec((1, 1, 3 * D_MODEL), (lambda b, i: (0, 0, 0)) if shared_mod else (lambda b, i: (b, 0, 0))),
                pl.BlockSpec((1, D_MODEL), const2),
                pl.BlockSpec((D_MODEL, D_IN_PAD), const2),
                pl.BlockSpec((8, LANES), const2)]
    args = [x, mod, gpre, w, qkn]
    if rope:
        in_specs += [pl.BlockSpec((tm, LANES), lambda b, i: (i, 0))] * 2
        args += list(rope_tabs)
    widths = [(D_POOL, F32), (D_MODEL, F32), (D_ATTN, BF16), (2 * D_KV, BF16), (2 * D_KV, BF16),
              (3 * D_DELTA, F32), (LANES, F32)]
    if not rope:
        widths += [(D_KV, F32), (D_KV, F32)]
    out_specs = [pl.BlockSpec((1, tm, wd), row) for wd, _ in widths]
    out_shape = [jax.ShapeDtypeStruct((B, L, wd), dt) for wd, dt in widths]
    return pl.pallas_call(
        functools.partial(_inproj_kernel, rope),
        grid=(B, L // tm),
        in_specs=in_specs,
        out_specs=out_specs,
        out_shape=out_shape,
        compiler_params=_cparams("parallel", "parallel"),
        name="inproj_rope" if rope else "inproj",
    )(*args)


def _attn_kernel(n_src, tk, q_ref, *rest):
    kv_refs = rest[:2 * n_src]
    o_ref, m_ref, l_ref, acc_ref = rest[2 * n_src:]
    tq = q_ref.shape[1]
    lane = lax.broadcasted_iota(jnp.int32, (tq, LANES), 1)
    lo_mask = lane < HEAD_DIM

    m_ref[...] = jnp.full(m_ref.shape, MASK_VALUE, F32)
    l_ref[...] = jnp.zeros(l_ref.shape, F32)
    acc_ref[...] = jnp.zeros(acc_ref.shape, F32)

    qh = []
    for h in range(N_Q_HEADS):
        blk = q_ref[0, :, (h // 2) * LANES:(h // 2 + 1) * LANES]
        keep = lo_mask if h % 2 == 0 else jnp.logical_not(lo_mask)
        qh.append(jnp.where(keep, blk, jnp.zeros_like(blk)))

    def chunk(k2_ref, v2_ref, start, size):
        for i in range(N_Q_HEADS // 2):
            pvs, alphas = [], []
            for h in (2 * i, 2 * i + 1):
                natural = (h // 4) == (h % 2)
                col = 0 if natural else LANES
                kblk = k2_ref[0, pl.ds(start, size), col:col + LANES]
                vblk = v2_ref[0, pl.ds(start, size), col:col + LANES]
                s = _dot_nt(qh[h], kblk)
                m_prev = m_ref[h]
                m_new = jnp.maximum(m_prev, jnp.max(s, axis=-1, keepdims=True))
                alpha = jnp.exp(m_prev - m_new)
                p = jnp.exp(s - jnp.tile(m_new, (1, size // LANES)))
                l_ref[h] = alpha * l_ref[h] + jnp.sum(p, axis=-1, keepdims=True)
                m_ref[h] = m_new
                pvs.append(_dot(p.astype(BF16), vblk))
                alphas.append(alpha)
            sl = slice(i * LANES, (i + 1) * LANES)
            acc_ref[:, sl] = (acc_ref[:, sl] * jnp.where(lo_mask, alphas[0], alphas[1])
                              + jnp.where(lo_mask, pvs[0], pvs[1]))

    for j in range(n_src):
        k2_ref, v2_ref = kv_refs[2 * j], kv_refs[2 * j + 1]
        S = k2_ref.shape[1]
        size = min(tk, S)
        n = S // size
        if n == 1:
            chunk(k2_ref, v2_ref, 0, size)
        else:
            def body(c, carry, k2_ref=k2_ref, v2_ref=v2_ref, size=size):
                chunk(k2_ref, v2_ref, pl.multiple_of(c * size, size), size)
                return carry
            lax.fori_loop(0, n, body, 0)

    for i in range(N_Q_HEADS // 2):
        sl = slice(i * LANES, (i + 1) * LANES)
        o_ref[0, :, sl] = acc_ref[:, sl] / jnp.where(lo_mask, l_ref[2 * i], l_ref[2 * i + 1])


def _attention(q, sources, tq, tk):
    B, L, _ = q.shape
    in_specs = [pl.BlockSpec((1, tq, D_ATTN), lambda b, i: (b, i, 0))]
    args = [q]
    for k2, v2 in sources:
        S = k2.shape[1]
        in_specs += [pl.BlockSpec((1, S, 2 * D_KV), lambda b, i: (b, 0, 0))] * 2
        args += [k2, v2]
    return pl.pallas_call(
        functools.partial(_attn_kernel, len(sources), tk),
        grid=(B, L // tq),
        in_specs=in_specs,
        out_specs=pl.BlockSpec((1, tq, D_ATTN), lambda b, i: (b, i, 0)),
        out_shape=jax.ShapeDtypeStruct((B, L, D_ATTN), F32),
        scratch_shapes=[pltpu.VMEM((N_Q_HEADS, tq, LANES), F32),
                        pltpu.VMEM((N_Q_HEADS, tq, LANES), F32),
                        pltpu.VMEM((tq, D_ATTN), F32)],
        compiler_params=_cparams("parallel", "parallel"),
        name="attention",
    )(*args)


def _delta_prep_kernel(cur_ref, prev_ref, next_ref, ba_ref, convw_ref, alog_ref, dtb_ref, act_ref, bg_ref, xp_ref):
    i = pl.program_id(1)
    tt = cur_ref.shape[1]
    first = i == 0
    last = i == pl.num_programs(1) - 1
    xp_ref[0:HALO, :] = jnp.where(first, 0.0, prev_ref[0])
    xp_ref[HALO:HALO + tt, :] = cur_ref[0]
    xp_ref[HALO + tt:2 * HALO + tt, :] = jnp.where(last, 0.0, next_ref[0])
    acc = convw_ref[0:1, :] * xp_ref[HALO - 2:HALO - 2 + tt, :]
    for j in range(1, 4):
        acc = acc + convw_ref[j:j + 1, :] * xp_ref[HALO - 2 + j:HALO - 2 + j + tt, :]
    act_ref[0] = _silu(acc)
    ba = ba_ref[0]
    lane = lax.broadcasted_iota(jnp.int32, ba.shape, 1)
    beta = jax.nn.sigmoid(ba)
    g = -jnp.exp(alog_ref[...]) * _softplus(ba + dtb_ref[...])
    bg_ref[0] = jnp.where(lane < 2 * N_DELTA_HEADS, beta, jnp.where(lane < 4 * N_DELTA_HEADS, g, 0.0))


def _delta_prep(qkvd, ba, conv_w, alog_row, dtb_row, tt):
    B, L, W = qkvd.shape
    nb = tt // HALO
    last_blk = L // HALO - 1
    return pl.pallas_call(
        _delta_prep_kernel,
        grid=(B, L // tt),
        in_specs=[pl.BlockSpec((1, tt, W), lambda b, i: (b, i, 0)),
                  pl.BlockSpec((1, HALO, W), lambda b, i: (b, jnp.maximum(i * nb - 1, 0), 0)),
                  pl.BlockSpec((1, HALO, W), lambda b, i: (b, jnp.minimum((i + 1) * nb, last_blk), 0)),
                  pl.BlockSpec((1, tt, LANES), lambda b, i: (b, i, 0)),
                  pl.BlockSpec((4, W), lambda b, i: (0, 0)),
                  pl.BlockSpec((1, LANES), lambda b, i: (0, 0)),
                  pl.BlockSpec((1, LANES), lambda b, i: (0, 0))],
        out_specs=[pl.BlockSpec((1, tt, W), lambda b, i: (b, i, 0)),
                   pl.BlockSpec((1, tt, LANES), lambda b, i: (b, i, 0))],
        out_shape=[jax.ShapeDtypeStruct((B, L, W), F32), jax.ShapeDtypeStruct((B, L, LANES), F32)],
        scratch_shapes=[pltpu.VMEM((tt + 2 * HALO, W), F32)],
        compiler_params=_cparams("parallel", "parallel"),
        name="delta_prep",
    )(qkvd, qkvd, qkvd, ba, conv_w, alog_row, dtb_row)


def _delta_chunk_kernel(q_ref, k_ref, v_ref, bg_ref, bgt_ref, pm_ref, bm_ref, ol_ref, e_ref):
    h = pl.program_id(1)
    C = CHUNK
    n_chunks = q_ref.shape[3] // C
    ri = lax.broadcasted_iota(jnp.int32, (C, C), 0)
    ci = lax.broadcasted_iota(jnp.int32, (C, C), 1)
    eye = (ri == ci).astype(F32)
    lane = lax.broadcasted_iota(jnp.int32, (C, LANES), 1)
    sub = lax.broadcasted_iota(jnp.int32, (2 * HALO, C), 0)

    def l2n(x):
        return x * lax.rsqrt(jnp.sum(x * x, axis=-1, keepdims=True) + EPS)

    for c in range(n_chunks):
        rows = slice(c * C, (c + 1) * C)
        q = l2n(q_ref[0, 0, 0, rows, :]) * (HEAD_DIM ** -0.5)
        k = l2n(k_ref[0, 0, 0, rows, :])
        v = v_ref[0, 0, 0, rows, :]
        kb = k.astype(BF16)
        kk = _dot_nt(kb, kb)
        qk = _dot_nt(q.astype(BF16), kb)
        bg = bg_ref[0, rows, :]
        bgt = bgt_ref[0, c]
        for d in range(2):
            u_idx = d * N_DELTA_HEADS + h
            beta_col = jnp.sum(jnp.where(lane == u_idx, bg, 0.0), axis=-1, keepdims=True)
            g_col = jnp.sum(jnp.where(lane == 2 * N_DELTA_HEADS + u_idx, bg, 0.0), axis=-1, keepdims=True)
            g_row = jnp.sum(jnp.where(sub == 2 * N_DELTA_HEADS + u_idx, bgt, 0.0), axis=0, keepdims=True)
            incl = (ri >= ci) if d == 0 else (ri <= ci)
            strict = (ri > ci) if d == 0 else (ri < ci)
            gc_col = jnp.sum(jnp.where(incl, g_row, 0.0), axis=1, keepdims=True)
            gc_row = jnp.sum(jnp.where(incl, 0.0, g_col) + jnp.where(ri == ci, g_col, 0.0), axis=0, keepdims=True)
            g_tot = jnp.sum(g_col, axis=0, keepdims=True)
            decay = jnp.where(incl, jnp.exp(jnp.where(incl, gc_col - gc_row, 0.0)), 0.0)
            attn = qk * decay
            t = jnp.where(strict, -(beta_col * kk * decay), 0.0)
            p = eye + t
            t = _dot_split(_split(t), _split(t))
            for _ in range(4):
                both = _dot_split(_split(jnp.concatenate([t, p], axis=0)), _split(t))
                t = both[0:C]
                p = p + both[C:2 * C]
            p = p + _dot_split(_split(p), _split(t))
            pb = p.astype(BF16)
            egc = jnp.exp(gc_col)
            u = _dot(pb, (v * beta_col).astype(BF16))
            w = _dot(pb, (k * (beta_col * egc)).astype(BF16))
            kdec = (k * jnp.exp(g_tot - gc_col)).astype(BF16)
            ub = u.astype(BF16)
            wb = w.astype(BF16)
            ab = attn.astype(BF16)
            bm_ref[0, d, 0, c] = _dot_tn(kdec, ub)
            mp = _dot_tn(kdec, wb)
            qt = q * egc - _dot(ab, wb)
            ol_ref[0, d, 0, c] = _dot(ab, ub)
            pm_ref[0, d, 0, c] = jnp.concatenate([mp, qt], axis=0).astype(BF16)
            e_ref[0, d, 0, c] = jnp.broadcast_to(jnp.exp(g_tot), (HALO, LANES))


def _delta_chunks(qkv, bg, bgt, tc):
    _, B, H, L, _ = qkv.shape
    N = L // CHUNK
    cb = tc // CHUNK
    qspec = lambda j: pl.BlockSpec((1, 1, 1, tc, HEAD_DIM), lambda b, h, i, j=j: (j, b, h, i, 0))
    ospec = lambda r, c: pl.BlockSpec((1, 2, 1, cb, r, c), lambda b, h, i: (b, 0, h, i, 0, 0))
    return pl.pallas_call(
        _delta_chunk_kernel,
        grid=(B, H, L // tc),
        in_specs=[qspec(0), qspec(1), qspec(2),
                  pl.BlockSpec((1, tc, LANES), lambda b, h, i: (b, i, 0)),
                  pl.BlockSpec((1, cb, 2 * HALO, CHUNK), lambda b, h, i: (b, i, 0, 0))],
        out_specs=[ospec(2 * CHUNK, HEAD_DIM), ospec(CHUNK, HEAD_DIM), ospec(CHUNK, HEAD_DIM), ospec(HALO, LANES)],
        out_shape=[jax.ShapeDtypeStruct((B, 2, H, N, 2 * CHUNK, HEAD_DIM), BF16),
                   jax.ShapeDtypeStruct((B, 2, H, N, CHUNK, HEAD_DIM), F32),
                   jax.ShapeDtypeStruct((B, 2, H, N, CHUNK, HEAD_DIM), F32),
                   jax.ShapeDtypeStruct((B, 2, H, N, HALO, LANES), F32)],
        compiler_params=_cparams("parallel", "parallel", "parallel"),
        name="delta_chunks",
    )(qkv, qkv, qkv, bg, bgt)


def _delta_scan_kernel(pmf_ref, bmf_ref, olf_ref, ef_ref, pmb_ref, bmb_ref, olb_ref, eb_ref, s0_ref,
                       of_ref, ob_ref, sout_ref, s_ref):
    n = pl.program_id(1)
    bb = s_ref.shape[0]
    C = CHUNK

    @pl.when(n == 0)
    def _():
        s_ref[...] = s0_ref[...]

    for b in range(bb):
        for d, (pm_ref, bm_ref, ol_ref, e_ref, o_ref) in enumerate(
                ((pmf_ref, bmf_ref, olf_ref, ef_ref, of_ref), (pmb_ref, bmb_ref, olb_ref, eb_ref, ob_ref))):
            for h in range(N_DELTA_HEADS):
                s = s_ref[b, d, h]
                r = _dot(pm_ref[b, 0, h, 0], s.astype(BF16))
                e = e_ref[b, 0, h, 0][0:1, 0:HEAD_DIM]
                s_ref[b, d, h] = e * s + bm_ref[b, 0, h, 0] - r[0:C]
                o_ref[b, h] = r[C:2 * C] + ol_ref[b, 0, h, 0]

    @pl.when(n == pl.num_programs(1) - 1)
    def _():
        sout_ref[...] = s_ref[...]


def _delta_scan(pm, bm, ol, e, s0, bb):
    B, _, H, N, _, _ = pm.shape
    L = N * CHUNK

    def spec(r, c, d):
        if d == 0:
            return pl.BlockSpec((bb, 1, H, 1, r, c), lambda b, n: (b, 0, 0, n, 0, 0))
        return pl.BlockSpec((bb, 1, H, 1, r, c), lambda b, n: (b, 1, 0, N - 1 - n, 0, 0))

    in_specs = []
    for d in range(2):
        in_specs += [spec(2 * CHUNK, HEAD_DIM, d), spec(CHUNK, HEAD_DIM, d), spec(CHUNK, HEAD_DIM, d), spec(HALO, LANES, d)]
    in_specs.append(pl.BlockSpec((bb, 2, H, HEAD_DIM, HEAD_DIM), lambda b, n: (b, 0, 0, 0, 0)))
    return pl.pallas_call(
        _delta_scan_kernel,
        grid=(B // bb, N),
        in_specs=in_specs,
        out_specs=[pl.BlockSpec((bb, H, CHUNK, HEAD_DIM), lambda b, n: (b, 0, n, 0)),
                   pl.BlockSpec((bb, H, CHUNK, HEAD_DIM), lambda b, n: (b, 0, N - 1 - n, 0)),
                   pl.BlockSpec((bb, 2, H, HEAD_DIM, HEAD_DIM), lambda b, n: (b, 0, 0, 0, 0))],
        out_shape=[jax.ShapeDtypeStruct((B, H, L, HEAD_DIM), F32),
                   jax.ShapeDtypeStruct((B, H, L, HEAD_DIM), F32),
                   jax.ShapeDtypeStruct((B, 2, H, HEAD_DIM, HEAD_DIM), F32)],
        scratch_shapes=[pltpu.VMEM((bb, 2, H, HEAD_DIM, HEAD_DIM), F32)],
        compiler_params=_cparams("parallel", "arbitrary"),
        name="delta_scan",
    )(pm, bm, ol, e, pm, bm, ol, e, s0)


def _merge_kernel(seq_len, x_ref, mod_ref, up_ref, upp_ref, upn_ref, gates_ref, oattn_ref, of_ref, ob_ref,
                  pw_ref, pscale_ref, onorm_ref, wout_ref, gpost_ref, o_ref, pad_ref):
    i = pl.program_id(1)
    tm = x_ref.shape[1]
    first = i == 0
    last = i == pl.num_programs(1) - 1
    pad_ref[0:HALO, :] = jnp.where(first, 0.0, upp_ref[0])
    pad_ref[HALO:HALO + tm, :] = up_ref[0]
    pad_ref[HALO + tm:2 * HALO + tm, :] = jnp.where(last, 0.0, upn_ref[0])

    def window_sum(col, lo, hi):
        acc = pad_ref[HALO + lo:HALO + lo + tm, col]
        for j in range(lo + 1, hi):
            acc = acc + pad_ref[HALO + j:HALO + j + tm, col]
        return acc

    lane = lax.broadcasted_iota(jnp.int32, (tm, LANES), 1)
    lo_mask = lane < HEAD_DIM
    t = lax.broadcasted_iota(jnp.int32, (tm, LANES), 0) + i * tm

    def count(w):
        return (jnp.minimum(t - w // 2 + w, seq_len) - jnp.maximum(t - w // 2, 0)).astype(F32)

    pooled = []
    for col_blk, (w_lo, w_hi) in enumerate(((2, 4), (8, 16))):
        col = slice(col_blk * LANES, (col_blk + 1) * LANES)
        s_lo = window_sum(col, -(w_lo // 2), w_lo // 2)
        s_hi = s_lo + window_sum(col, -(w_hi // 2), -(w_lo // 2)) + window_sum(col, w_lo // 2, w_hi // 2)
        mean = jnp.where(lo_mask, s_lo / count(w_lo), s_hi / count(w_hi))
        pooled.append(mean - up_ref[0, :, col])
    pooled = jnp.concatenate(pooled, axis=1)
    o_pool = _dot(pooled.astype(BF16), pw_ref[...]) * pscale_ref[...]

    od = of_ref[0] + ob_ref[0]
    odn = []
    for j in range(D_DELTA // LANES):
        blk = od[:, j * LANES:(j + 1) * LANES]
        odn.append(blk * _half_rms_scale(blk, lo_mask) * onorm_ref[...])
    o_delta = jnp.concatenate(odn, axis=1)

    gates = gates_ref[0]
    y = _dot((gates[:, 0:D_POOL] * o_pool).astype(BF16), wout_ref[0:D_POOL, :])
    y = y + _dot((gates[:, D_POOL:D_POOL + D_ATTN] * oattn_ref[0]).astype(BF16), wout_ref[D_POOL:D_POOL + D_ATTN, :])
    y = y + _dot((gates[:, D_POOL + D_ATTN:] * o_delta).astype(BF16), wout_ref[D_POOL + D_ATTN:, :])
    ms = jnp.mean(y * y, axis=-1, keepdims=True)
    yn = (y * lax.rsqrt(ms + EPS)) * gpost_ref[...]
    gate = mod_ref[0][:, 2 * D_MODEL:3 * D_MODEL]
    o_ref[0] = x_ref[0] + gate * yn


def _merge(x, mod, up, gates, oattn, o_f, o_b, pw_bd, pscale, onorm_row, wout, gpost, tm):
    B, L, _ = x.shape
    nb = tm // HALO
    last_blk = L // HALO - 1
    shared_mod = mod.shape[0] == 1
    row = lambda b, i: (b, i, 0)
    const2 = lambda b, i: (0, 0)
    return pl.pallas_call(
        functools.partial(_merge_kernel, L),
        grid=(B, L // tm),
        in_specs=[pl.BlockSpec((1, tm, D_MODEL), row),
                  pl.BlockSpec((1, 1, 3 * D_MODEL), (lambda b, i: (0, 0, 0)) if shared_mod else (lambda b, i: (b, 0, 0))),
                  pl.BlockSpec((1, tm, D_POOL), row),
                  pl.BlockSpec((1, HALO, D_POOL), lambda b, i: (b, jnp.maximum(i * nb - 1, 0), 0)),
                  pl.BlockSpec((1, HALO, D_POOL), lambda b, i: (b, jnp.minimum((i + 1) * nb, last_blk), 0)),
                  pl.BlockSpec((1, tm, D_MODEL), row),
                  pl.BlockSpec((1, tm, D_ATTN), row),
                  pl.BlockSpec((1, tm, D_DELTA), row),
                  pl.BlockSpec((1, tm, D_DELTA), row),
                  pl.BlockSpec((D_POOL, D_POOL), const2),
                  pl.BlockSpec((1, D_POOL), const2),
                  pl.BlockSpec((1, LANES), const2),
                  pl.BlockSpec((D_MODEL, D_MODEL), const2),
                  pl.BlockSpec((1, D_MODEL), const2)],
        out_specs=pl.BlockSpec((1, tm, D_MODEL), row),
        out_shape=jax.ShapeDtypeStruct((B, L, D_MODEL), F32),
        scratch_shapes=[pltpu.VMEM((tm + 2 * HALO, D_POOL), F32)],
        compiler_params=_cparams("parallel", "parallel"),
        name="merge",
    )(x, mod, up, up, up, gates, oattn, o_f, o_b, pw_bd, pscale, onorm_row, wout, gpost)


def _rope_tables(num_tokens):
    rows = num_tokens // GRID_W
    row = jnp.repeat(jnp.arange(rows, dtype=F32), GRID_W)
    col = (jnp.arange(rows * GRID_W) % GRID_W).astype(F32)
    axis_dim = HEAD_DIM // 2
    inv = ROPE_THETA ** (-jnp.arange(0, axis_dim, 2, dtype=F32) / axis_dim)
    ang = jnp.concatenate([row[:, None] * inv, col[:, None] * inv], axis=-1)
    cos = jnp.repeat(jnp.cos(ang), 2, axis=-1)
    sin = jnp.repeat(jnp.sin(ang), 2, axis=-1)
    sign = jnp.tile(jnp.array([-1.0, 1.0], F32), HEAD_DIM // 2)
    return jnp.tile(cos, (1, 2)), jnp.tile(sin * sign, (1, 2))


def _permute_w_in(w):
    u_pool, g_pool, q, k, v, g_attn, qkv_d, b_d, a_d, g_delta = jnp.split(
        w, [256, 512, 1024, 1152, 1280, 1792, 2560, 2568, 2576], axis=1)
    pad = jnp.zeros((w.shape[0], LANES - 4 * N_DELTA_HEADS), w.dtype)
    return jnp.concatenate([u_pool, g_pool, g_attn, g_delta, q, k, v, qkv_d, b_d, a_d, pad], axis=1).astype(BF16)


def _lane_row(vec, offset):
    return jnp.zeros((1, LANES), F32).at[0, offset:offset + vec.size].set(vec.reshape(-1))


def _rolled_pair(x):
    return jnp.concatenate([x, jnp.roll(x, HEAD_DIM, axis=-1)], axis=-1).astype(BF16)


def _layer(x, mod, lw, rope_tabs, ctx, tiles):
    B, L, _ = x.shape
    latent = ctx is not None
    outs = _inproj(x, mod, lw["gpre"], lw["w_in"], lw["qkn"], rope_tabs, tiles["tm"])
    up, gates, q, k2, v2, qkvd, ba = outs[:7]
    sources = [(k2, v2)]
    if latent:
        sources.append((ctx["k2"], ctx["v2"]))
    oattn = _attention(q, sources, tiles["tq"], tiles["tk"])

    act, bg = _delta_prep(qkvd, ba, lw["conv_w"], lw["alog_row"], lw["dtb_row"], tiles["tt"])
    qkv = act.reshape(B, L, 3, N_DELTA_HEADS, HEAD_DIM).transpose(2, 0, 3, 1, 4)
    bgt = bg[:, :, 0:2 * HALO].reshape(B, L // CHUNK, CHUNK, 2 * HALO).transpose(0, 1, 3, 2)
    pm, bm, ol, e = _delta_chunks(qkv, bg, bgt, tiles["tc"])
    s0 = ctx["state"] if latent else jnp.zeros((B, 2, N_DELTA_HEADS, HEAD_DIM, HEAD_DIM), F32)
    o_f, o_b, s_out = _delta_scan(pm, bm, ol, e, s0, tiles["bb"])
    o_f = o_f.transpose(0, 2, 1, 3).reshape(B, L, D_DELTA)
    o_b = o_b.transpose(0, 2, 1, 3).reshape(B, L, D_DELTA)

    y = _merge(x, mod, up, gates, oattn, o_f, o_b, lw["pw_bd"], lw["pscale"], lw["onorm_row"],
               lw["w_out"], lw["gpost"], tiles["tm"])
    if latent:
        return y
    kn, v = outs[7], outs[8]
    return y, kn, v, s_out


def kernel(x_prompt, x_sample, cache_attn_k, cache_attn_v, state_delta, c, c_ctx, w_mod, b_mod, norm_pre, norm_post,
           w_in, w_out, pool_w, pool_scale, q_norm, k_norm, conv_w, a_log, dt_bias, o_norm):
    B, L, _ = x_prompt.shape
    DB, DL, _ = x_sample.shape
    past = cache_attn_k.shape[2]

    conds = jnp.zeros((8, D_MODEL), F32).at[0].set(c_ctx).at[1:1 + DB].set(c)
    mod = _modulation(conds, w_mod, b_mod)
    rope_tabs = _rope_tables(DL)

    ctx_tiles = dict(tm=256, tq=256, tk=256, tt=256, tc=256, bb=8)
    lat_tiles = dict(tm=512, tq=256, tk=512, tt=512, tc=256, bb=4)

    hp, hs = x_prompt, x_sample
    new_k, new_v, new_s = [], [], []
    for l in range(DEPTH):
        blocks = jnp.zeros((N_DELTA_HEADS, HEAD_DIM, N_DELTA_HEADS, HEAD_DIM), F32)
        blocks = blocks.at[jnp.arange(4), :, jnp.arange(4), :].set(pool_w[l])
        qkn = jnp.zeros((8, LANES), F32).at[0].set(jnp.tile(q_norm[l], 2)).at[1].set(jnp.tile(k_norm[l], 2))
        lw = dict(
            gpre=norm_pre[l].reshape(1, D_MODEL),
            gpost=norm_post[l].reshape(1, D_MODEL),
            w_in=_permute_w_in(w_in[l]),
            w_out=w_out[l].astype(BF16),
            qkn=qkn,
            conv_w=conv_w[l],
            alog_row=_lane_row(a_log[l], 2 * N_DELTA_HEADS),
            dtb_row=_lane_row(dt_bias[l], 2 * N_DELTA_HEADS),
            pw_bd=blocks.reshape(D_POOL, D_POOL).astype(BF16),
            pscale=pool_scale[l].reshape(1, D_POOL),
            onorm_row=jnp.tile(o_norm[l], 2).reshape(1, LANES),
        )
        hp, k_l, v_l, s_l = _layer(hp, mod[l, 0:1].reshape(1, 1, 3 * D_MODEL), lw, None, None, ctx_tiles)
        ctx = dict(k2=_rolled_pair(cache_attn_k[:, l].reshape(DB, past, D_KV)),
                   v2=_rolled_pair(cache_attn_v[:, l].reshape(DB, past, D_KV)),
                   state=state_delta[:, l])
        hs = _layer(hs, mod[l, 1:1 + DB].reshape(DB, 1, 3 * D_MODEL), lw, rope_tabs, ctx, lat_tiles)
        new_k.append(k_l.reshape(B, L, 2, HEAD_DIM))
        new_v.append(v_l.reshape(B, L, 2, HEAD_DIM))
        new_s.append(s_l)
    return (hp, hs, jnp.stack(new_k, axis=1), jnp.stack(new_v, axis=1), jnp.stack(new_s, axis=1))
```

```python
import functools

import jax
import jax.numpy as jnp
from jax import lax
from jax.experimental import pallas as pl
from jax.experimental.pallas import tpu as pltpu

F32 = jnp.float32
BF16 = jnp.bfloat16

D_MODEL = 1024
DEPTH = 2
GRID_W = 64
HEAD_DIM = 64
D_POOL = 256
D_ATTN = 512
D_DELTA = 256
D_KV = 128
N_Q_HEADS = 8
N_DELTA_HEADS = 4
POOL_WINDOWS = (2, 4, 8, 16)
CHUNK = 64
ROPE_THETA = 10000.0
EPS = 1e-6
LANES = 128
HALO = 8
MASK_VALUE = -1e30

OFF_UP = 0
OFF_GATES = 256
OFF_QK = 1280
OFF_V = 1920
OFF_QKVD = 2048
OFF_BA = 2816
D_IN_PAD = 2944

VMEM_LIMIT = 56 * 1024 * 1024


def _cparams(*sem):
    return pltpu.CompilerParams(dimension_semantics=sem, vmem_limit_bytes=VMEM_LIMIT)


def _dot(a, b):
    return jnp.dot(a, b, preferred_element_type=F32)


def _dot_nt(a, b):
    return lax.dot_general(a, b, (((1,), (1,)), ((), ())), preferred_element_type=F32)


def _dot_tn(a, b):
    return lax.dot_general(a, b, (((0,), (0,)), ((), ())), preferred_element_type=F32)


def _split(x):
    hi = x.astype(BF16)
    return hi, (x - hi.astype(F32)).astype(BF16)


def _dot_split(a, b):
    (ah, al), (bh, bl) = a, b
    m = ah.shape[0]
    both = _dot(jnp.concatenate([ah, al], axis=0), bh)
    return both[0:m] + both[m:2 * m] + _dot(ah, bl)


def _silu(x):
    return x * jax.nn.sigmoid(x)


def _softplus(x):
    return jnp.maximum(x, 0.0) + jnp.log1p(jnp.exp(-jnp.abs(x)))


def _half_rms_scale(x, lo_mask):
    sq = x * x
    ss_lo = jnp.sum(jnp.where(lo_mask, sq, 0.0), axis=-1, keepdims=True)
    ss_hi = jnp.sum(jnp.where(lo_mask, 0.0, sq), axis=-1, keepdims=True)
    r_lo = lax.rsqrt(ss_lo * (1.0 / HEAD_DIM) + EPS)
    r_hi = lax.rsqrt(ss_hi * (1.0 / HEAD_DIM) + EPS)
    return jnp.where(lo_mask, r_lo, r_hi)


def _mod_kernel(c_ref, w_ref, b_ref, o_ref):
    s = _silu(c_ref[...])
    o_ref[0] = _dot(s.astype(BF16), w_ref[0].astype(BF16)) + b_ref[0]


def _modulation(conds, w_mod, b_mod):
    tn = 768
    return pl.pallas_call(
        _mod_kernel,
        grid=(DEPTH, 3 * D_MODEL // tn),
        in_specs=[pl.BlockSpec((8, D_MODEL), lambda l, j: (0, 0)),
                  pl.BlockSpec((1, D_MODEL, tn), lambda l, j: (l, 0, j)),
                  pl.BlockSpec((1, 1, tn), lambda l, j: (l, 0, j))],
        out_specs=pl.BlockSpec((1, 8, tn), lambda l, j: (l, 0, j)),
        out_shape=jax.ShapeDtypeStruct((DEPTH, 8, 3 * D_MODEL), F32),
        compiler_params=_cparams("parallel", "parallel"),
        name="modulation",
    )(conds, w_mod, b_mod.reshape(DEPTH, 1, 3 * D_MODEL))


def _inproj_kernel(rope, x_ref, mod_ref, gpre_ref, w_ref, qkn_ref, *rest):
    if rope:
        cos_ref, sin_ref, up_ref, gates_ref, q_ref, k2_ref, v2_ref, qkvd_ref, ba_ref = rest
    else:
        up_ref, gates_ref, q_ref, k2_ref, v2_ref, qkvd_ref, ba_ref, kn_ref, v_ref = rest
    x = x_ref[0]
    tm = x.shape[0]
    mod = mod_ref[0]
    shift = mod[:, 0:D_MODEL]
    scale = mod[:, D_MODEL:2 * D_MODEL]
    ms = jnp.mean(x * x, axis=-1, keepdims=True)
    h = (x * lax.rsqrt(ms + EPS)) * gpre_ref[...] * (1.0 + scale) + shift
    hb = h.astype(BF16)

    up_ref[0] = _dot(hb, w_ref[:, OFF_UP:OFF_UP + D_POOL])
    gates_ref[0] = _silu(_dot(hb, w_ref[:, OFF_GATES:OFF_GATES + D_MODEL]))
    qkvd_ref[0] = _dot(hb, w_ref[:, OFF_QKVD:OFF_QKVD + 3 * D_DELTA])
    ba_ref[0] = _dot(hb, w_ref[:, OFF_BA:OFF_BA + LANES])

    lane = lax.broadcasted_iota(jnp.int32, (tm, LANES), 1)
    lo_mask = lane < HEAD_DIM
    even = (lane % 2) == 0

    def head_norm_rope(blk, gain):
        y = blk * _half_rms_scale(blk, lo_mask) * gain
        if rope:
            swapped = jnp.where(even, pltpu.roll(y, LANES - 1, 1), pltpu.roll(y, 1, 1))
            return y, y * cos_ref[...] + swapped * sin_ref[...]
        return y, y

    zqk = _dot(hb, w_ref[:, OFF_QK:OFF_QK + D_ATTN + D_KV])
    for i in range(D_ATTN // LANES):
        _, qr = head_norm_rope(zqk[:, i * LANES:(i + 1) * LANES], qkn_ref[0:1, :])
        q_ref[0, :, i * LANES:(i + 1) * LANES] = (qr * (HEAD_DIM ** -0.5)).astype(BF16)
    kn, kr = head_norm_rope(zqk[:, D_ATTN:D_ATTN + D_KV], qkn_ref[1:2, :])
    k2_ref[0, :, 0:LANES] = kr.astype(BF16)
    k2_ref[0, :, LANES:2 * LANES] = pltpu.roll(kr, HEAD_DIM, 1).astype(BF16)
    v = _dot(hb, w_ref[:, OFF_V:OFF_V + D_KV])
    v2_ref[0, :, 0:LANES] = v.astype(BF16)
    v2_ref[0, :, LANES:2 * LANES] = pltpu.roll(v, HEAD_DIM, 1).astype(BF16)
    if not rope:
        kn_ref[0] = kn
        v_ref[0] = v


def _inproj(x, mod, gpre, w, qkn, rope_tabs, tm):
    B, L, _ = x.shape
    rope = rope_tabs is not None
    shared_mod = mod.shape[0] == 1
    row = lambda b, i: (b, i, 0)
    const2 = lambda b, i: (0, 0)
    in_specs = [pl.BlockSpec((1, tm, D_MODEL), row),
                pl.BlockSpec((1, 1, 3 * D_MODEL), (lambda b, i: (0, 0, 0)) if shared_mod else (lambda b, i: (b, 0, 0))),
                pl.BlockSpec((1, D_MODEL), const2),
                pl.BlockSpec((D_MODEL, D_IN_PAD), const2),
                pl.BlockSpec((8, LANES), const2)]
    args = [x, mod, gpre, w, qkn]
    if rope:
        in_specs += [pl.BlockSpec((tm, LANES), lambda b, i: (i, 0))] * 2
        args += list(rope_tabs)
    widths = [(D_POOL, F32), (D_MODEL, F32), (D_ATTN, BF16), (2 * D_KV, BF16), (2 * D_KV, BF16),
              (3 * D_DELTA, F32), (LANES, F32)]
    if not rope:
        widths += [(D_KV, F32), (D_KV, F32)]
    out_specs = [pl.BlockSpec((1, tm, wd), row) for wd, _ in widths]
    out_shape = [jax.ShapeDtypeStruct((B, L, wd), dt) for wd, dt in widths]
    return pl.pallas_call(
        functools.partial(_inproj_kernel, rope),
        grid=(B, L // tm),
        in_specs=in_specs,
        out_specs=out_specs,
        out_shape=out_shape,
        compiler_params=_cparams("parallel", "parallel"),
        name="inproj_rope" if rope else "inproj",
    )(*args)


_HEAD_GROUPS = ((0, 2, 5, 7), (1, 3, 4, 6))


def _attn_kernel(n_src, tk, q_ref, *rest):
    kv_refs = rest[:2 * n_src]
    o_ref, m_ref, l_ref, acc_ref = rest[2 * n_src:]
    tq = q_ref.shape[1]
    lane = lax.broadcasted_iota(jnp.int32, (tq, LANES), 1)
    lo_mask = lane < HEAD_DIM

    m_ref[...] = jnp.full(m_ref.shape, MASK_VALUE, F32)
    l_ref[...] = jnp.zeros(l_ref.shape, F32)
    acc_ref[...] = jnp.zeros(acc_ref.shape, F32)

    qg = []
    for grp in _HEAD_GROUPS:
        parts = []
        for h in grp:
            blk = q_ref[0, :, (h // 2) * LANES:(h // 2 + 1) * LANES]
            keep = lo_mask if h % 2 == 0 else jnp.logical_not(lo_mask)
            parts.append(jnp.where(keep, blk, jnp.zeros_like(blk)))
        qg.append(jnp.concatenate(parts, axis=0))

    def chunk(k2_ref, v2_ref, start, size):
        cols = [slice(g * LANES, (g + 1) * LANES) for g in range(2)]
        s = [_dot_nt(qg[g], k2_ref[0, pl.ds(start, size), cols[g]]) for g in range(2)]
        m_prev = [m_ref[g] for g in range(2)]
        m_new = [jnp.maximum(m_prev[g], jnp.max(s[g], axis=-1, keepdims=True)) for g in range(2)]
        alpha = [jnp.exp(m_prev[g] - m_new[g]) for g in range(2)]
        p = [jnp.exp(s[g] - jnp.tile(m_new[g], (1, size // LANES))) for g in range(2)]
        for g in range(2):
            l_ref[g] = alpha[g] * l_ref[g] + jnp.sum(p[g], axis=-1, keepdims=True)
            m_ref[g] = m_new[g]
        pv = [_dot(p[g].astype(BF16), v2_ref[0, pl.ds(start, size), cols[g]]) for g in range(2)]
        for g in range(2):
            acc_ref[g] = alpha[g] * acc_ref[g] + pv[g]

    for j in range(n_src):
        k2_ref, v2_ref = kv_refs[2 * j], kv_refs[2 * j + 1]
        S = k2_ref.shape[1]
        size = min(tk, S)
        n = S // size
        if n == 1:
            chunk(k2_ref, v2_ref, 0, size)
        else:
            def body(c, carry, k2_ref=k2_ref, v2_ref=v2_ref, size=size):
                chunk(k2_ref, v2_ref, pl.multiple_of(c * size, size), size)
                return carry
            lax.fori_loop(0, n, body, 0)

    def head_rows(ref, h):
        g = 0 if h in _HEAD_GROUPS[0] else 1
        r = _HEAD_GROUPS[g].index(h) * tq
        return ref[g, r:r + tq, :]

    for i in range(N_Q_HEADS // 2):
        even = head_rows(acc_ref, 2 * i) / head_rows(l_ref, 2 * i)
        odd = head_rows(acc_ref, 2 * i + 1) / head_rows(l_ref, 2 * i + 1)
        o_ref[0, :, i * LANES:(i + 1) * LANES] = jnp.where(lo_mask, even, odd)


def _attention(q, sources, tq, tk):
    B, L, _ = q.shape
    in_specs = [pl.BlockSpec((1, tq, D_ATTN), lambda b, i: (b, i, 0))]
    args = [q]
    for k2, v2 in sources:
        S = k2.shape[1]
        in_specs += [pl.BlockSpec((1, S, 2 * D_KV), lambda b, i: (b, 0, 0))] * 2
        args += [k2, v2]
    return pl.pallas_call(
        functools.partial(_attn_kernel, len(sources), tk),
        grid=(B, L // tq),
        in_specs=in_specs,
        out_specs=pl.BlockSpec((1, tq, D_ATTN), lambda b, i: (b, i, 0)),
        out_shape=jax.ShapeDtypeStruct((B, L, D_ATTN), F32),
        scratch_shapes=[pltpu.VMEM((2, N_Q_HEADS // 2 * tq, LANES), F32)] * 3,
        compiler_params=_cparams("parallel", "parallel"),
        name="attention",
    )(*args)


def _delta_prep_kernel(cur_ref, prev_ref, next_ref, ba_ref, convw_ref, alog_ref, dtb_ref, act_ref, bg_ref, xp_ref):
    i = pl.program_id(1)
    tt = cur_ref.shape[1]
    first = i == 0
    last = i == pl.num_programs(1) - 1
    xp_ref[0:HALO, :] = jnp.where(first, 0.0, prev_ref[0])
    xp_ref[HALO:HALO + tt, :] = cur_ref[0]
    xp_ref[HALO + tt:2 * HALO + tt, :] = jnp.where(last, 0.0, next_ref[0])
    acc = convw_ref[0:1, :] * xp_ref[HALO - 2:HALO - 2 + tt, :]
    for j in range(1, 4):
        acc = acc + convw_ref[j:j + 1, :] * xp_ref[HALO - 2 + j:HALO - 2 + j + tt, :]
    act_ref[0] = _silu(acc)
    ba = ba_ref[0]
    lane = lax.broadcasted_iota(jnp.int32, ba.shape, 1)
    beta = jax.nn.sigmoid(ba)
    g = -jnp.exp(alog_ref[...]) * _softplus(ba + dtb_ref[...])
    bg_ref[0] = jnp.where(lane < 2 * N_DELTA_HEADS, beta, jnp.where(lane < 4 * N_DELTA_HEADS, g, 0.0))


def _delta_prep(qkvd, ba, conv_w, alog_row, dtb_row, tt):
    B, L, W = qkvd.shape
    nb = tt // HALO
    last_blk = L // HALO - 1
    return pl.pallas_call(
        _delta_prep_kernel,
        grid=(B, L // tt),
        in_specs=[pl.BlockSpec((1, tt, W), lambda b, i: (b, i, 0)),
                  pl.BlockSpec((1, HALO, W), lambda b, i: (b, jnp.maximum(i * nb - 1, 0), 0)),
                  pl.BlockSpec((1, HALO, W), lambda b, i: (b, jnp.minimum((i + 1) * nb, last_blk), 0)),
                  pl.BlockSpec((1, tt, LANES), lambda b, i: (b, i, 0)),
                  pl.BlockSpec((4, W), lambda b, i: (0, 0)),
                  pl.BlockSpec((1, LANES), lambda b, i: (0, 0)),
                  pl.BlockSpec((1, LANES), lambda b, i: (0, 0))],
        out_specs=[pl.BlockSpec((1, tt, W), lambda b, i: (b, i, 0)),
                   pl.BlockSpec((1, tt, LANES), lambda b, i: (b, i, 0))],
        out_shape=[jax.ShapeDtypeStruct((B, L, W), F32), jax.ShapeDtypeStruct((B, L, LANES), F32)],
        scratch_shapes=[pltpu.VMEM((tt + 2 * HALO, W), F32)],
        compiler_params=_cparams("parallel", "parallel"),
        name="delta_prep",
    )(qkvd, qkvd, qkvd, ba, conv_w, alog_row, dtb_row)


def _delta_chunk_kernel(q_ref, k_ref, v_ref, bg_ref, bgt_ref, pm_ref, bm_ref, ol_ref, e_ref):
    h = pl.program_id(1)
    C = CHUNK
    n_chunks = q_ref.shape[3] // C
    ri = lax.broadcasted_iota(jnp.int32, (C, C), 0)
    ci = lax.broadcasted_iota(jnp.int32, (C, C), 1)
    eye = (ri == ci).astype(F32)
    lane = lax.broadcasted_iota(jnp.int32, (C, LANES), 1)
    sub = lax.broadcasted_iota(jnp.int32, (2 * HALO, C), 0)

    def l2n(x):
        return x * lax.rsqrt(jnp.sum(x * x, axis=-1, keepdims=True) + EPS)

    units = []
    for c in range(n_chunks):
        rows = slice(c * C, (c + 1) * C)
        q = l2n(q_ref[0, 0, 0, rows, :]) * (HEAD_DIM ** -0.5)
        k = l2n(k_ref[0, 0, 0, rows, :])
        v = v_ref[0, 0, 0, rows, :]
        kb = k.astype(BF16)
        kk = _dot_nt(kb, kb)
        qk = _dot_nt(q.astype(BF16), kb)
        bg = bg_ref[0, rows, :]
        bgt = bgt_ref[0, c]
        for d in range(2):
            u_idx = d * N_DELTA_HEADS + h
            beta_col = jnp.sum(jnp.where(lane == u_idx, bg, 0.0), axis=-1, keepdims=True)
            g_col = jnp.sum(jnp.where(lane == 2 * N_DELTA_HEADS + u_idx, bg, 0.0), axis=-1, keepdims=True)
            g_row = jnp.sum(jnp.where(sub == 2 * N_DELTA_HEADS + u_idx, bgt, 0.0), axis=0, keepdims=True)
            incl = (ri >= ci) if d == 0 else (ri <= ci)
            strict = (ri > ci) if d == 0 else (ri < ci)
            gc_col = jnp.sum(jnp.where(incl, g_row, 0.0), axis=1, keepdims=True)
            gc_row = jnp.sum(jnp.where(incl, 0.0, g_col) + jnp.where(ri == ci, g_col, 0.0), axis=0, keepdims=True)
            g_tot = jnp.sum(g_col, axis=0, keepdims=True)
            decay = jnp.where(incl, jnp.exp(jnp.where(incl, gc_col - gc_row, 0.0)), 0.0)
            t = jnp.where(strict, -(beta_col * kk * decay), 0.0)
            units.append(dict(c=c, d=d, q=q, k=k, v=v, beta=beta_col, gc=gc_col, g_tot=g_tot,
                              attn=(qk * decay).astype(BF16), t=t, p=eye + t))

    for un in units:
        un["t"] = _dot_split(_split(un["t"]), _split(un["t"]))
    for _ in range(4):
        for un in units:
            both = _dot_split(_split(jnp.concatenate([un["t"], un["p"]], axis=0)), _split(un["t"]))
            un["t"] = both[0:C]
            un["p"] = un["p"] + both[C:2 * C]
    for un in units:
        un["p"] = (un["p"] + _dot_split(_split(un["p"]), _split(un["t"]))).astype(BF16)
    for un in units:
        un["egc"] = jnp.exp(un["gc"])
        un["u"] = _dot(un["p"], (un["v"] * un["beta"]).astype(BF16)).astype(BF16)
        un["w"] = _dot(un["p"], (un["k"] * (un["beta"] * un["egc"])).astype(BF16)).astype(BF16)
    for un in units:
        c, d = un["c"], un["d"]
        kdec = (un["k"] * jnp.exp(un["g_tot"] - un["gc"])).astype(BF16)
        bm_ref[0, d, 0, c] = _dot_tn(kdec, un["u"])
        mp = _dot_tn(kdec, un["w"])
        qt = un["q"] * un["egc"] - _dot(un["attn"], un["w"])
        ol_ref[0, d, 0, c] = _dot(un["attn"], un["u"])
        pm_ref[0, d, 0, c] = jnp.concatenate([mp, qt], axis=0).astype(BF16)
        e_ref[0, d, 0, c] = jnp.broadcast_to(jnp.exp(un["g_tot"]), (HALO, LANES))


def _delta_chunks(qkv, bg, bgt, tc):
    _, B, H, L, _ = qkv.shape
    N = L // CHUNK
    cb = tc // CHUNK
    qspec = lambda j: pl.BlockSpec((1, 1, 1, tc, HEAD_DIM), lambda b, h, i, j=j: (j, b, h, i, 0))
    ospec = lambda r, c: pl.BlockSpec((1, 2, 1, cb, r, c), lambda b, h, i: (b, 0, h, i, 0, 0))
    return pl.pallas_call(
        _delta_chunk_kernel,
        grid=(B, H, L // tc),
        in_specs=[qspec(0), qspec(1), qspec(2),
                  pl.BlockSpec((1, tc, LANES), lambda b, h, i: (b, i, 0)),
                  pl.BlockSpec((1, cb, 2 * HALO, CHUNK), lambda b, h, i: (b, i, 0, 0))],
        out_specs=[ospec(2 * CHUNK, HEAD_DIM), ospec(CHUNK, HEAD_DIM), ospec(CHUNK, HEAD_DIM), ospec(HALO, LANES)],
        out_shape=[jax.ShapeDtypeStruct((B, 2, H, N, 2 * CHUNK, HEAD_DIM), BF16),
                   jax.ShapeDtypeStruct((B, 2, H, N, CHUNK, HEAD_DIM), F32),
                   jax.ShapeDtypeStruct((B, 2, H, N, CHUNK, HEAD_DIM), F32),
                   jax.ShapeDtypeStruct((B, 2, H, N, HALO, LANES), F32)],
        compiler_params=_cparams("parallel", "parallel", "parallel"),
        name="delta_chunks",
    )(qkv, qkv, qkv, bg, bgt)


def _delta_scan_kernel(pmf_ref, bmf_ref, olf_ref, ef_ref, pmb_ref, bmb_ref, olb_ref, eb_ref, s0_ref,
                       of_ref, ob_ref, sout_ref, s_ref):
    n = pl.program_id(1)
    bb = s_ref.shape[0]
    C = CHUNK

    @pl.when(n == 0)
    def _():
        s_ref[...] = s0_ref[...]

    for b in range(bb):
        for d, (pm_ref, bm_ref, ol_ref, e_ref, o_ref) in enumerate(
                ((pmf_ref, bmf_ref, olf_ref, ef_ref, of_ref), (pmb_ref, bmb_ref, olb_ref, eb_ref, ob_ref))):
            for h in range(N_DELTA_HEADS):
                s = s_ref[b, d, h]
                r = _dot(pm_ref[b, 0, h, 0], s.astype(BF16))
                e = e_ref[b, 0, h, 0][0:1, 0:HEAD_DIM]
                s_ref[b, d, h] = e * s + bm_ref[b, 0, h, 0] - r[0:C]
                o_ref[b, h] = r[C:2 * C] + ol_ref[b, 0, h, 0]

    @pl.when(n == pl.num_programs(1) - 1)
    def _():
        sout_ref[...] = s_ref[...]


def _delta_scan(pm, bm, ol, e, s0, bb):
    B, _, H, N, _, _ = pm.shape
    L = N * CHUNK

    def spec(r, c, d):
        if d == 0:
            return pl.BlockSpec((bb, 1, H, 1, r, c), lambda b, n: (b, 0, 0, n, 0, 0))
        return pl.BlockSpec((bb, 1, H, 1, r, c), lambda b, n: (b, 1, 0, N - 1 - n, 0, 0))

    in_specs = []
    for d in range(2):
        in_specs += [spec(2 * CHUNK, HEAD_DIM, d), spec(CHUNK, HEAD_DIM, d), spec(CHUNK, HEAD_DIM, d), spec(HALO, LANES, d)]
    in_specs.append(pl.BlockSpec((bb, 2, H, HEAD_DIM, HEAD_DIM), lambda b, n: (b, 0, 0, 0, 0)))
    return pl.pallas_call(
        _delta_scan_kernel,
        grid=(B // bb, N),
        in_specs=in_specs,
        out_specs=[pl.BlockSpec((bb, H, CHUNK, HEAD_DIM), lambda b, n: (b, 0, n, 0)),
                   pl.BlockSpec((bb, H, CHUNK, HEAD_DIM), lambda b, n: (b, 0, N - 1 - n, 0)),
                   pl.BlockSpec((bb, 2, H, HEAD_DIM, HEAD_DIM), lambda b, n: (b, 0, 0, 0, 0))],
        out_shape=[jax.ShapeDtypeStruct((B, H, L, HEAD_DIM), F32),
                   jax.ShapeDtypeStruct((B, H, L, HEAD_DIM), F32),
                   jax.ShapeDtypeStruct((B, 2, H, HEAD_DIM, HEAD_DIM), F32)],
        scratch_shapes=[pltpu.VMEM((bb, 2, H, HEAD_DIM, HEAD_DIM), F32)],
        compiler_params=_cparams("parallel", "arbitrary"),
        name="delta_scan",
    )(pm, bm, ol, e, pm, bm, ol, e, s0)


def _merge_kernel(seq_len, x_ref, mod_ref, up_ref, upp_ref, upn_ref, gates_ref, oattn_ref, of_ref, ob_ref,
                  pw_ref, pscale_ref, onorm_ref, wout_ref, gpost_ref, o_ref, pad_ref):
    i = pl.program_id(1)
    tm = x_ref.shape[1]
    first = i == 0
    last = i == pl.num_programs(1) - 1
    pad_ref[0:HALO, :] = jnp.where(first, 0.0, upp_ref[0])
    pad_ref[HALO:HALO + tm, :] = up_ref[0]
    pad_ref[HALO + tm:2 * HALO + tm, :] = jnp.where(last, 0.0, upn_ref[0])

    def window_sum(col, lo, hi):
        acc = pad_ref[HALO + lo:HALO + lo + tm, col]
        for j in range(lo + 1, hi):
            acc = acc + pad_ref[HALO + j:HALO + j + tm, col]
        return acc

    lane = lax.broadcasted_iota(jnp.int32, (tm, LANES), 1)
    lo_mask = lane < HEAD_DIM
    t = lax.broadcasted_iota(jnp.int32, (tm, LANES), 0) + i * tm

    def count(w):
        return (jnp.minimum(t - w // 2 + w, seq_len) - jnp.maximum(t - w // 2, 0)).astype(F32)

    pooled = []
    for col_blk, (w_lo, w_hi) in enumerate(((2, 4), (8, 16))):
        col = slice(col_blk * LANES, (col_blk + 1) * LANES)
        s_lo = window_sum(col, -(w_lo // 2), w_lo // 2)
        s_hi = s_lo + window_sum(col, -(w_hi // 2), -(w_lo // 2)) + window_sum(col, w_lo // 2, w_hi // 2)
        mean = jnp.where(lo_mask, s_lo / count(w_lo), s_hi / count(w_hi))
        pooled.append(mean - up_ref[0, :, col])
    pooled = jnp.concatenate(pooled, axis=1)
    o_pool = _dot(pooled.astype(BF16), pw_ref[...]) * pscale_ref[...]

    od = of_ref[0] + ob_ref[0]
    odn = []
    for j in range(D_DELTA // LANES):
        blk = od[:, j * LANES:(j + 1) * LANES]
        odn.append(blk * _half_rms_scale(blk, lo_mask) * onorm_ref[...])
    o_delta = jnp.concatenate(odn, axis=1)

    gates = gates_ref[0]
    y = _dot((gates[:, 0:D_POOL] * o_pool).astype(BF16), wout_ref[0:D_POOL, :])
    y = y + _dot((gates[:, D_POOL:D_POOL + D_ATTN] * oattn_ref[0]).astype(BF16), wout_ref[D_POOL:D_POOL + D_ATTN, :])
    y = y + _dot((gates[:, D_POOL + D_ATTN:] * o_delta).astype(BF16), wout_ref[D_POOL + D_ATTN:, :])
    ms = jnp.mean(y * y, axis=-1, keepdims=True)
    yn = (y * lax.rsqrt(ms + EPS)) * gpost_ref[...]
    gate = mod_ref[0][:, 2 * D_MODEL:3 * D_MODEL]
    o_ref[0] = x_ref[0] + gate * yn


def _merge(x, mod, up, gates, oattn, o_f, o_b, pw_bd, pscale, onorm_row, wout, gpost, tm):
    B, L, _ = x.shape
    nb = tm // HALO
    last_blk = L // HALO - 1
    shared_mod = mod.shape[0] == 1
    row = lambda b, i: (b, i, 0)
    const2 = lambda b, i: (0, 0)
    return pl.pallas_call(
        functools.partial(_merge_kernel, L),
        grid=(B, L // tm),
        in_specs=[pl.BlockSpec((1, tm, D_MODEL), row),
                  pl.BlockSpec((1, 1, 3 * D_MODEL), (lambda b, i: (0, 0, 0)) if shared_mod else (lambda b, i: (b, 0, 0))),
                  pl.BlockSpec((1, tm, D_POOL), row),
                  pl.BlockSpec((1, HALO, D_POOL), lambda b, i: (b, jnp.maximum(i * nb - 1, 0), 0)),
                  pl.BlockSpec((1, HALO, D_POOL), lambda b, i: (b, jnp.minimum((i + 1) * nb, last_blk), 0)),
                  pl.BlockSpec((1, tm, D_MODEL), row),
                  pl.BlockSpec((1, tm, D_ATTN), row),
                  pl.BlockSpec((1, tm, D_DELTA), row),
                  pl.BlockSpec((1, tm, D_DELTA), row),
                  pl.BlockSpec((D_POOL, D_POOL), const2),
                  pl.BlockSpec((1, D_POOL), const2),
                  pl.BlockSpec((1, LANES), const2),
                  pl.BlockSpec((D_MODEL, D_MODEL), const2),
                  pl.BlockSpec((1, D_MODEL), const2)],
        out_specs=pl.BlockSpec((1, tm, D_MODEL), row),
        out_shape=jax.ShapeDtypeStruct((B, L, D_MODEL), F32),
        scratch_shapes=[pltpu.VMEM((tm + 2 * HALO, D_POOL), F32)],
        compiler_params=_cparams("parallel", "parallel"),
        name="merge",
    )(x, mod, up, up, up, gates, oattn, o_f, o_b, pw_bd, pscale, onorm_row, wout, gpost)


def _rope_tables(num_tokens):
    rows = num_tokens // GRID_W
    row = jnp.repeat(jnp.arange(rows, dtype=F32), GRID_W)
    col = (jnp.arange(rows * GRID_W) % GRID_W).astype(F32)
    axis_dim = HEAD_DIM // 2
    inv = ROPE_THETA ** (-jnp.arange(0, axis_dim, 2, dtype=F32) / axis_dim)
    ang = jnp.concatenate([row[:, None] * inv, col[:, None] * inv], axis=-1)
    cos = jnp.repeat(jnp.cos(ang), 2, axis=-1)
    sin = jnp.repeat(jnp.sin(ang), 2, axis=-1)
    sign = jnp.tile(jnp.array([-1.0, 1.0], F32), HEAD_DIM // 2)
    return jnp.tile(cos, (1, 2)), jnp.tile(sin * sign, (1, 2))


def _permute_w_in(w):
    u_pool, g_pool, q, k, v, g_attn, qkv_d, b_d, a_d, g_delta = jnp.split(
        w, [256, 512, 1024, 1152, 1280, 1792, 2560, 2568, 2576], axis=1)
    pad = jnp.zeros((w.shape[0], LANES - 4 * N_DELTA_HEADS), w.dtype)
    return jnp.concatenate([u_pool, g_pool, g_attn, g_delta, q, k, v, qkv_d, b_d, a_d, pad], axis=1).astype(BF16)


def _lane_row(vec, offset):
    return jnp.zeros((1, LANES), F32).at[0, offset:offset + vec.size].set(vec.reshape(-1))


def _rolled_pair(x):
    return jnp.concatenate([x, jnp.roll(x, HEAD_DIM, axis=-1)], axis=-1).astype(BF16)


def _layer(x, mod, lw, rope_tabs, ctx, tiles):
    B, L, _ = x.shape
    latent = ctx is not None
    outs = _inproj(x, mod, lw["gpre"], lw["w_in"], lw["qkn"], rope_tabs, tiles["tm"])
    up, gates, q, k2, v2, qkvd, ba = outs[:7]
    sources = [(k2, v2)]
    if latent:
        sources.append((ctx["k2"], ctx["v2"]))
    oattn = _attention(q, sources, tiles["tq"], tiles["tk"])

    act, bg = _delta_prep(qkvd, ba, lw["conv_w"], lw["alog_row"], lw["dtb_row"], tiles["tt"])
    qkv = act.reshape(B, L, 3, N_DELTA_HEADS, HEAD_DIM).transpose(2, 0, 3, 1, 4)
    bgt = bg[:, :, 0:2 * HALO].reshape(B, L // CHUNK, CHUNK, 2 * HALO).transpose(0, 1, 3, 2)
    pm, bm, ol, e = _delta_chunks(qkv, bg, bgt, tiles["tc"])
    s0 = ctx["state"] if latent else jnp.zeros((B, 2, N_DELTA_HEADS, HEAD_DIM, HEAD_DIM), F32)
    o_f, o_b, s_out = _delta_scan(pm, bm, ol, e, s0, tiles["bb"])
    o_f = o_f.transpose(0, 2, 1, 3).reshape(B, L, D_DELTA)
    o_b = o_b.transpose(0, 2, 1, 3).reshape(B, L, D_DELTA)

    y = _merge(x, mod, up, gates, oattn, o_f, o_b, lw["pw_bd"], lw["pscale"], lw["onorm_row"],
               lw["w_out"], lw["gpost"], tiles["tm"])
    if latent:
        return y
    kn, v = outs[7], outs[8]
    return y, kn, v, s_out


def kernel(x_prompt, x_sample, cache_attn_k, cache_attn_v, state_delta, c, c_ctx, w_mod, b_mod, norm_pre, norm_post,
           w_in, w_out, pool_w, pool_scale, q_norm, k_norm, conv_w, a_log, dt_bias, o_norm):
    B, L, _ = x_prompt.shape
    DB, DL, _ = x_sample.shape
    past = cache_attn_k.shape[2]

    conds = jnp.zeros((8, D_MODEL), F32).at[0].set(c_ctx).at[1:1 + DB].set(c)
    mod = _modulation(conds, w_mod, b_mod)
    rope_tabs = _rope_tables(DL)

    ctx_tiles = dict(tm=256, tq=256, tk=256, tt=256, tc=256, bb=8)
    lat_tiles = dict(tm=512, tq=256, tk=512, tt=512, tc=256, bb=4)

    hp, hs = x_prompt, x_sample
    new_k, new_v, new_s = [], [], []
    for l in range(DEPTH):
        blocks = jnp.zeros((N_DELTA_HEADS, HEAD_DIM, N_DELTA_HEADS, HEAD_DIM), F32)
        blocks = blocks.at[jnp.arange(4), :, jnp.arange(4), :].set(pool_w[l])
        qkn = jnp.zeros((8, LANES), F32).at[0].set(jnp.tile(q_norm[l], 2)).at[1].set(jnp.tile(k_norm[l], 2))
        lw = dict(
            gpre=norm_pre[l].reshape(1, D_MODEL),
            gpost=norm_post[l].reshape(1, D_MODEL),
            w_in=_permute_w_in(w_in[l]),
            w_out=w_out[l].astype(BF16),
            qkn=qkn,
            conv_w=conv_w[l],
            alog_row=_lane_row(a_log[l], 2 * N_DELTA_HEADS),
            dtb_row=_lane_row(dt_bias[l], 2 * N_DELTA_HEADS),
            pw_bd=blocks.reshape(D_POOL, D_POOL).astype(BF16),
            pscale=pool_scale[l].reshape(1, D_POOL),
            onorm_row=jnp.tile(o_norm[l], 2).reshape(1, LANES),
        )
        hp, k_l, v_l, s_l = _layer(hp, mod[l, 0:1].reshape(1, 1, 3 * D_MODEL), lw, None, None, ctx_tiles)
        ctx = dict(k2=_rolled_pair(cache_attn_k[:, l].reshape(DB, past, D_KV)),
                   v2=_rolled_pair(cache_attn_v[:, l].reshape(DB, past, D_KV)),
                   state=state_delta[:, l])
        hs = _layer(hs, mod[l, 1:1 + DB].reshape(DB, 1, 3 * D_MODEL), lw, rope_tabs, ctx, lat_tiles)
        new_k.append(k_l.reshape(B, L, 2, HEAD_DIM))
        new_v.append(v_l.reshape(B, L, 2, HEAD_DIM))
        new_s.append(s_l)
    return (hp, hs, jnp.stack(new_k, axis=1), jnp.stack(new_v, axis=1), jnp.stack(new_s, axis=1))
```

```python
import functools

import jax
import jax.numpy as jnp
from jax import lax
from jax.experimental import pallas as pl
from jax.experimental.pallas import tpu as pltpu

F32 = jnp.float32
BF16 = jnp.bfloat16

D_MODEL = 1024
DEPTH = 2
GRID_W = 64
HEAD_DIM = 64
D_POOL = 256
D_ATTN = 512
D_DELTA = 256
D_KV = 128
N_Q_HEADS = 8
N_DELTA_HEADS = 4
POOL_WINDOWS = (2, 4, 8, 16)
CHUNK = 64
ROPE_THETA = 10000.0
EPS = 1e-6
LANES = 128
HALO = 8
MASK_VALUE = -1e30

OFF_UP = 0
OFF_GATES = 256
OFF_QK = 1280
OFF_V = 1920
OFF_QKVD = 2048
OFF_BA = 2816
D_IN_PAD = 2944

VMEM_LIMIT = 56 * 1024 * 1024


def _cparams(*sem):
    return pltpu.CompilerParams(dimension_semantics=sem, vmem_limit_bytes=VMEM_LIMIT)


def _dot(a, b):
    return jnp.dot(a, b, preferred_element_type=F32)


def _dot_nt(a, b):
    return lax.dot_general(a, b, (((1,), (1,)), ((), ())), preferred_element_type=F32)


def _dot_tn(a, b):
    return lax.dot_general(a, b, (((0,), (0,)), ((), ())), preferred_element_type=F32)


def _split(x):
    hi = x.astype(BF16)
    return hi, (x - hi.astype(F32)).astype(BF16)


def _dot_split(a, b):
    (ah, al), (bh, bl) = a, b
    m = ah.shape[0]
    both = _dot(jnp.concatenate([ah, al], axis=0), bh)
    return both[0:m] + both[m:2 * m] + _dot(ah, bl)


def _silu(x):
    return x * jax.nn.sigmoid(x)


def _softplus(x):
    return jnp.maximum(x, 0.0) + jnp.log1p(jnp.exp(-jnp.abs(x)))


def _half_rms_scale(x, lo_mask):
    sq = x * x
    ss_lo = jnp.sum(jnp.where(lo_mask, sq, 0.0), axis=-1, keepdims=True)
    ss_hi = jnp.sum(jnp.where(lo_mask, 0.0, sq), axis=-1, keepdims=True)
    r_lo = lax.rsqrt(ss_lo * (1.0 / HEAD_DIM) + EPS)
    r_hi = lax.rsqrt(ss_hi * (1.0 / HEAD_DIM) + EPS)
    return jnp.where(lo_mask, r_lo, r_hi)


def _mod_kernel(c_ref, w_ref, b_ref, o_ref):
    s = _silu(c_ref[...])
    o_ref[0] = _dot(s.astype(BF16), w_ref[0].astype(BF16)) + b_ref[0]


def _modulation(conds, w_mod, b_mod):
    tn = 768
    return pl.pallas_call(
        _mod_kernel,
        grid=(DEPTH, 3 * D_MODEL // tn),
        in_specs=[pl.BlockSpec((8, D_MODEL), lambda l, j: (0, 0)),
                  pl.BlockSpec((1, D_MODEL, tn), lambda l, j: (l, 0, j)),
                  pl.BlockSpec((1, 1, tn), lambda l, j: (l, 0, j))],
        out_specs=pl.BlockSpec((1, 8, tn), lambda l, j: (l, 0, j)),
        out_shape=jax.ShapeDtypeStruct((DEPTH, 8, 3 * D_MODEL), F32),
        compiler_params=_cparams("parallel", "parallel"),
        name="modulation",
    )(conds, w_mod, b_mod.reshape(DEPTH, 1, 3 * D_MODEL))


def _inproj_kernel(rope, x_ref, mod_ref, gpre_ref, w_ref, qkn_ref, *rest):
    if rope:
        cos_ref, sin_ref, up_ref, gates_ref, q_ref, k2_ref, v2_ref, qkvd_ref, ba_ref = rest
    else:
        up_ref, gates_ref, q_ref, k2_ref, v2_ref, qkvd_ref, ba_ref, kn_ref, v_ref = rest
    x = x_ref[0]
    tm = x.shape[0]
    mod = mod_ref[0]
    shift = mod[:, 0:D_MODEL]
    scale = mod[:, D_MODEL:2 * D_MODEL]
    ms = jnp.mean(x * x, axis=-1, keepdims=True)
    h = (x * lax.rsqrt(ms + EPS)) * gpre_ref[...] * (1.0 + scale) + shift
    hb = h.astype(BF16)

    up_ref[0] = _dot(hb, w_ref[:, OFF_UP:OFF_UP + D_POOL])
    gates_ref[0] = _silu(_dot(hb, w_ref[:, OFF_GATES:OFF_GATES + D_MODEL]))
    qkvd_ref[0] = _dot(hb, w_ref[:, OFF_QKVD:OFF_QKVD + 3 * D_DELTA])
    ba_ref[0] = _dot(hb, w_ref[:, OFF_BA:OFF_BA + LANES])

    lane = lax.broadcasted_iota(jnp.int32, (tm, LANES), 1)
    lo_mask = lane < HEAD_DIM
    even = (lane % 2) == 0

    def head_norm_rope(blk, gain):
        y = blk * _half_rms_scale(blk, lo_mask) * gain
        if rope:
            swapped = jnp.where(even, pltpu.roll(y, LANES - 1, 1), pltpu.roll(y, 1, 1))
            return y, y * cos_ref[...] + swapped * sin_ref[...]
        return y, y

    zqk = _dot(hb, w_ref[:, OFF_QK:OFF_QK + D_ATTN + D_KV])
    for i in range(D_ATTN // LANES):
        _, qr = head_norm_rope(zqk[:, i * LANES:(i + 1) * LANES], qkn_ref[0:1, :])
        q_ref[0, :, i * LANES:(i + 1) * LANES] = (qr * (HEAD_DIM ** -0.5)).astype(BF16)
    kn, kr = head_norm_rope(zqk[:, D_ATTN:D_ATTN + D_KV], qkn_ref[1:2, :])
    k2_ref[0, :, 0:LANES] = kr.astype(BF16)
    k2_ref[0, :, LANES:2 * LANES] = pltpu.roll(kr, HEAD_DIM, 1).astype(BF16)
    v = _dot(hb, w_ref[:, OFF_V:OFF_V + D_KV])
    v2_ref[0, :, 0:LANES] = v.astype(BF16)
    v2_ref[0, :, LANES:2 * LANES] = pltpu.roll(v, HEAD_DIM, 1).astype(BF16)
    if not rope:
        kn_ref[0] = kn
        v_ref[0] = v


def _inproj(x, mod, gpre, w, qkn, rope_tabs, tm):
    B, L, _ = x.shape
    rope = rope_tabs is not None
    shared_mod = mod.shape[0] == 1
    row = lambda b, i: (b, i, 0)
    const2 = lambda b, i: (0, 0)
    in_specs = [pl.BlockSpec((1, tm, D_MODEL), row),
                pl.BlockSpec((1, 1, 3 * D_MODEL), (lambda b, i: (0, 0, 0)) if shared_mod else (lambda b, i: (b, 0, 0))),
                pl.BlockSpec((1, D_MODEL), const2),
                pl.BlockSpec((D_MODEL, D_IN_PAD), const2),
                pl.BlockSpec((8, LANES), const2)]
    args = [x, mod, gpre, w, qkn]
    if rope:
        in_specs += [pl.BlockSpec((tm, LANES), lambda b, i: (i, 0))] * 2
        args += list(rope_tabs)
    widths = [(D_POOL, F32), (D_MODEL, F32), (D_ATTN, BF16), (2 * D_KV, BF16), (2 * D_KV, BF16),
              (3 * D_DELTA, F32), (LANES, F32)]
    if not rope:
        widths += [(D_KV, F32), (D_KV, F32)]
    out_specs = [pl.BlockSpec((1, tm, wd), row) for wd, _ in widths]
    out_shape = [jax.ShapeDtypeStruct((B, L, wd), dt) for wd, dt in widths]
    return pl.pallas_call(
        functools.partial(_inproj_kernel, rope),
        grid=(B, L // tm),
        in_specs=in_specs,
        out_specs=out_specs,
        out_shape=out_shape,
        compiler_params=_cparams("parallel", "parallel"),
        name="inproj_rope" if rope else "inproj",
    )(*args)


_HEAD_GROUPS = ((0, 2, 5, 7), (1, 3, 4, 6))


def _attn_kernel(n_src, tk, q_ref, *rest):
    kv_refs = rest[:2 * n_src]
    o_ref, m_ref, l_ref, acc_ref = rest[2 * n_src:]
    tq = q_ref.shape[1]
    lane = lax.broadcasted_iota(jnp.int32, (tq, LANES), 1)
    lo_mask = lane < HEAD_DIM

    m_ref[...] = jnp.full(m_ref.shape, MASK_VALUE, F32)
    l_ref[...] = jnp.zeros(l_ref.shape, F32)
    acc_ref[...] = jnp.zeros(acc_ref.shape, F32)

    qg = []
    for grp in _HEAD_GROUPS:
        parts = []
        for h in grp:
            blk = q_ref[0, :, (h // 2) * LANES:(h // 2 + 1) * LANES]
            keep = lo_mask if h % 2 == 0 else jnp.logical_not(lo_mask)
            parts.append(jnp.where(keep, blk, jnp.zeros_like(blk)))
        qg.append(jnp.concatenate(parts, axis=0))

    cols = [slice(g * LANES, (g + 1) * LANES) for g in range(2)]
    chunks = []
    for j in range(n_src):
        k2_ref, v2_ref = kv_refs[2 * j], kv_refs[2 * j + 1]
        S = k2_ref.shape[1]
        size = min(tk, S)
        chunks += [(k2_ref, v2_ref, c * size, size) for c in range(S // size)]

    def scores(ch):
        k2_ref, _, start, size = ch
        return [_dot_nt(qg[g], k2_ref[0, start:start + size, cols[g]]) for g in range(2)]

    s_next = scores(chunks[0])
    for ci, ch in enumerate(chunks):
        _, v2_ref, start, size = ch
        s = s_next
        if ci + 1 < len(chunks):
            s_next = scores(chunks[ci + 1])
        m_prev = [m_ref[g] for g in range(2)]
        m_new = [jnp.maximum(m_prev[g], jnp.max(s[g], axis=-1, keepdims=True)) for g in range(2)]
        alpha = [jnp.exp(m_prev[g] - m_new[g]) for g in range(2)]
        p = [jnp.exp(s[g] - jnp.tile(m_new[g], (1, size // LANES))) for g in range(2)]
        for g in range(2):
            l_ref[g] = alpha[g] * l_ref[g] + jnp.sum(p[g], axis=-1, keepdims=True)
            m_ref[g] = m_new[g]
        pv = [_dot(p[g].astype(BF16), v2_ref[0, start:start + size, cols[g]]) for g in range(2)]
        for g in range(2):
            acc_ref[g] = alpha[g] * acc_ref[g] + pv[g]

    def head_rows(ref, h):
        g = 0 if h in _HEAD_GROUPS[0] else 1
        r = _HEAD_GROUPS[g].index(h) * tq
        return ref[g, r:r + tq, :]

    for i in range(N_Q_HEADS // 2):
        even = head_rows(acc_ref, 2 * i) / head_rows(l_ref, 2 * i)
        odd = head_rows(acc_ref, 2 * i + 1) / head_rows(l_ref, 2 * i + 1)
        o_ref[0, :, i * LANES:(i + 1) * LANES] = jnp.where(lo_mask, even, odd)


def _attention(q, sources, tq, tk):
    B, L, _ = q.shape
    in_specs = [pl.BlockSpec((1, tq, D_ATTN), lambda b, i: (b, i, 0))]
    args = [q]
    for k2, v2 in sources:
        S = k2.shape[1]
        in_specs += [pl.BlockSpec((1, S, 2 * D_KV), lambda b, i: (b, 0, 0))] * 2
        args += [k2, v2]
    return pl.pallas_call(
        functools.partial(_attn_kernel, len(sources), tk),
        grid=(B, L // tq),
        in_specs=in_specs,
        out_specs=pl.BlockSpec((1, tq, D_ATTN), lambda b, i: (b, i, 0)),
        out_shape=jax.ShapeDtypeStruct((B, L, D_ATTN), F32),
        scratch_shapes=[pltpu.VMEM((2, N_Q_HEADS // 2 * tq, LANES), F32)] * 3,
        compiler_params=_cparams("parallel", "parallel"),
        name="attention",
    )(*args)


def _delta_prep_kernel(cur_ref, prev_ref, next_ref, ba_ref, convw_ref, alog_ref, dtb_ref, act_ref, bg_ref, xp_ref):
    i = pl.program_id(1)
    tt = cur_ref.shape[1]
    first = i == 0
    last = i == pl.num_programs(1) - 1
    xp_ref[0:HALO, :] = jnp.where(first, 0.0, prev_ref[0])
    xp_ref[HALO:HALO + tt, :] = cur_ref[0]
    xp_ref[HALO + tt:2 * HALO + tt, :] = jnp.where(last, 0.0, next_ref[0])
    acc = convw_ref[0:1, :] * xp_ref[HALO - 2:HALO - 2 + tt, :]
    for j in range(1, 4):
        acc = acc + convw_ref[j:j + 1, :] * xp_ref[HALO - 2 + j:HALO - 2 + j + tt, :]
    act_ref[0] = _silu(acc)
    ba = ba_ref[0]
    lane = lax.broadcasted_iota(jnp.int32, ba.shape, 1)
    beta = jax.nn.sigmoid(ba)
    g = -jnp.exp(alog_ref[...]) * _softplus(ba + dtb_ref[...])
    bg_ref[0] = jnp.where(lane < 2 * N_DELTA_HEADS, beta, jnp.where(lane < 4 * N_DELTA_HEADS, g, 0.0))


def _delta_prep(qkvd, ba, conv_w, alog_row, dtb_row, tt):
    B, L, W = qkvd.shape
    nb = tt // HALO
    last_blk = L // HALO - 1
    return pl.pallas_call(
        _delta_prep_kernel,
        grid=(B, L // tt),
        in_specs=[pl.BlockSpec((1, tt, W), lambda b, i: (b, i, 0)),
                  pl.BlockSpec((1, HALO, W), lambda b, i: (b, jnp.maximum(i * nb - 1, 0), 0)),
                  pl.BlockSpec((1, HALO, W), lambda b, i: (b, jnp.minimum((i + 1) * nb, last_blk), 0)),
                  pl.BlockSpec((1, tt, LANES), lambda b, i: (b, i, 0)),
                  pl.BlockSpec((4, W), lambda b, i: (0, 0)),
                  pl.BlockSpec((1, LANES), lambda b, i: (0, 0)),
                  pl.BlockSpec((1, LANES), lambda b, i: (0, 0))],
        out_specs=[pl.BlockSpec((1, tt, W), lambda b, i: (b, i, 0)),
                   pl.BlockSpec((1, tt, LANES), lambda b, i: (b, i, 0))],
        out_shape=[jax.ShapeDtypeStruct((B, L, W), F32), jax.ShapeDtypeStruct((B, L, LANES), F32)],
        scratch_shapes=[pltpu.VMEM((tt + 2 * HALO, W), F32)],
        compiler_params=_cparams("parallel", "parallel"),
        name="delta_prep",
    )(qkvd, qkvd, qkvd, ba, conv_w, alog_row, dtb_row)


def _block_diag(x, bd_mask):
    return jnp.where(bd_mask, jnp.concatenate([x] * N_DELTA_HEADS, axis=0), jnp.zeros((), x.dtype))


def _dot_split_bd(a, b, bd_mask):
    (ah, al), (bh, bl) = a, b
    m = ah.shape[0]
    both = _dot(jnp.concatenate([ah, al], axis=0), _block_diag(bh, bd_mask))
    return both[0:m] + both[m:2 * m] + _dot(ah, _block_diag(bl, bd_mask))


def _delta_chunk_kernel(act_ref, bg_ref, gt_ref, pm_ref, bm_ref, ol_ref, e_ref):
    C, W, H = CHUNK, D_DELTA, N_DELTA_HEADS
    n_chunks = act_ref.shape[1] // C
    ri = lax.broadcasted_iota(jnp.int32, (C, W), 0)
    lane = lax.broadcasted_iota(jnp.int32, (C, W), 1)
    lj = lane % HEAD_DIM
    blk = lane // HEAD_DIM
    bd_mask = (lax.broadcasted_iota(jnp.int32, (W, W), 0) // HEAD_DIM) == (lax.broadcasted_iota(jnp.int32, (W, W), 1) // HEAD_DIM)
    diag = ri == lj
    eye = diag.astype(F32)

    def expand(cols):
        res = cols[H - 1]
        for h in range(H - 2, -1, -1):
            res = jnp.where(blk == h, cols[h], res)
        return res

    def seg_sum(x):
        return expand([jnp.sum(jnp.where(blk == h, x, 0.0), axis=-1, keepdims=True) for h in range(H)])

    def l2n(x):
        return x * lax.rsqrt(seg_sum(x * x) + EPS)

    def row_form(x):
        res = x[(H - 1) * C:H * C]
        for h in range(H - 2, -1, -1):
            res = jnp.where(blk == h, x[h * C:(h + 1) * C], res)
        return res

    units = []
    for c in range(n_chunks):
        rows = slice(c * C, (c + 1) * C)
        q = l2n(act_ref[0, rows, 0:W]) * (HEAD_DIM ** -0.5)
        k = l2n(act_ref[0, rows, W:2 * W])
        v = act_ref[0, rows, 2 * W:3 * W]
        kb = k.astype(BF16)
        kkqk = _dot_nt(jnp.concatenate([kb, q.astype(BF16)], axis=0), _block_diag(kb, bd_mask))
        kk, qk = kkqk[0:C], kkqk[C:2 * C]
        bg = bg_ref[0, rows, :]
        gt = gt_ref[0, c]
        for d in range(2):
            beta = expand([bg[:, d * H + h:d * H + h + 1] for h in range(H)])
            g = expand([bg[:, 2 * H + d * H + h:2 * H + d * H + h + 1] for h in range(H)])
            g_row = gt[d:d + 1, :]
            incl = (ri >= lj) if d == 0 else (ri <= lj)
            incl_t = (ri <= lj) if d == 0 else (ri >= lj)
            strict = (ri > lj) if d == 0 else (ri < lj)
            gc = seg_sum(jnp.where(incl, g_row, 0.0))
            gc_row = jnp.sum(jnp.where(incl_t, g, 0.0), axis=0, keepdims=True)
            g_tot = jnp.sum(g, axis=0, keepdims=True)
            decay = jnp.where(incl, jnp.exp(jnp.where(incl, gc - gc_row, 0.0)), 0.0)
            t = jnp.where(strict, -(beta * kk * decay), 0.0)
            units.append(dict(c=c, d=d, q=q, k=k, v=v, beta=beta, gc=gc, g_tot=g_tot,
                              attn=(qk * decay).astype(BF16), t=t, p=eye + t))

    for un in units:
        un["t"] = _dot_split_bd(_split(un["t"]), _split(un["t"]), bd_mask)
    for _ in range(4):
        for un in units:
            both = _dot_split_bd(_split(jnp.concatenate([un["t"], un["p"]], axis=0)), _split(un["t"]), bd_mask)
            un["t"] = both[0:C]
            un["p"] = un["p"] + both[C:2 * C]
    for un in units:
        un["p"] = (un["p"] + _dot_split_bd(_split(un["p"]), _split(un["t"]), bd_mask)).astype(BF16)
    for un in units:
        un["egc"] = jnp.exp(un["gc"])
        un["u"] = _dot(un["p"], _block_diag((un["v"] * un["beta"]).astype(BF16), bd_mask)).astype(BF16)
        un["w"] = _dot(un["p"], _block_diag((un["k"] * (un["beta"] * un["egc"])).astype(BF16), bd_mask)).astype(BF16)
    for un in units:
        c, d = un["c"], un["d"]
        kdec = (un["k"] * jnp.exp(un["g_tot"] - un["gc"])).astype(BF16)
        bm_ref[0, d, c] = row_form(_dot_tn(kdec, un["u"]))
        mp = row_form(_dot_tn(kdec, un["w"]))
        qt = un["q"] * un["egc"] - _dot(un["attn"], _block_diag(un["w"], bd_mask))
        ol_ref[0, d, c] = _dot(un["attn"], _block_diag(un["u"], bd_mask))
        pm_ref[0, d, c] = jnp.concatenate([mp, qt], axis=0).astype(BF16)
        e_ref[0, d, c] = jnp.broadcast_to(jnp.exp(un["g_tot"]), (HALO, W))


def _delta_chunks(act, bg, gt, tc):
    B, L, _ = act.shape
    N = L // CHUNK
    cb = tc // CHUNK
    W = D_DELTA
    ospec = lambda r: pl.BlockSpec((1, 2, cb, r, W), lambda b, i: (b, 0, i, 0, 0))
    return pl.pallas_call(
        _delta_chunk_kernel,
        grid=(B, L // tc),
        in_specs=[pl.BlockSpec((1, tc, 3 * W), lambda b, i: (b, i, 0)),
                  pl.BlockSpec((1, tc, LANES), lambda b, i: (b, i, 0)),
                  pl.BlockSpec((1, cb, 2, W), lambda b, i: (b, i, 0, 0))],
        out_specs=[ospec(2 * CHUNK), ospec(CHUNK), ospec(CHUNK), ospec(HALO)],
        out_shape=[jax.ShapeDtypeStruct((B, 2, N, 2 * CHUNK, W), BF16),
                   jax.ShapeDtypeStruct((B, 2, N, CHUNK, W), F32),
                   jax.ShapeDtypeStruct((B, 2, N, CHUNK, W), F32),
                   jax.ShapeDtypeStruct((B, 2, N, HALO, W), F32)],
        compiler_params=_cparams("parallel", "parallel"),
        name="delta_chunks",
    )(act, bg, gt)


def _delta_scan_kernel(pmf_ref, bmf_ref, olf_ref, ef_ref, pmb_ref, bmb_ref, olb_ref, eb_ref, s0_ref,
                       of_ref, ob_ref, sout_ref, s_ref):
    n = pl.program_id(1)
    bb = s_ref.shape[0]
    C, W = CHUNK, D_DELTA
    bd_mask = (lax.broadcasted_iota(jnp.int32, (W, W), 0) // HEAD_DIM) == (lax.broadcasted_iota(jnp.int32, (W, W), 1) // HEAD_DIM)

    @pl.when(n == 0)
    def _():
        s_ref[...] = s0_ref[...]

    dirs = ((pmf_ref, bmf_ref, olf_ref, ef_ref, of_ref), (pmb_ref, bmb_ref, olb_ref, eb_ref, ob_ref))
    chains = [(b, d) for b in range(bb) for d in range(2)]
    s = [s_ref[b, d] for b, d in chains]
    r = [_dot(dirs[d][0][b, 0, 0], _block_diag(s[i].astype(BF16), bd_mask)) for i, (b, d) in enumerate(chains)]
    for i, (b, d) in enumerate(chains):
        _, bm_ref, ol_ref, e_ref, o_ref = dirs[d]
        s_ref[b, d] = e_ref[b, 0, 0][0:1, :] * s[i] + bm_ref[b, 0, 0] - r[i][0:C]
        o_ref[b] = r[i][C:2 * C] + ol_ref[b, 0, 0]

    @pl.when(n == pl.num_programs(1) - 1)
    def _():
        sout_ref[...] = s_ref[...]


def _delta_scan(pm, bm, ol, e, s0, bb):
    B, _, N, _, W = pm.shape
    L = N * CHUNK

    def spec(r, d):
        if d == 0:
            return pl.BlockSpec((bb, 1, 1, r, W), lambda b, n: (b, 0, n, 0, 0))
        return pl.BlockSpec((bb, 1, 1, r, W), lambda b, n: (b, 1, N - 1 - n, 0, 0))

    in_specs = []
    for d in range(2):
        in_specs += [spec(2 * CHUNK, d), spec(CHUNK, d), spec(CHUNK, d), spec(HALO, d)]
    in_specs.append(pl.BlockSpec((bb, 2, CHUNK, W), lambda b, n: (b, 0, 0, 0)))
    return pl.pallas_call(
        _delta_scan_kernel,
        grid=(B // bb, N),
        in_specs=in_specs,
        out_specs=[pl.BlockSpec((bb, CHUNK, W), lambda b, n: (b, n, 0)),
                   pl.BlockSpec((bb, CHUNK, W), lambda b, n: (b, N - 1 - n, 0)),
                   pl.BlockSpec((bb, 2, CHUNK, W), lambda b, n: (b, 0, 0, 0))],
        out_shape=[jax.ShapeDtypeStruct((B, L, W), F32),
                   jax.ShapeDtypeStruct((B, L, W), F32),
                   jax.ShapeDtypeStruct((B, 2, CHUNK, W), F32)],
        scratch_shapes=[pltpu.VMEM((bb, 2, CHUNK, W), F32)],
        compiler_params=_cparams("parallel", "arbitrary"),
        name="delta_scan",
    )(pm, bm, ol, e, pm, bm, ol, e, s0)


def _merge_kernel(seq_len, x_ref, mod_ref, up_ref, upp_ref, upn_ref, gates_ref, oattn_ref, of_ref, ob_ref,
                  pw_ref, pscale_ref, onorm_ref, wout_ref, gpost_ref, o_ref, pad_ref):
    i = pl.program_id(1)
    tm = x_ref.shape[1]
    first = i == 0
    last = i == pl.num_programs(1) - 1
    pad_ref[0:HALO, :] = jnp.where(first, 0.0, upp_ref[0])
    pad_ref[HALO:HALO + tm, :] = up_ref[0]
    pad_ref[HALO + tm:2 * HALO + tm, :] = jnp.where(last, 0.0, upn_ref[0])

    def window_sum(col, lo, hi):
        acc = pad_ref[HALO + lo:HALO + lo + tm, col]
        for j in range(lo + 1, hi):
            acc = acc + pad_ref[HALO + j:HALO + j + tm, col]
        return acc

    lane = lax.broadcasted_iota(jnp.int32, (tm, LANES), 1)
    lo_mask = lane < HEAD_DIM
    t = lax.broadcasted_iota(jnp.int32, (tm, LANES), 0) + i * tm

    def count(w):
        return (jnp.minimum(t - w // 2 + w, seq_len) - jnp.maximum(t - w // 2, 0)).astype(F32)

    pooled = []
    for col_blk, (w_lo, w_hi) in enumerate(((2, 4), (8, 16))):
        col = slice(col_blk * LANES, (col_blk + 1) * LANES)
        s_lo = window_sum(col, -(w_lo // 2), w_lo // 2)
        s_hi = s_lo + window_sum(col, -(w_hi // 2), -(w_lo // 2)) + window_sum(col, w_lo // 2, w_hi // 2)
        mean = jnp.where(lo_mask, s_lo / count(w_lo), s_hi / count(w_hi))
        pooled.append(mean - up_ref[0, :, col])
    pooled = jnp.concatenate(pooled, axis=1)
    o_pool = _dot(pooled.astype(BF16), pw_ref[...]) * pscale_ref[...]

    od = of_ref[0] + ob_ref[0]
    odn = []
    for j in range(D_DELTA // LANES):
        blk = od[:, j * LANES:(j + 1) * LANES]
        odn.append(blk * _half_rms_scale(blk, lo_mask) * onorm_ref[...])
    o_delta = jnp.concatenate(odn, axis=1)

    gates = gates_ref[0]
    y = _dot((gates[:, 0:D_POOL] * o_pool).astype(BF16), wout_ref[0:D_POOL, :])
    y = y + _dot((gates[:, D_POOL:D_POOL + D_ATTN] * oattn_ref[0]).astype(BF16), wout_ref[D_POOL:D_POOL + D_ATTN, :])
    y = y + _dot((gates[:, D_POOL + D_ATTN:] * o_delta).astype(BF16), wout_ref[D_POOL + D_ATTN:, :])
    ms = jnp.mean(y * y, axis=-1, keepdims=True)
    yn = (y * lax.rsqrt(ms + EPS)) * gpost_ref[...]
    gate = mod_ref[0][:, 2 * D_MODEL:3 * D_MODEL]
    o_ref[0] = x_ref[0] + gate * yn


def _merge(x, mod, up, gates, oattn, o_f, o_b, pw_bd, pscale, onorm_row, wout, gpost, tm):
    B, L, _ = x.shape
    nb = tm // HALO
    last_blk = L // HALO - 1
    shared_mod = mod.shape[0] == 1
    row = lambda b, i: (b, i, 0)
    const2 = lambda b, i: (0, 0)
    return pl.pallas_call(
        functools.partial(_merge_kernel, L),
        grid=(B, L // tm),
        in_specs=[pl.BlockSpec((1, tm, D_MODEL), row),
                  pl.BlockSpec((1, 1, 3 * D_MODEL), (lambda b, i: (0, 0, 0)) if shared_mod else (lambda b, i: (b, 0, 0))),
                  pl.BlockSpec((1, tm, D_POOL), row),
                  pl.BlockSpec((1, HALO, D_POOL), lambda b, i: (b, jnp.maximum(i * nb - 1, 0), 0)),
                  pl.BlockSpec((1, HALO, D_POOL), lambda b, i: (b, jnp.minimum((i + 1) * nb, last_blk), 0)),
                  pl.BlockSpec((1, tm, D_MODEL), row),
                  pl.BlockSpec((1, tm, D_ATTN), row),
                  pl.BlockSpec((1, tm, D_DELTA), row),
                  pl.BlockSpec((1, tm, D_DELTA), row),
                  pl.BlockSpec((D_POOL, D_POOL), const2),
                  pl.BlockSpec((1, D_POOL), const2),
                  pl.BlockSpec((1, LANES), const2),
                  pl.BlockSpec((D_MODEL, D_MODEL), const2),
                  pl.BlockSpec((1, D_MODEL), const2)],
        out_specs=pl.BlockSpec((1, tm, D_MODEL), row),
        out_shape=jax.ShapeDtypeStruct((B, L, D_MODEL), F32),
        scratch_shapes=[pltpu.VMEM((tm + 2 * HALO, D_POOL), F32)],
        compiler_params=_cparams("parallel", "parallel"),
        name="merge",
    )(x, mod, up, up, up, gates, oattn, o_f, o_b, pw_bd, pscale, onorm_row, wout, gpost)


def _rope_tables(num_tokens):
    rows = num_tokens // GRID_W
    row = jnp.repeat(jnp.arange(rows, dtype=F32), GRID_W)
    col = (jnp.arange(rows * GRID_W) % GRID_W).astype(F32)
    axis_dim = HEAD_DIM // 2
    inv = ROPE_THETA ** (-jnp.arange(0, axis_dim, 2, dtype=F32) / axis_dim)
    ang = jnp.concatenate([row[:, None] * inv, col[:, None] * inv], axis=-1)
    cos = jnp.repeat(jnp.cos(ang), 2, axis=-1)
    sin = jnp.repeat(jnp.sin(ang), 2, axis=-1)
    sign = jnp.tile(jnp.array([-1.0, 1.0], F32), HEAD_DIM // 2)
    return jnp.tile(cos, (1, 2)), jnp.tile(sin * sign, (1, 2))


def _permute_w_in(w):
    u_pool, g_pool, q, k, v, g_attn, qkv_d, b_d, a_d, g_delta = jnp.split(
        w, [256, 512, 1024, 1152, 1280, 1792, 2560, 2568, 2576], axis=1)
    pad = jnp.zeros((w.shape[0], LANES - 4 * N_DELTA_HEADS), w.dtype)
    return jnp.concatenate([u_pool, g_pool, g_attn, g_delta, q, k, v, qkv_d, b_d, a_d, pad], axis=1).astype(BF16)


def _lane_row(vec, offset):
    return jnp.zeros((1, LANES), F32).at[0, offset:offset + vec.size].set(vec.reshape(-1))


def _rolled_pair(x):
    return jnp.concatenate([x, jnp.roll(x, HEAD_DIM, axis=-1)], axis=-1).astype(BF16)


def _layer(x, mod, lw, rope_tabs, ctx, tiles):
    B, L, _ = x.shape
    latent = ctx is not None
    outs = _inproj(x, mod, lw["gpre"], lw["w_in"], lw["qkn"], rope_tabs, tiles["tm"])
    up, gates, q, k2, v2, qkvd, ba = outs[:7]
    sources = [(k2, v2)]
    if latent:
        sources.append((ctx["k2"], ctx["v2"]))
    oattn = _attention(q, sources, tiles["tq"], tiles["tk"])

    act, bg = _delta_prep(qkvd, ba, lw["conv_w"], lw["alog_row"], lw["dtb_row"], tiles["tt"])
    gt = bg[:, :, 2 * N_DELTA_HEADS:4 * N_DELTA_HEADS].reshape(B, L // CHUNK, CHUNK, 2, N_DELTA_HEADS)
    gt = gt.transpose(0, 1, 3, 4, 2).reshape(B, L // CHUNK, 2, D_DELTA)
    pm, bm, ol, e = _delta_chunks(act, bg, gt, tiles["tc"])
    s0 = ctx["state"] if latent else jnp.zeros((B, 2, HEAD_DIM, D_DELTA), F32)
    o_f, o_b, s_out = _delta_scan(pm, bm, ol, e, s0, tiles["bb"])

    y = _merge(x, mod, up, gates, oattn, o_f, o_b, lw["pw_bd"], lw["pscale"], lw["onorm_row"],
               lw["w_out"], lw["gpost"], tiles["tm"])
    if latent:
        return y
    kn, v = outs[7], outs[8]
    s_out = s_out.reshape(B, 2, HEAD_DIM, N_DELTA_HEADS, HEAD_DIM).transpose(0, 1, 3, 2, 4)
    return y, kn, v, s_out


def kernel(x_prompt, x_sample, cache_attn_k, cache_attn_v, state_delta, c, c_ctx, w_mod, b_mod, norm_pre, norm_post,
           w_in, w_out, pool_w, pool_scale, q_norm, k_norm, conv_w, a_log, dt_bias, o_norm):
    B, L, _ = x_prompt.shape
    DB, DL, _ = x_sample.shape
    past = cache_attn_k.shape[2]

    conds = jnp.zeros((8, D_MODEL), F32).at[0].set(c_ctx).at[1:1 + DB].set(c)
    mod = _modulation(conds, w_mod, b_mod)
    rope_tabs = _rope_tables(DL)

    ctx_tiles = dict(tm=256, tq=256, tk=256, tt=256, tc=256, bb=8)
    lat_tiles = dict(tm=512, tq=256, tk=1024, tt=512, tc=256, bb=4)

    hp, hs = x_prompt, x_sample
    new_k, new_v, new_s = [], [], []
    for l in range(DEPTH):
        blocks = jnp.zeros((N_DELTA_HEADS, HEAD_DIM, N_DELTA_HEADS, HEAD_DIM), F32)
        blocks = blocks.at[jnp.arange(4), :, jnp.arange(4), :].set(pool_w[l])
        qkn = jnp.zeros((8, LANES), F32).at[0].set(jnp.tile(q_norm[l], 2)).at[1].set(jnp.tile(k_norm[l], 2))
        lw = dict(
            gpre=norm_pre[l].reshape(1, D_MODEL),
            gpost=norm_post[l].reshape(1, D_MODEL),
            w_in=_permute_w_in(w_in[l]),
            w_out=w_out[l].astype(BF16),
            qkn=qkn,
            conv_w=conv_w[l],
            alog_row=_lane_row(a_log[l], 2 * N_DELTA_HEADS),
            dtb_row=_lane_row(dt_bias[l], 2 * N_DELTA_HEADS),
            pw_bd=blocks.reshape(D_POOL, D_POOL).astype(BF16),
            pscale=pool_scale[l].reshape(1, D_POOL),
            onorm_row=jnp.tile(o_norm[l], 2).reshape(1, LANES),
        )
        hp, k_l, v_l, s_l = _layer(hp, mod[l, 0:1].reshape(1, 1, 3 * D_MODEL), lw, None, None, ctx_tiles)
        ctx = dict(k2=_rolled_pair(cache_attn_k[:, l].reshape(DB, past, D_KV)),
                   v2=_rolled_pair(cache_attn_v[:, l].reshape(DB, past, D_KV)),
                   state=state_delta[:, l].transpose(0, 1, 3, 2, 4).reshape(DB, 2, HEAD_DIM, D_DELTA))
        hs = _layer(hs, mod[l, 1:1 + DB].reshape(DB, 1, 3 * D_MODEL), lw, rope_tabs, ctx, lat_tiles)
        new_k.append(k_l.reshape(B, L, 2, HEAD_DIM))
        new_v.append(v_l.reshape(B, L, 2, HEAD_DIM))
        new_s.append(s_l)
    return (hp, hs, jnp.stack(new_k, axis=1), jnp.stack(new_v, axis=1), jnp.stack(new_s, axis=1))
```

```python
import functools

import jax
import jax.numpy as jnp
from jax import lax
from jax.experimental import pallas as pl
from jax.experimental.pallas import tpu as pltpu

F32 = jnp.float32
BF16 = jnp.bfloat16

D_MODEL = 1024
DEPTH = 2
GRID_W = 64
HEAD_DIM = 64
D_POOL = 256
D_ATTN = 512
D_DELTA = 256
D_KV = 128
N_Q_HEADS = 8
N_DELTA_HEADS = 4
POOL_WINDOWS = (2, 4, 8, 16)
CHUNK = 64
ROPE_THETA = 10000.0
EPS = 1e-6
LANES = 128
HALO = 8
MASK_VALUE = -1e30
LOG2E = 1.4426950408889634
Q_SCALE = HEAD_DIM ** -0.5 * LOG2E
SHIFT_LIMIT = 60.0

OFF_UP = 0
OFF_GATES = 256
OFF_QK = 1280
OFF_V = 1920
OFF_QKVD = 2048
OFF_BA = 2816
D_IN_PAD = 2944

VMEM_LIMIT = 56 * 1024 * 1024


def _cparams(*sem):
    return pltpu.CompilerParams(dimension_semantics=sem, vmem_limit_bytes=VMEM_LIMIT)


def _dot(a, b):
    return jnp.dot(a, b, preferred_element_type=F32)


def _dot_nt(a, b):
    return lax.dot_general(a, b, (((1,), (1,)), ((), ())), preferred_element_type=F32)


def _dot_tn(a, b):
    return lax.dot_general(a, b, (((0,), (0,)), ((), ())), preferred_element_type=F32)


def _split(x):
    hi = x.astype(BF16)
    return hi, (x - hi.astype(F32)).astype(BF16)


def _silu(x):
    return x * jax.nn.sigmoid(x)


def _softplus(x):
    return jnp.maximum(x, 0.0) + jnp.log1p(jnp.exp(-jnp.abs(x)))


def _kv_variants(k, k_sw, v, v_sw, lane):
    lo = lane < HEAD_DIM
    one_hi = (lane == HEAD_DIM).astype(k.dtype)
    one_lo = (lane == 0).astype(k.dtype)
    ones = jnp.ones_like(v)
    return ((jnp.where(lo, k, one_hi), jnp.where(lo, v, ones)),
            (jnp.where(lo, one_lo, k_sw), jnp.where(lo, ones, v_sw)),
            (jnp.where(lo, k_sw, one_hi), jnp.where(lo, v_sw, ones)),
            (jnp.where(lo, one_lo, k), jnp.where(lo, ones, v)))


def _half_rms_scale(x, lo_mask):
    sq = x * x
    ss_lo = jnp.sum(jnp.where(lo_mask, sq, 0.0), axis=-1, keepdims=True)
    ss_hi = jnp.sum(jnp.where(lo_mask, 0.0, sq), axis=-1, keepdims=True)
    r_lo = lax.rsqrt(ss_lo * (1.0 / HEAD_DIM) + EPS)
    r_hi = lax.rsqrt(ss_hi * (1.0 / HEAD_DIM) + EPS)
    return jnp.where(lo_mask, r_lo, r_hi)


def _mod_kernel(c_ref, w_ref, b_ref, o_ref):
    s = _silu(c_ref[...])
    o_ref[0] = _dot(s.astype(BF16), w_ref[0].astype(BF16)) + b_ref[0]


def _modulation(conds, w_mod, b_mod):
    tn = 768
    return pl.pallas_call(
        _mod_kernel,
        grid=(DEPTH, 3 * D_MODEL // tn),
        in_specs=[pl.BlockSpec((8, D_MODEL), lambda l, j: (0, 0)),
                  pl.BlockSpec((1, D_MODEL, tn), lambda l, j: (l, 0, j)),
                  pl.BlockSpec((1, 1, tn), lambda l, j: (l, 0, j))],
        out_specs=pl.BlockSpec((1, 8, tn), lambda l, j: (l, 0, j)),
        out_shape=jax.ShapeDtypeStruct((DEPTH, 8, 3 * D_MODEL), F32),
        compiler_params=_cparams("parallel", "parallel"),
        name="modulation",
    )(conds, w_mod, b_mod.reshape(DEPTH, 1, 3 * D_MODEL))


def _inproj_kernel(rope, x_ref, mod_ref, gpre_ref, w_ref, qkn_ref, *rest):
    if rope:
        cos_ref, sin_ref, up_ref, gates_ref, q_ref, k4_ref, v4_ref, ksq_ref, qkvd_ref, ba_ref = rest
    else:
        up_ref, gates_ref, q_ref, k4_ref, v4_ref, ksq_ref, qkvd_ref, ba_ref, kn_ref, v_ref = rest
    x = x_ref[0]
    tm = x.shape[0]
    mod = mod_ref[0]
    shift = mod[:, 0:D_MODEL]
    scale = mod[:, D_MODEL:2 * D_MODEL]
    ms = jnp.mean(x * x, axis=-1, keepdims=True)
    h = (x * lax.rsqrt(ms + EPS)) * gpre_ref[...] * (1.0 + scale) + shift
    hb = h.astype(BF16)

    zqk = _dot(hb, w_ref[:, OFF_QK:OFF_QK + D_ATTN + D_KV])
    v = _dot(hb, w_ref[:, OFF_V:OFF_V + D_KV])
    zgates = _dot(hb, w_ref[:, OFF_GATES:OFF_GATES + D_MODEL])

    lane = lax.broadcasted_iota(jnp.int32, (tm, LANES), 1)
    lo_mask = lane < HEAD_DIM
    even = (lane % 2) == 0

    def head_norm_rope(blk, gain):
        y = blk * _half_rms_scale(blk, lo_mask) * gain
        if rope:
            swapped = jnp.where(even, pltpu.roll(y, LANES - 1, 1), pltpu.roll(y, 1, 1))
            return y, y * cos_ref[...] + swapped * sin_ref[...]
        return y, y

    for i in range(D_ATTN // LANES):
        _, qr = head_norm_rope(zqk[:, i * LANES:(i + 1) * LANES], qkn_ref[0:1, :])
        q_ref[0, :, i * LANES:(i + 1) * LANES] = (qr * Q_SCALE).astype(BF16)
    gates_ref[0] = _silu(zgates)
    qkvd_ref[0] = _dot(hb, w_ref[:, OFF_QKVD:OFF_QKVD + 3 * D_DELTA])
    kn, kr = head_norm_rope(zqk[:, D_ATTN:D_ATTN + D_KV], qkn_ref[1:2, :])
    up_ref[0] = _dot(hb, w_ref[:, OFF_UP:OFF_UP + D_POOL])
    ba_ref[0] = _dot(hb, w_ref[:, OFF_BA:OFF_BA + LANES])
    for a, (kx, vx) in enumerate(_kv_variants(kr, pltpu.roll(kr, HEAD_DIM, 1), v, pltpu.roll(v, HEAD_DIM, 1), lane)):
        k4_ref[0, :, a * LANES:(a + 1) * LANES] = kx.astype(BF16)
        v4_ref[0, :, a * LANES:(a + 1) * LANES] = vx.astype(BF16)
    ksq = kr * kr
    ksq = jnp.where(lo_mask, jnp.sum(jnp.where(lo_mask, ksq, 0.0), axis=-1, keepdims=True),
                    jnp.sum(jnp.where(lo_mask, 0.0, ksq), axis=-1, keepdims=True))
    ksq_ref[0, 0] = jnp.broadcast_to(jnp.max(ksq, axis=0, keepdims=True), (HALO, LANES))
    if not rope:
        kn_ref[0] = kn
        v_ref[0] = v


def _inproj(x, mod, gpre, w, qkn, rope_tabs, tm):
    B, L, _ = x.shape
    rope = rope_tabs is not None
    shared_mod = mod.shape[0] == 1
    row = lambda b, i: (b, i, 0)
    const2 = lambda b, i: (0, 0)
    in_specs = [pl.BlockSpec((1, tm, D_MODEL), row),
                pl.BlockSpec((1, 1, 3 * D_MODEL), (lambda b, i: (0, 0, 0)) if shared_mod else (lambda b, i: (b, 0, 0))),
                pl.BlockSpec((1, D_MODEL), const2),
                pl.BlockSpec((D_MODEL, D_IN_PAD), const2),
                pl.BlockSpec((8, LANES), const2)]
    args = [x, mod, gpre, w, qkn]
    if rope:
        in_specs += [pl.BlockSpec((tm, LANES), lambda b, i: (i, 0))] * 2
        args += list(rope_tabs)
    widths = [(D_POOL, F32), (D_MODEL, F32), (D_ATTN, BF16), (4 * LANES, BF16), (4 * LANES, BF16), None,
              (3 * D_DELTA, F32), (LANES, F32)]
    if not rope:
        widths += [(D_KV, F32), (D_KV, F32)]
    out_specs = [pl.BlockSpec((1, tm, wd[0]), row) if wd else pl.BlockSpec((1, 1, HALO, LANES), lambda b, i: (b, i, 0, 0))
                 for wd in widths]
    out_shape = [jax.ShapeDtypeStruct((B, L, wd[0]), wd[1]) if wd else jax.ShapeDtypeStruct((B, L // tm, HALO, LANES), F32)
                 for wd in widths]
    return pl.pallas_call(
        functools.partial(_inproj_kernel, rope),
        grid=(B, L // tm),
        in_specs=in_specs,
        out_specs=out_specs,
        out_shape=out_shape,
        compiler_params=_cparams("parallel", "parallel"),
        name="inproj_rope" if rope else "inproj",
    )(*args)


_VARIANT_HEADS = ((0, 2), (1, 3), (4, 6), (5, 7))


def _attn_kernel(n_src, tk, q_ref, ksq_ref, *rest):
    kv_refs = rest[:2 * n_src]
    o_ref, acc_ref, m_ref = rest[2 * n_src:]
    tq = q_ref.shape[1]
    n_var = len(_VARIANT_HEADS)
    lane = lax.broadcasted_iota(jnp.int32, (tq, LANES), 1)
    lo_mask = lane < HEAD_DIM

    ksq = jnp.max(ksq_ref[0], axis=0)[0:1, :]
    lo_row = lo_mask[0:1, :]
    ksq_head = [jnp.max(jnp.where(lo_row, ksq, 0.0), axis=-1, keepdims=True),
                jnp.max(jnp.where(lo_row, 0.0, ksq), axis=-1, keepdims=True)]
    for j in range(1, n_src):
        for kv_head in range(2):
            kx = kv_refs[2 * j][0, :, 2 * kv_head * LANES:(2 * kv_head + 1) * LANES].astype(F32)
            sq = jnp.sum(jnp.where(lane[0:1, :] < HEAD_DIM, kx * kx, 0.0), axis=-1, keepdims=True)
            ksq_head[kv_head] = jnp.maximum(ksq_head[kv_head], jnp.max(sq, axis=0, keepdims=True))

    q_shift, q_plain, bound_max = [], [], None
    for heads in _VARIANT_HEADS:
        shifted, plain = [], []
        for h in heads:
            qf = q_ref[0, :, (h // 2) * LANES:(h // 2 + 1) * LANES].astype(F32)
            own = lo_mask if h % 2 == 0 else jnp.logical_not(lo_mask)
            qf = jnp.where(own, qf, 0.0)
            bound = jnp.sqrt(jnp.sum(qf * qf, axis=-1, keepdims=True) * ksq_head[h // 4])
            bias_lane = HEAD_DIM if h % 2 == 0 else 0
            shifted.append(jnp.where(lane == bias_lane, -bound, qf).astype(BF16))
            plain.append(qf.astype(BF16))
            bound_max = bound if bound_max is None else jnp.maximum(bound_max, bound)
        q_shift.append(jnp.concatenate(shifted, axis=0))
        q_plain.append(jnp.concatenate(plain, axis=0))
    safe = jnp.max(bound_max) <= SHIFT_LIMIT

    cols = [slice(a * LANES, (a + 1) * LANES) for a in range(n_var)]
    chunks = []
    for j in range(n_src):
        k4_ref, v4_ref = kv_refs[2 * j], kv_refs[2 * j + 1]
        S = k4_ref.shape[1]
        size = min(tk, S)
        chunks += [(k4_ref, v4_ref, c * size, size) for c in range(S // size)]

    @pl.when(safe)
    def _():
        def scores(ch):
            k4_ref, _, start, size = ch
            return [_dot_nt(q_shift[a], k4_ref[0, start:start + size, cols[a]]) for a in range(n_var)]

        acc = [None] * n_var
        s_next = scores(chunks[0])
        for ci, (_, v4_ref, start, size) in enumerate(chunks):
            s = s_next
            if ci + 1 < len(chunks):
                s_next = scores(chunks[ci + 1])
            for a in range(n_var):
                pv = _dot(jnp.exp2(s[a]).astype(BF16), v4_ref[0, start:start + size, cols[a]])
                acc[a] = pv if acc[a] is None else acc[a] + pv
        for a in range(n_var):
            acc_ref[a] = acc[a]

    @pl.when(jnp.logical_not(safe))
    def _():
        m_ref[...] = jnp.full(m_ref.shape, MASK_VALUE, F32)
        acc_ref[...] = jnp.zeros(acc_ref.shape, F32)
        for k4_ref, v4_ref, start, size in chunks:
            for a in range(n_var):
                s = _dot_nt(q_plain[a], k4_ref[0, start:start + size, cols[a]])
                m_prev = m_ref[a]
                m_new = jnp.maximum(m_prev, jnp.max(s, axis=-1, keepdims=True))
                p = jnp.exp2(s - jnp.tile(m_new, (1, size // LANES)))
                m_ref[a] = m_new
                acc_ref[a] = jnp.exp2(m_prev - m_new) * acc_ref[a] + _dot(p.astype(BF16), v4_ref[0, start:start + size, cols[a]])

    def head_out(h):
        a = 2 * (h // 4) + h % 2
        r = _VARIANT_HEADS[a].index(h) * tq
        rows = acc_ref[a, r:r + tq, :]
        return rows / pltpu.roll(rows, HEAD_DIM, 1)

    for i in range(N_Q_HEADS // 2):
        o_ref[0, :, i * LANES:(i + 1) * LANES] = jnp.where(lo_mask, head_out(2 * i), head_out(2 * i + 1))


def _attention(q, ksq, sources, tq, tk):
    B, L, _ = q.shape
    n_var = len(_VARIANT_HEADS)
    in_specs = [pl.BlockSpec((1, tq, D_ATTN), lambda b, i: (b, i, 0)),
                pl.BlockSpec((1,) + ksq.shape[1:], lambda b, i: (b, 0, 0, 0))]
    args = [q, ksq]
    for k4, v4 in sources:
        S = k4.shape[1]
        in_specs += [pl.BlockSpec((1, S, n_var * LANES), lambda b, i: (b, 0, 0))] * 2
        args += [k4, v4]
    rows = N_Q_HEADS // n_var * tq
    return pl.pallas_call(
        functools.partial(_attn_kernel, len(sources), tk),
        grid=(B, L // tq),
        in_specs=in_specs,
        out_specs=pl.BlockSpec((1, tq, D_ATTN), lambda b, i: (b, i, 0)),
        out_shape=jax.ShapeDtypeStruct((B, L, D_ATTN), F32),
        scratch_shapes=[pltpu.VMEM((n_var, rows, LANES), F32)] * 2,
        compiler_params=_cparams("parallel", "parallel"),
        name="attention",
    )(*args)


def _delta_prep_kernel(cur_ref, prev_ref, next_ref, ba_ref, convw_ref, alog_ref, dtb_ref, act_ref, bg_ref, xp_ref):
    i = pl.program_id(1)
    tt = cur_ref.shape[1]
    first = i == 0
    last = i == pl.num_programs(1) - 1
    xp_ref[0:HALO, :] = jnp.where(first, 0.0, prev_ref[0])
    xp_ref[HALO:HALO + tt, :] = cur_ref[0]
    xp_ref[HALO + tt:2 * HALO + tt, :] = jnp.where(last, 0.0, next_ref[0])
    acc = convw_ref[0:1, :] * xp_ref[HALO - 2:HALO - 2 + tt, :]
    for j in range(1, 4):
        acc = acc + convw_ref[j:j + 1, :] * xp_ref[HALO - 2 + j:HALO - 2 + j + tt, :]
    act_ref[0] = _silu(acc)
    ba = ba_ref[0]
    lane = lax.broadcasted_iota(jnp.int32, ba.shape, 1)
    beta = jax.nn.sigmoid(ba)
    g = -jnp.exp(alog_ref[...]) * _softplus(ba + dtb_ref[...])
    bg_ref[0] = jnp.where(lane < 2 * N_DELTA_HEADS, beta, jnp.where(lane < 4 * N_DELTA_HEADS, g, 0.0))


def _delta_prep(qkvd, ba, conv_w, alog_row, dtb_row, tt):
    B, L, W = qkvd.shape
    nb = tt // HALO
    last_blk = L // HALO - 1
    return pl.pallas_call(
        _delta_prep_kernel,
        grid=(B, L // tt),
        in_specs=[pl.BlockSpec((1, tt, W), lambda b, i: (b, i, 0)),
                  pl.BlockSpec((1, HALO, W), lambda b, i: (b, jnp.maximum(i * nb - 1, 0), 0)),
                  pl.BlockSpec((1, HALO, W), lambda b, i: (b, jnp.minimum((i + 1) * nb, last_blk), 0)),
                  pl.BlockSpec((1, tt, LANES), lambda b, i: (b, i, 0)),
                  pl.BlockSpec((4, W), lambda b, i: (0, 0)),
                  pl.BlockSpec((1, LANES), lambda b, i: (0, 0)),
                  pl.BlockSpec((1, LANES), lambda b, i: (0, 0))],
        out_specs=[pl.BlockSpec((1, tt, W), lambda b, i: (b, i, 0)),
                   pl.BlockSpec((1, tt, LANES), lambda b, i: (b, i, 0))],
        out_shape=[jax.ShapeDtypeStruct((B, L, W), F32), jax.ShapeDtypeStruct((B, L, LANES), F32)],
        scratch_shapes=[pltpu.VMEM((tt + 2 * HALO, W), F32)],
        compiler_params=_cparams("parallel", "parallel"),
        name="delta_prep",
    )(qkvd, qkvd, qkvd, ba, conv_w, alog_row, dtb_row)


def _block_diag(x, bd_mask):
    return jnp.where(bd_mask, jnp.concatenate([x] * N_DELTA_HEADS, axis=0), jnp.zeros((), x.dtype))


def _dot_split_bd(a, b, bd_mask):
    (ah, al), (bh, bl) = a, b
    m = ah.shape[0]
    both = _dot(jnp.concatenate([ah, al], axis=0), _block_diag(bh, bd_mask))
    return both[0:m] + both[m:2 * m] + _dot(ah, _block_diag(bl, bd_mask))


def _delta_chunk_kernel(act_ref, bg_ref, gt_ref, pm_ref, bm_ref, ol_ref, e_ref):
    C, W, H = CHUNK, D_DELTA, N_DELTA_HEADS
    n_chunks = act_ref.shape[1] // C
    ri = lax.broadcasted_iota(jnp.int32, (C, W), 0)
    lane = lax.broadcasted_iota(jnp.int32, (C, W), 1)
    lj = lane % HEAD_DIM
    blk = lane // HEAD_DIM
    bd_mask = (lax.broadcasted_iota(jnp.int32, (W, W), 0) // HEAD_DIM) == (lax.broadcasted_iota(jnp.int32, (W, W), 1) // HEAD_DIM)
    diag = ri == lj
    eye = diag.astype(F32)

    def expand(cols):
        res = cols[H - 1]
        for h in range(H - 2, -1, -1):
            res = jnp.where(blk == h, cols[h], res)
        return res

    def seg_sum(x):
        return expand([jnp.sum(jnp.where(blk == h, x, 0.0), axis=-1, keepdims=True) for h in range(H)])

    def l2n(x):
        return x * lax.rsqrt(seg_sum(x * x) + EPS)

    def row_form(x):
        res = x[(H - 1) * C:H * C]
        for h in range(H - 2, -1, -1):
            res = jnp.where(blk == h, x[h * C:(h + 1) * C], res)
        return res

    units = []
    for c in range(n_chunks):
        rows = slice(c * C, (c + 1) * C)
        q = l2n(act_ref[0, rows, 0:W]) * (HEAD_DIM ** -0.5)
        k = l2n(act_ref[0, rows, W:2 * W])
        v = act_ref[0, rows, 2 * W:3 * W]
        kb = k.astype(BF16)
        kkqk = _dot_nt(jnp.concatenate([kb, q.astype(BF16)], axis=0), _block_diag(kb, bd_mask))
        kk, qk = kkqk[0:C], kkqk[C:2 * C]
        bg = bg_ref[0, rows, :]
        gt = gt_ref[0, c]
        for d in range(2):
            beta = expand([bg[:, d * H + h:d * H + h + 1] for h in range(H)])
            g = expand([bg[:, 2 * H + d * H + h:2 * H + d * H + h + 1] for h in range(H)])
            g_row = gt[d:d + 1, :]
            incl = (ri >= lj) if d == 0 else (ri <= lj)
            incl_t = (ri <= lj) if d == 0 else (ri >= lj)
            strict = (ri > lj) if d == 0 else (ri < lj)
            gc = seg_sum(jnp.where(incl, g_row, 0.0))
            gc_row = jnp.sum(jnp.where(incl_t, g, 0.0), axis=0, keepdims=True)
            g_tot = jnp.sum(g, axis=0, keepdims=True)
            decay = jnp.where(incl, jnp.exp(jnp.where(incl, gc - gc_row, 0.0)), 0.0)
            t = jnp.where(strict, -(beta * kk * decay), 0.0)
            units.append(dict(c=c, d=d, q=q, k=k, v=v, beta=beta, gc=gc, g_tot=g_tot,
                              attn=(qk * decay).astype(BF16), t=t, p=eye + t))

    for un in units:
        un["t"] = _dot_split_bd(_split(un["t"]), _split(un["t"]), bd_mask)
    for _ in range(4):
        for un in units:
            both = _dot_split_bd(_split(jnp.concatenate([un["t"], un["p"]], axis=0)), _split(un["t"]), bd_mask)
            un["t"] = both[0:C]
            un["p"] = un["p"] + both[C:2 * C]
    for un in units:
        un["p"] = (un["p"] + _dot_split_bd(_split(un["p"]), _split(un["t"]), bd_mask)).astype(BF16)
    for un in units:
        un["egc"] = jnp.exp(un["gc"])
        un["u"] = _dot(un["p"], _block_diag((un["v"] * un["beta"]).astype(BF16), bd_mask)).astype(BF16)
        un["w"] = _dot(un["p"], _block_diag((un["k"] * (un["beta"] * un["egc"])).astype(BF16), bd_mask)).astype(BF16)
    for un in units:
        c, d = un["c"], un["d"]
        kdec = (un["k"] * jnp.exp(un["g_tot"] - un["gc"])).astype(BF16)
        bm_ref[0, d, c] = row_form(_dot_tn(kdec, un["u"]))
        mp = row_form(_dot_tn(kdec, un["w"]))
        qt = un["q"] * un["egc"] - _dot(un["attn"], _block_diag(un["w"], bd_mask))
        ol_ref[0, d, c] = _dot(un["attn"], _block_diag(un["u"], bd_mask))
        pm_ref[0, d, c] = jnp.concatenate([mp, qt], axis=0).astype(BF16)
        e_ref[0, d, c] = jnp.broadcast_to(jnp.exp(un["g_tot"]), (HALO, W))


def _delta_chunks(act, bg, gt, tc):
    B, L, _ = act.shape
    N = L // CHUNK
    cb = tc // CHUNK
    W = D_DELTA
    ospec = lambda r: pl.BlockSpec((1, 2, cb, r, W), lambda b, i: (b, 0, i, 0, 0))
    return pl.pallas_call(
        _delta_chunk_kernel,
        grid=(B, L // tc),
        in_specs=[pl.BlockSpec((1, tc, 3 * W), lambda b, i: (b, i, 0)),
                  pl.BlockSpec((1, tc, LANES), lambda b, i: (b, i, 0)),
                  pl.BlockSpec((1, cb, 2, W), lambda b, i: (b, i, 0, 0))],
        out_specs=[ospec(2 * CHUNK), ospec(CHUNK), ospec(CHUNK), ospec(HALO)],
        out_shape=[jax.ShapeDtypeStruct((B, 2, N, 2 * CHUNK, W), BF16),
                   jax.ShapeDtypeStruct((B, 2, N, CHUNK, W), F32),
                   jax.ShapeDtypeStruct((B, 2, N, CHUNK, W), F32),
                   jax.ShapeDtypeStruct((B, 2, N, HALO, W), F32)],
        compiler_params=_cparams("parallel", "parallel"),
        name="delta_chunks",
    )(act, bg, gt)


def _delta_scan_kernel(pmf_ref, bmf_ref, olf_ref, ef_ref, pmb_ref, bmb_ref, olb_ref, eb_ref, s0_ref,
                       of_ref, ob_ref, sout_ref, s_ref):
    n = pl.program_id(1)
    bb = s_ref.shape[0]
    C, W = CHUNK, D_DELTA
    bd_mask = (lax.broadcasted_iota(jnp.int32, (W, W), 0) // HEAD_DIM) == (lax.broadcasted_iota(jnp.int32, (W, W), 1) // HEAD_DIM)

    @pl.when(n == 0)
    def _():
        s_ref[...] = s0_ref[...]

    dirs = ((pmf_ref, bmf_ref, olf_ref, ef_ref, of_ref), (pmb_ref, bmb_ref, olb_ref, eb_ref, ob_ref))
    chains = [(b, d) for b in range(bb) for d in range(2)]
    s = [s_ref[b, d] for b, d in chains]
    r = [_dot(dirs[d][0][b, 0, 0], _block_diag(s[i].astype(BF16), bd_mask)) for i, (b, d) in enumerate(chains)]
    for i, (b, d) in enumerate(chains):
        _, bm_ref, ol_ref, e_ref, o_ref = dirs[d]
        s_ref[b, d] = e_ref[b, 0, 0][0:1, :] * s[i] + bm_ref[b, 0, 0] - r[i][0:C]
        o_ref[b] = r[i][C:2 * C] + ol_ref[b, 0, 0]

    @pl.when(n == pl.num_programs(1) - 1)
    def _():
        sout_ref[...] = s_ref[...]


def _delta_scan(pm, bm, ol, e, s0, bb):
    B, _, N, _, W = pm.shape
    L = N * CHUNK

    def spec(r, d):
        if d == 0:
            return pl.BlockSpec((bb, 1, 1, r, W), lambda b, n: (b, 0, n, 0, 0))
        return pl.BlockSpec((bb, 1, 1, r, W), lambda b, n: (b, 1, N - 1 - n, 0, 0))

    in_specs = []
    for d in range(2):
        in_specs += [spec(2 * CHUNK, d), spec(CHUNK, d), spec(CHUNK, d), spec(HALO, d)]
    in_specs.append(pl.BlockSpec((bb, 2, CHUNK, W), lambda b, n: (b, 0, 0, 0)))
    return pl.pallas_call(
        _delta_scan_kernel,
        grid=(B // bb, N),
        in_specs=in_specs,
        out_specs=[pl.BlockSpec((bb, CHUNK, W), lambda b, n: (b, n, 0)),
                   pl.BlockSpec((bb, CHUNK, W), lambda b, n: (b, N - 1 - n, 0)),
                   pl.BlockSpec((bb, 2, CHUNK, W), lambda b, n: (b, 0, 0, 0))],
        out_shape=[jax.ShapeDtypeStruct((B, L, W), F32),
                   jax.ShapeDtypeStruct((B, L, W), F32),
                   jax.ShapeDtypeStruct((B, 2, CHUNK, W), F32)],
        scratch_shapes=[pltpu.VMEM((bb, 2, CHUNK, W), F32)],
        compiler_params=_cparams("parallel", "arbitrary"),
        name="delta_scan",
    )(pm, bm, ol, e, pm, bm, ol, e, s0)


def _merge_kernel(x_ref, mod_ref, up_ref, upp_ref, upn_ref, cnt_ref, gates_ref, oattn_ref, of_ref, ob_ref,
                  pw_ref, pscale_ref, onorm_ref, wout_ref, gpost_ref, o_ref, pad_ref):
    i = pl.program_id(1)
    tm = x_ref.shape[1]
    first = i == 0
    last = i == pl.num_programs(1) - 1
    gates = gates_ref[0]
    y = _dot((gates[:, D_POOL:D_POOL + D_ATTN] * oattn_ref[0]).astype(BF16), wout_ref[D_POOL:D_POOL + D_ATTN, :])
    pad_ref[0:HALO, :] = jnp.where(first, 0.0, upp_ref[0])
    pad_ref[HALO:HALO + tm, :] = up_ref[0]
    pad_ref[HALO + tm:2 * HALO + tm, :] = jnp.where(last, 0.0, upn_ref[0])

    def window_sum(col, lo, hi):
        acc = pad_ref[HALO + lo:HALO + lo + tm, col]
        for j in range(lo + 1, hi):
            acc = acc + pad_ref[HALO + j:HALO + j + tm, col]
        return acc

    lane = lax.broadcasted_iota(jnp.int32, (tm, LANES), 1)
    lo_mask = lane < HEAD_DIM

    pooled = []
    for col_blk, (w_lo, w_hi) in enumerate(((2, 4), (8, 16))):
        col = slice(col_blk * LANES, (col_blk + 1) * LANES)
        s_lo = window_sum(col, -(w_lo // 2), w_lo // 2)
        s_hi = s_lo + window_sum(col, -(w_hi // 2), -(w_lo // 2)) + window_sum(col, w_lo // 2, w_hi // 2)
        mean = jnp.where(lo_mask, s_lo, s_hi) / cnt_ref[:, col]
        pooled.append(mean - up_ref[0, :, col])
    pooled = jnp.concatenate(pooled, axis=1)
    o_pool = _dot(pooled.astype(BF16), pw_ref[...]) * pscale_ref[...]

    od = of_ref[0] + ob_ref[0]
    odn = []
    for j in range(D_DELTA // LANES):
        blk = od[:, j * LANES:(j + 1) * LANES]
        odn.append(blk * _half_rms_scale(blk, lo_mask) * onorm_ref[...])
    o_delta = jnp.concatenate(odn, axis=1)

    y = y + _dot((gates[:, 0:D_POOL] * o_pool).astype(BF16), wout_ref[0:D_POOL, :])
    y = y + _dot((gates[:, D_POOL + D_ATTN:] * o_delta).astype(BF16), wout_ref[D_POOL + D_ATTN:, :])
    ms = jnp.mean(y * y, axis=-1, keepdims=True)
    yn = (y * lax.rsqrt(ms + EPS)) * gpost_ref[...]
    gate = mod_ref[0][:, 2 * D_MODEL:3 * D_MODEL]
    o_ref[0] = x_ref[0] + gate * yn


def _window_counts(seq_len):
    t = jnp.arange(seq_len)[:, None]
    w = jnp.repeat(jnp.array(POOL_WINDOWS), D_POOL // len(POOL_WINDOWS))[None, :]
    return (jnp.minimum(t - w // 2 + w, seq_len) - jnp.maximum(t - w // 2, 0)).astype(F32)


def _merge(x, mod, up, gates, oattn, o_f, o_b, pw_bd, pscale, onorm_row, wout, gpost, tm):
    B, L, _ = x.shape
    nb = tm // HALO
    last_blk = L // HALO - 1
    shared_mod = mod.shape[0] == 1
    row = lambda b, i: (b, i, 0)
    const2 = lambda b, i: (0, 0)
    return pl.pallas_call(
        _merge_kernel,
        grid=(B, L // tm),
        in_specs=[pl.BlockSpec((1, tm, D_MODEL), row),
                  pl.BlockSpec((1, 1, 3 * D_MODEL), (lambda b, i: (0, 0, 0)) if shared_mod else (lambda b, i: (b, 0, 0))),
                  pl.BlockSpec((1, tm, D_POOL), row),
                  pl.BlockSpec((1, HALO, D_POOL), lambda b, i: (b, jnp.maximum(i * nb - 1, 0), 0)),
                  pl.BlockSpec((1, HALO, D_POOL), lambda b, i: (b, jnp.minimum((i + 1) * nb, last_blk), 0)),
                  pl.BlockSpec((tm, D_POOL), lambda b, i: (i, 0)),
                  pl.BlockSpec((1, tm, D_MODEL), row),
                  pl.BlockSpec((1, tm, D_ATTN), row),
                  pl.BlockSpec((1, tm, D_DELTA), row),
                  pl.BlockSpec((1, tm, D_DELTA), row),
                  pl.BlockSpec((D_POOL, D_POOL), const2),
                  pl.BlockSpec((1, D_POOL), const2),
                  pl.BlockSpec((1, LANES), const2),
                  pl.BlockSpec((D_MODEL, D_MODEL), const2),
                  pl.BlockSpec((1, D_MODEL), const2)],
        out_specs=pl.BlockSpec((1, tm, D_MODEL), row),
        out_shape=jax.ShapeDtypeStruct((B, L, D_MODEL), F32),
        scratch_shapes=[pltpu.VMEM((tm + 2 * HALO, D_POOL), F32)],
        compiler_params=_cparams("parallel", "parallel"),
        name="merge",
    )(x, mod, up, up, up, _window_counts(L), gates, oattn, o_f, o_b, pw_bd, pscale, onorm_row, wout, gpost)


def _rope_tables(num_tokens):
    rows = num_tokens // GRID_W
    row = jnp.repeat(jnp.arange(rows, dtype=F32), GRID_W)
    col = (jnp.arange(rows * GRID_W) % GRID_W).astype(F32)
    axis_dim = HEAD_DIM // 2
    inv = ROPE_THETA ** (-jnp.arange(0, axis_dim, 2, dtype=F32) / axis_dim)
    ang = jnp.concatenate([row[:, None] * inv, col[:, None] * inv], axis=-1)
    cos = jnp.repeat(jnp.cos(ang), 2, axis=-1)
    sin = jnp.repeat(jnp.sin(ang), 2, axis=-1)
    sign = jnp.tile(jnp.array([-1.0, 1.0], F32), HEAD_DIM // 2)
    return jnp.tile(cos, (1, 2)), jnp.tile(sin * sign, (1, 2))


def _permute_w_in(w):
    u_pool, g_pool, q, k, v, g_attn, qkv_d, b_d, a_d, g_delta = jnp.split(
        w, [256, 512, 1024, 1152, 1280, 1792, 2560, 2568, 2576], axis=1)
    pad = jnp.zeros((w.shape[0], LANES - 4 * N_DELTA_HEADS), w.dtype)
    return jnp.concatenate([u_pool, g_pool, g_attn, g_delta, q, k, v, qkv_d, b_d, a_d, pad], axis=1).astype(BF16)


def _lane_row(vec, offset):
    return jnp.zeros((1, LANES), F32).at[0, offset:offset + vec.size].set(vec.reshape(-1))


def _cached_kv_variants(k, v):
    lane = jnp.arange(LANES)
    pairs = _kv_variants(k, jnp.roll(k, HEAD_DIM, axis=-1), v, jnp.roll(v, HEAD_DIM, axis=-1), lane)
    return (jnp.concatenate([p[0] for p in pairs], axis=-1).astype(BF16),
            jnp.concatenate([p[1] for p in pairs], axis=-1).astype(BF16))


def _layer(x, mod, lw, rope_tabs, ctx, tiles):
    B, L, _ = x.shape
    latent = ctx is not None
    outs = _inproj(x, mod, lw["gpre"], lw["w_in"], lw["qkn"], rope_tabs, tiles["tm"])
    up, gates, q, k4, v4, ksq, qkvd, ba = outs[:8]
    sources = [(k4, v4)]
    if latent:
        sources.append(ctx["kv"])
    oattn = _attention(q, ksq, sources, tiles["tq"], tiles["tk"])

    act, bg = _delta_prep(qkvd, ba, lw["conv_w"], lw["alog_row"], lw["dtb_row"], tiles["tt"])
    gt = bg[:, :, 2 * N_DELTA_HEADS:4 * N_DELTA_HEADS].reshape(B, L // CHUNK, CHUNK, 2, N_DELTA_HEADS)
    gt = gt.transpose(0, 1, 3, 4, 2).reshape(B, L // CHUNK, 2, D_DELTA)
    pm, bm, ol, e = _delta_chunks(act, bg, gt, tiles["tc"])
    s0 = ctx["state"] if latent else jnp.zeros((B, 2, HEAD_DIM, D_DELTA), F32)
    o_f, o_b, s_out = _delta_scan(pm, bm, ol, e, s0, tiles["bb"])

    y = _merge(x, mod, up, gates, oattn, o_f, o_b, lw["pw_bd"], lw["pscale"], lw["onorm_row"],
               lw["w_out"], lw["gpost"], tiles["tm"])
    if latent:
        return y
    kn, v = outs[8], outs[9]
    s_out = s_out.reshape(B, 2, HEAD_DIM, N_DELTA_HEADS, HEAD_DIM).transpose(0, 1, 3, 2, 4)
    return y, kn, v, s_out


def kernel(x_prompt, x_sample, cache_attn_k, cache_attn_v, state_delta, c, c_ctx, w_mod, b_mod, norm_pre, norm_post,
           w_in, w_out, pool_w, pool_scale, q_norm, k_norm, conv_w, a_log, dt_bias, o_norm):
    B, L, _ = x_prompt.shape
    DB, DL, _ = x_sample.shape
    past = cache_attn_k.shape[2]

    conds = jnp.zeros((8, D_MODEL), F32).at[0].set(c_ctx).at[1:1 + DB].set(c)
    mod = _modulation(conds, w_mod, b_mod)
    rope_tabs = _rope_tables(DL)

    ctx_tiles = dict(tm=256, tq=256, tk=256, tt=256, tc=256, bb=8)
    lat_tiles = dict(tm=512, tq=256, tk=1024, tt=512, tc=256, bb=4)

    hp, hs = x_prompt, x_sample
    new_k, new_v, new_s = [], [], []
    for l in range(DEPTH):
        blocks = jnp.zeros((N_DELTA_HEADS, HEAD_DIM, N_DELTA_HEADS, HEAD_DIM), F32)
        blocks = blocks.at[jnp.arange(4), :, jnp.arange(4), :].set(pool_w[l])
        qkn = jnp.zeros((8, LANES), F32).at[0].set(jnp.tile(q_norm[l], 2)).at[1].set(jnp.tile(k_norm[l], 2))
        lw = dict(
            gpre=norm_pre[l].reshape(1, D_MODEL),
            gpost=norm_post[l].reshape(1, D_MODEL),
            w_in=_permute_w_in(w_in[l]),
            w_out=w_out[l].astype(BF16),
            qkn=qkn,
            conv_w=conv_w[l],
            alog_row=_lane_row(a_log[l], 2 * N_DELTA_HEADS),
            dtb_row=_lane_row(dt_bias[l], 2 * N_DELTA_HEADS),
            pw_bd=blocks.reshape(D_POOL, D_POOL).astype(BF16),
            pscale=pool_scale[l].reshape(1, D_POOL),
            onorm_row=jnp.tile(o_norm[l], 2).reshape(1, LANES),
        )
        hp, k_l, v_l, s_l = _layer(hp, mod[l, 0:1].reshape(1, 1, 3 * D_MODEL), lw, None, None, ctx_tiles)
        ctx = dict(kv=_cached_kv_variants(cache_attn_k[:, l].reshape(DB, past, D_KV),
                                          cache_attn_v[:, l].reshape(DB, past, D_KV)),
                   state=state_delta[:, l].transpose(0, 1, 3, 2, 4).reshape(DB, 2, HEAD_DIM, D_DELTA))
        hs = _layer(hs, mod[l, 1:1 + DB].reshape(DB, 1, 3 * D_MODEL), lw, rope_tabs, ctx, lat_tiles)
        new_k.append(k_l.reshape(B, L, 2, HEAD_DIM))
        new_v.append(v_l.reshape(B, L, 2, HEAD_DIM))
        new_s.append(s_l)
    return (hp, hs, jnp.stack(new_k, axis=1), jnp.stack(new_v, axis=1), jnp.stack(new_s, axis=1))
```

```python
import functools

import jax
import jax.numpy as jnp
from jax import lax
from jax.experimental import pallas as pl
from jax.experimental.pallas import tpu as pltpu

F32 = jnp.float32
BF16 = jnp.bfloat16

D_MODEL = 1024
DEPTH = 2
GRID_W = 64
HEAD_DIM = 64
D_POOL = 256
D_ATTN = 512
D_DELTA = 256
D_KV = 128
N_Q_HEADS = 8
N_DELTA_HEADS = 4
POOL_WINDOWS = (2, 4, 8, 16)
CHUNK = 64
ROPE_THETA = 10000.0
EPS = 1e-6
LANES = 128
HALO = 8
MASK_VALUE = -1e30
LOG2E = 1.4426950408889634
Q_SCALE = HEAD_DIM ** -0.5 * LOG2E
SHIFT_LIMIT = 60.0

OFF_UP = 0
OFF_GATES = 256
OFF_QK = 1280
OFF_V = 1920
OFF_QKVD = 2048
OFF_BA = 2816
D_IN_PAD = 2944

VMEM_LIMIT = 56 * 1024 * 1024


def _cparams(*sem):
    return pltpu.CompilerParams(dimension_semantics=sem, vmem_limit_bytes=VMEM_LIMIT)


def _dot(a, b):
    return jnp.dot(a, b, preferred_element_type=F32)


def _dot_nt(a, b):
    return lax.dot_general(a, b, (((1,), (1,)), ((), ())), preferred_element_type=F32)


def _dot_tn(a, b):
    return lax.dot_general(a, b, (((0,), (0,)), ((), ())), preferred_element_type=F32)


def _split(x):
    hi = x.astype(BF16)
    return hi, (x - hi.astype(F32)).astype(BF16)


def _silu(x):
    return x * jax.nn.sigmoid(x)


def _softplus(x):
    return jnp.maximum(x, 0.0) + jnp.log1p(jnp.exp(-jnp.abs(x)))


def _kv_variants(k, k_sw, v, v_sw, lane):
    lo = lane < HEAD_DIM
    one_hi = (lane == HEAD_DIM).astype(k.dtype)
    one_lo = (lane == 0).astype(k.dtype)
    ones = jnp.ones_like(v)
    return ((jnp.where(lo, k, one_hi), jnp.where(lo, v, ones)),
            (jnp.where(lo, one_lo, k_sw), jnp.where(lo, ones, v_sw)),
            (jnp.where(lo, k_sw, one_hi), jnp.where(lo, v_sw, ones)),
            (jnp.where(lo, one_lo, k), jnp.where(lo, ones, v)))


def _half_rms_scale(x, lo_mask):
    sq = x * x
    ss_lo = jnp.sum(jnp.where(lo_mask, sq, 0.0), axis=-1, keepdims=True)
    ss_hi = jnp.sum(jnp.where(lo_mask, 0.0, sq), axis=-1, keepdims=True)
    r_lo = lax.rsqrt(ss_lo * (1.0 / HEAD_DIM) + EPS)
    r_hi = lax.rsqrt(ss_hi * (1.0 / HEAD_DIM) + EPS)
    return jnp.where(lo_mask, r_lo, r_hi)


def _mod_kernel(c_ref, w_ref, b_ref, o_ref):
    s = _silu(c_ref[...])
    o_ref[0] = _dot(s.astype(BF16), w_ref[0].astype(BF16)) + b_ref[0]


def _modulation(conds, w_mod, b_mod):
    tn = 768
    return pl.pallas_call(
        _mod_kernel,
        grid=(DEPTH, 3 * D_MODEL // tn),
        in_specs=[pl.BlockSpec((8, D_MODEL), lambda l, j: (0, 0)),
                  pl.BlockSpec((1, D_MODEL, tn), lambda l, j: (l, 0, j)),
                  pl.BlockSpec((1, 1, tn), lambda l, j: (l, 0, j))],
        out_specs=pl.BlockSpec((1, 8, tn), lambda l, j: (l, 0, j)),
        out_shape=jax.ShapeDtypeStruct((DEPTH, 8, 3 * D_MODEL), F32),
        compiler_params=_cparams("parallel", "parallel"),
        name="modulation",
    )(conds, w_mod, b_mod.reshape(DEPTH, 1, 3 * D_MODEL))


def _inproj_kernel(rope, x_ref, mod_ref, gpre_ref, w_ref, qkn_ref, *rest):
    if rope:
        cos_ref, sin_ref, up_ref, gates_ref, q_ref, k4_ref, v4_ref, ksq_ref, qkvd_ref, ba_ref = rest
    else:
        up_ref, gates_ref, q_ref, k4_ref, v4_ref, ksq_ref, qkvd_ref, ba_ref, kn_ref, v_ref = rest
    x = x_ref[0]
    tm = x.shape[0]
    mod = mod_ref[0]
    shift = mod[:, 0:D_MODEL]
    scale = mod[:, D_MODEL:2 * D_MODEL]
    ms = jnp.mean(x * x, axis=-1, keepdims=True)
    h = (x * lax.rsqrt(ms + EPS)) * gpre_ref[...] * (1.0 + scale) + shift
    hb = h.astype(BF16)

    zqk = _dot(hb, w_ref[:, OFF_QK:OFF_QK + D_ATTN + D_KV])
    v = _dot(hb, w_ref[:, OFF_V:OFF_V + D_KV])
    zgates = _dot(hb, w_ref[:, OFF_GATES:OFF_GATES + D_MODEL])

    lane = lax.broadcasted_iota(jnp.int32, (tm, LANES), 1)
    lo_mask = lane < HEAD_DIM
    even = (lane % 2) == 0

    def head_norm_rope(blk, gain):
        y = blk * _half_rms_scale(blk, lo_mask) * gain
        if rope:
            swapped = jnp.where(even, pltpu.roll(y, LANES - 1, 1), pltpu.roll(y, 1, 1))
            return y, y * cos_ref[...] + swapped * sin_ref[...]
        return y, y

    for i in range(D_ATTN // LANES):
        _, qr = head_norm_rope(zqk[:, i * LANES:(i + 1) * LANES], qkn_ref[0:1, :])
        q_ref[0, :, i * LANES:(i + 1) * LANES] = (qr * Q_SCALE).astype(BF16)
    gates_ref[0] = _silu(zgates).astype(BF16)
    qkvd_ref[0] = _dot(hb, w_ref[:, OFF_QKVD:OFF_QKVD + 3 * D_DELTA])
    kn, kr = head_norm_rope(zqk[:, D_ATTN:D_ATTN + D_KV], qkn_ref[1:2, :])
    up_ref[0] = _dot(hb, w_ref[:, OFF_UP:OFF_UP + D_POOL])
    ba_ref[0] = _dot(hb, w_ref[:, OFF_BA:OFF_BA + LANES])
    for a, (kx, vx) in enumerate(_kv_variants(kr, pltpu.roll(kr, HEAD_DIM, 1), v, pltpu.roll(v, HEAD_DIM, 1), lane)):
        k4_ref[0, :, a * LANES:(a + 1) * LANES] = kx.astype(BF16)
        v4_ref[0, :, a * LANES:(a + 1) * LANES] = vx.astype(BF16)
    ksq = kr * kr
    ksq = jnp.where(lo_mask, jnp.sum(jnp.where(lo_mask, ksq, 0.0), axis=-1, keepdims=True),
                    jnp.sum(jnp.where(lo_mask, 0.0, ksq), axis=-1, keepdims=True))
    ksq_ref[0, 0] = jnp.broadcast_to(jnp.max(ksq, axis=0, keepdims=True), (HALO, LANES))
    if not rope:
        kn_ref[0] = kn
        v_ref[0] = v


def _inproj(x, mod, gpre, w, qkn, rope_tabs, tm):
    B, L, _ = x.shape
    rope = rope_tabs is not None
    shared_mod = mod.shape[0] == 1
    row = lambda b, i: (b, i, 0)
    const2 = lambda b, i: (0, 0)
    in_specs = [pl.BlockSpec((1, tm, D_MODEL), row),
                pl.BlockSpec((1, 1, 3 * D_MODEL), (lambda b, i: (0, 0, 0)) if shared_mod else (lambda b, i: (b, 0, 0))),
                pl.BlockSpec((1, D_MODEL), const2),
                pl.BlockSpec((D_MODEL, D_IN_PAD), const2),
                pl.BlockSpec((8, LANES), const2)]
    args = [x, mod, gpre, w, qkn]
    if rope:
        in_specs += [pl.BlockSpec((tm, LANES), lambda b, i: (i, 0))] * 2
        args += list(rope_tabs)
    widths = [(D_POOL, F32), (D_MODEL, BF16), (D_ATTN, BF16), (4 * LANES, BF16), (4 * LANES, BF16), None,
              (3 * D_DELTA, F32), (LANES, F32)]
    if not rope:
        widths += [(D_KV, F32), (D_KV, F32)]
    out_specs = [pl.BlockSpec((1, tm, wd[0]), row) if wd else pl.BlockSpec((1, 1, HALO, LANES), lambda b, i: (b, i, 0, 0))
                 for wd in widths]
    out_shape = [jax.ShapeDtypeStruct((B, L, wd[0]), wd[1]) if wd else jax.ShapeDtypeStruct((B, L // tm, HALO, LANES), F32)
                 for wd in widths]
    return pl.pallas_call(
        functools.partial(_inproj_kernel, rope),
        grid=(B, L // tm),
        in_specs=in_specs,
        out_specs=out_specs,
        out_shape=out_shape,
        compiler_params=_cparams("parallel", "parallel"),
        name="inproj_rope" if rope else "inproj",
    )(*args)


_VARIANT_HEADS = ((0, 2), (1, 3), (4, 6), (5, 7))


def _attn_kernel(n_src, tk, q_ref, ksq_ref, *rest):
    kv_refs = rest[:2 * n_src]
    o_ref, acc_ref, m_ref = rest[2 * n_src:]
    tq = q_ref.shape[1]
    n_var = len(_VARIANT_HEADS)
    lane = lax.broadcasted_iota(jnp.int32, (tq, LANES), 1)
    lo_mask = lane < HEAD_DIM

    ksq = jnp.max(ksq_ref[0], axis=0)[0:1, :]
    lo_row = lo_mask[0:1, :]
    ksq_head = [jnp.max(jnp.where(lo_row, ksq, 0.0), axis=-1, keepdims=True),
                jnp.max(jnp.where(lo_row, 0.0, ksq), axis=-1, keepdims=True)]
    for j in range(1, n_src):
        for kv_head in range(2):
            kx = kv_refs[2 * j][0, :, 2 * kv_head * LANES:(2 * kv_head + 1) * LANES].astype(F32)
            sq = jnp.sum(jnp.where(lane[0:1, :] < HEAD_DIM, kx * kx, 0.0), axis=-1, keepdims=True)
            ksq_head[kv_head] = jnp.maximum(ksq_head[kv_head], jnp.max(sq, axis=0, keepdims=True))

    q_shift, q_plain, bound_max = [], [], None
    for heads in _VARIANT_HEADS:
        shifted, plain = [], []
        for h in heads:
            qf = q_ref[0, :, (h // 2) * LANES:(h // 2 + 1) * LANES].astype(F32)
            own = lo_mask if h % 2 == 0 else jnp.logical_not(lo_mask)
            qf = jnp.where(own, qf, 0.0)
            bound = jnp.sqrt(jnp.sum(qf * qf, axis=-1, keepdims=True) * ksq_head[h // 4])
            bias_lane = HEAD_DIM if h % 2 == 0 else 0
            shifted.append(jnp.where(lane == bias_lane, -bound, qf).astype(BF16))
            plain.append(qf.astype(BF16))
            bound_max = bound if bound_max is None else jnp.maximum(bound_max, bound)
        q_shift.append(jnp.concatenate(shifted, axis=0))
        q_plain.append(jnp.concatenate(plain, axis=0))
    safe = jnp.max(bound_max) <= SHIFT_LIMIT

    cols = [slice(a * LANES, (a + 1) * LANES) for a in range(n_var)]
    chunks = []
    for j in range(n_src):
        k4_ref, v4_ref = kv_refs[2 * j], kv_refs[2 * j + 1]
        S = k4_ref.shape[1]
        size = min(tk, S)
        chunks += [(k4_ref, v4_ref, c * size, size) for c in range(S // size)]

    @pl.when(safe)
    def _():
        def scores(ch):
            k4_ref, _, start, size = ch
            return [_dot_nt(q_shift[a], k4_ref[0, start:start + size, cols[a]]) for a in range(n_var)]

        acc = [None] * n_var
        s_next = scores(chunks[0])
        for ci, (_, v4_ref, start, size) in enumerate(chunks):
            s = s_next
            if ci + 1 < len(chunks):
                s_next = scores(chunks[ci + 1])
            for a in range(n_var):
                pv = _dot(jnp.exp2(s[a]).astype(BF16), v4_ref[0, start:start + size, cols[a]])
                acc[a] = pv if acc[a] is None else acc[a] + pv
        for a in range(n_var):
            acc_ref[a] = acc[a]

    @pl.when(jnp.logical_not(safe))
    def _():
        m_ref[...] = jnp.full(m_ref.shape, MASK_VALUE, F32)
        acc_ref[...] = jnp.zeros(acc_ref.shape, F32)
        for k4_ref, v4_ref, start, size in chunks:
            for a in range(n_var):
                s = _dot_nt(q_plain[a], k4_ref[0, start:start + size, cols[a]])
                m_prev = m_ref[a]
                m_new = jnp.maximum(m_prev, jnp.max(s, axis=-1, keepdims=True))
                p = jnp.exp2(s - jnp.tile(m_new, (1, size // LANES)))
                m_ref[a] = m_new
                acc_ref[a] = jnp.exp2(m_prev - m_new) * acc_ref[a] + _dot(p.astype(BF16), v4_ref[0, start:start + size, cols[a]])

    def head_out(h):
        a = 2 * (h // 4) + h % 2
        r = _VARIANT_HEADS[a].index(h) * tq
        rows = acc_ref[a, r:r + tq, :]
        return rows / pltpu.roll(rows, HEAD_DIM, 1)

    for i in range(N_Q_HEADS // 2):
        o_ref[0, :, i * LANES:(i + 1) * LANES] = jnp.where(lo_mask, head_out(2 * i), head_out(2 * i + 1)).astype(BF16)


def _attention(q, ksq, sources, tq, tk):
    B, L, _ = q.shape
    n_var = len(_VARIANT_HEADS)
    in_specs = [pl.BlockSpec((1, tq, D_ATTN), lambda b, i: (b, i, 0)),
                pl.BlockSpec((1,) + ksq.shape[1:], lambda b, i: (b, 0, 0, 0))]
    args = [q, ksq]
    for k4, v4 in sources:
        S = k4.shape[1]
        in_specs += [pl.BlockSpec((1, S, n_var * LANES), lambda b, i: (b, 0, 0))] * 2
        args += [k4, v4]
    rows = N_Q_HEADS // n_var * tq
    return pl.pallas_call(
        functools.partial(_attn_kernel, len(sources), tk),
        grid=(B, L // tq),
        in_specs=in_specs,
        out_specs=pl.BlockSpec((1, tq, D_ATTN), lambda b, i: (b, i, 0)),
        out_shape=jax.ShapeDtypeStruct((B, L, D_ATTN), BF16),
        scratch_shapes=[pltpu.VMEM((n_var, rows, LANES), F32)] * 2,
        compiler_params=_cparams("parallel", "parallel"),
        name="attention",
    )(*args)


def _delta_prep_kernel(cur_ref, prev_ref, next_ref, ba_ref, convw_ref, alog_ref, dtb_ref, act_ref, bg_ref, xp_ref):
    i = pl.program_id(1)
    tt = cur_ref.shape[1]
    first = i == 0
    last = i == pl.num_programs(1) - 1
    xp_ref[0:HALO, :] = jnp.where(first, 0.0, prev_ref[0])
    xp_ref[HALO:HALO + tt, :] = cur_ref[0]
    xp_ref[HALO + tt:2 * HALO + tt, :] = jnp.where(last, 0.0, next_ref[0])
    xp = xp_ref[...]
    rows = xp.shape[0]
    acc = None
    for j in range(4):
        shifted = xp if j == 2 else pltpu.roll(xp, (2 - j) % rows, 0)
        term = convw_ref[j:j + 1, :] * shifted[HALO:HALO + tt, :]
        acc = term if acc is None else acc + term
    act_ref[0] = _silu(acc)
    ba = ba_ref[0]
    lane = lax.broadcasted_iota(jnp.int32, ba.shape, 1)
    beta = jax.nn.sigmoid(ba)
    g = -jnp.exp(alog_ref[...]) * _softplus(ba + dtb_ref[...])
    bg_ref[0] = jnp.where(lane < 2 * N_DELTA_HEADS, beta, jnp.where(lane < 4 * N_DELTA_HEADS, g, 0.0))


def _delta_prep(qkvd, ba, conv_w, alog_row, dtb_row, tt):
    B, L, W = qkvd.shape
    nb = tt // HALO
    last_blk = L // HALO - 1
    return pl.pallas_call(
        _delta_prep_kernel,
        grid=(B, L // tt),
        in_specs=[pl.BlockSpec((1, tt, W), lambda b, i: (b, i, 0)),
                  pl.BlockSpec((1, HALO, W), lambda b, i: (b, jnp.maximum(i * nb - 1, 0), 0)),
                  pl.BlockSpec((1, HALO, W), lambda b, i: (b, jnp.minimum((i + 1) * nb, last_blk), 0)),
                  pl.BlockSpec((1, tt, LANES), lambda b, i: (b, i, 0)),
                  pl.BlockSpec((4, W), lambda b, i: (0, 0)),
                  pl.BlockSpec((1, LANES), lambda b, i: (0, 0)),
                  pl.BlockSpec((1, LANES), lambda b, i: (0, 0))],
        out_specs=[pl.BlockSpec((1, tt, W), lambda b, i: (b, i, 0)),
                   pl.BlockSpec((1, tt, LANES), lambda b, i: (b, i, 0))],
        out_shape=[jax.ShapeDtypeStruct((B, L, W), F32), jax.ShapeDtypeStruct((B, L, LANES), F32)],
        scratch_shapes=[pltpu.VMEM((tt + 2 * HALO, W), F32)],
        compiler_params=_cparams("parallel", "parallel"),
        name="delta_prep",
    )(qkvd, qkvd, qkvd, ba, conv_w, alog_row, dtb_row)


def _block_diag(x, bd_mask):
    return jnp.where(bd_mask, jnp.concatenate([x] * N_DELTA_HEADS, axis=0), jnp.zeros((), x.dtype))


def _dot_split_bd(a, b, bd_mask):
    (ah, al), (bh, bl) = a, b
    m = ah.shape[0]
    both = _dot(jnp.concatenate([ah, al], axis=0), _block_diag(bh, bd_mask))
    return both[0:m] + both[m:2 * m] + _dot(ah, _block_diag(bl, bd_mask))


def _delta_chunk_kernel(act_ref, bg_ref, gt_ref, pm_ref, bm_ref, ol_ref, e_ref):
    C, W, H = CHUNK, D_DELTA, N_DELTA_HEADS
    n_chunks = act_ref.shape[1] // C
    ri = lax.broadcasted_iota(jnp.int32, (C, W), 0)
    lane = lax.broadcasted_iota(jnp.int32, (C, W), 1)
    lj = lane % HEAD_DIM
    blk = lane // HEAD_DIM
    bd_mask = (lax.broadcasted_iota(jnp.int32, (W, W), 0) // HEAD_DIM) == (lax.broadcasted_iota(jnp.int32, (W, W), 1) // HEAD_DIM)
    diag = ri == lj
    eye = diag.astype(F32)

    def expand(cols):
        res = cols[H - 1]
        for h in range(H - 2, -1, -1):
            res = jnp.where(blk == h, cols[h], res)
        return res

    def seg_sum(x):
        return expand([jnp.sum(jnp.where(blk == h, x, 0.0), axis=-1, keepdims=True) for h in range(H)])

    def l2n(x):
        return x * lax.rsqrt(seg_sum(x * x) + EPS)

    def row_form(x):
        res = x[(H - 1) * C:H * C]
        for h in range(H - 2, -1, -1):
            res = jnp.where(blk == h, x[h * C:(h + 1) * C], res)
        return res

    units = []
    for c in range(n_chunks):
        rows = slice(c * C, (c + 1) * C)
        q = l2n(act_ref[0, rows, 0:W]) * (HEAD_DIM ** -0.5)
        k = l2n(act_ref[0, rows, W:2 * W])
        v = act_ref[0, rows, 2 * W:3 * W]
        kb = k.astype(BF16)
        kkqk = _dot_nt(jnp.concatenate([kb, q.astype(BF16)], axis=0), _block_diag(kb, bd_mask))
        kk, qk = kkqk[0:C], kkqk[C:2 * C]
        bg = bg_ref[0, rows, :]
        gt = gt_ref[0, c]
        for d in range(2):
            beta = expand([bg[:, d * H + h:d * H + h + 1] for h in range(H)])
            g = expand([bg[:, 2 * H + d * H + h:2 * H + d * H + h + 1] for h in range(H)])
            g_row = gt[d:d + 1, :]
            incl = (ri >= lj) if d == 0 else (ri <= lj)
            incl_t = (ri <= lj) if d == 0 else (ri >= lj)
            strict = (ri > lj) if d == 0 else (ri < lj)
            gc = seg_sum(jnp.where(incl, g_row, 0.0))
            gc_row = jnp.sum(jnp.where(incl_t, g, 0.0), axis=0, keepdims=True)
            g_tot = jnp.sum(g, axis=0, keepdims=True)
            decay = jnp.where(incl, jnp.exp(jnp.where(incl, gc - gc_row, 0.0)), 0.0)
            t = jnp.where(strict, -(beta * kk * decay), 0.0)
            units.append(dict(c=c, d=d, q=q, k=k, v=v, beta=beta, gc=gc, g_tot=g_tot,
                              attn=(qk * decay).astype(BF16), t=t, p=eye + t))

    for un in units:
        un["t"] = _dot_split_bd(_split(un["t"]), _split(un["t"]), bd_mask)
    for _ in range(4):
        for un in units:
            both = _dot_split_bd(_split(jnp.concatenate([un["t"], un["p"]], axis=0)), _split(un["t"]), bd_mask)
            un["t"] = both[0:C]
            un["p"] = un["p"] + both[C:2 * C]
    for un in units:
        un["p"] = (un["p"] + _dot(un["p"].astype(BF16), _block_diag(un["t"].astype(BF16), bd_mask))).astype(BF16)
    for un in units:
        un["egc"] = jnp.exp(un["gc"])
        un["u"] = _dot(un["p"], _block_diag((un["v"] * un["beta"]).astype(BF16), bd_mask)).astype(BF16)
        un["w"] = _dot(un["p"], _block_diag((un["k"] * (un["beta"] * un["egc"])).astype(BF16), bd_mask)).astype(BF16)
    for un in units:
        c, d = un["c"], un["d"]
        kdec = (un["k"] * jnp.exp(un["g_tot"] - un["gc"])).astype(BF16)
        bm_ref[0, d, c] = row_form(_dot_tn(kdec, un["u"]))
        mp = row_form(_dot_tn(kdec, un["w"]))
        qt = un["q"] * un["egc"] - _dot(un["attn"], _block_diag(un["w"], bd_mask))
        ol_ref[0, d, c] = _dot(un["attn"], _block_diag(un["u"], bd_mask))
        pm_ref[0, d, c] = jnp.concatenate([mp, qt], axis=0).astype(BF16)
        e_ref[0, d, c] = jnp.broadcast_to(jnp.exp(un["g_tot"]), (HALO, W))


def _delta_chunks(act, bg, gt, tc):
    B, L, _ = act.shape
    N = L // CHUNK
    cb = tc // CHUNK
    W = D_DELTA
    ospec = lambda r: pl.BlockSpec((1, 2, cb, r, W), lambda b, i: (b, 0, i, 0, 0))
    return pl.pallas_call(
        _delta_chunk_kernel,
        grid=(B, L // tc),
        in_specs=[pl.BlockSpec((1, tc, 3 * W), lambda b, i: (b, i, 0)),
                  pl.BlockSpec((1, tc, LANES), lambda b, i: (b, i, 0)),
                  pl.BlockSpec((1, cb, 2, W), lambda b, i: (b, i, 0, 0))],
        out_specs=[ospec(2 * CHUNK), ospec(CHUNK), ospec(CHUNK), ospec(HALO)],
        out_shape=[jax.ShapeDtypeStruct((B, 2, N, 2 * CHUNK, W), BF16),
                   jax.ShapeDtypeStruct((B, 2, N, CHUNK, W), F32),
                   jax.ShapeDtypeStruct((B, 2, N, CHUNK, W), F32),
                   jax.ShapeDtypeStruct((B, 2, N, HALO, W), F32)],
        compiler_params=_cparams("parallel", "parallel"),
        name="delta_chunks",
    )(act, bg, gt)


def _delta_scan_kernel(pmf_ref, bmf_ref, olf_ref, ef_ref, pmb_ref, bmb_ref, olb_ref, eb_ref, s0_ref,
                       of_ref, ob_ref, sout_ref, s_ref):
    n = pl.program_id(1)
    bb = s_ref.shape[0]
    C, W = CHUNK, D_DELTA
    bd_mask = (lax.broadcasted_iota(jnp.int32, (W, W), 0) // HEAD_DIM) == (lax.broadcasted_iota(jnp.int32, (W, W), 1) // HEAD_DIM)

    @pl.when(n == 0)
    def _():
        s_ref[...] = s0_ref[...]

    dirs = ((pmf_ref, bmf_ref, olf_ref, ef_ref, of_ref), (pmb_ref, bmb_ref, olb_ref, eb_ref, ob_ref))
    cs = pmf_ref.shape[2]
    chains = [(b, d) for b in range(bb) for d in range(2)]
    s = [s_ref[b, d] for b, d in chains]
    for step in range(cs):
        idx = [step if d == 0 else cs - 1 - step for _, d in chains]
        r = [_dot(dirs[d][0][b, 0, idx[i]], _block_diag(s[i].astype(BF16), bd_mask)) for i, (b, d) in enumerate(chains)]
        for i, (b, d) in enumerate(chains):
            _, bm_ref, ol_ref, e_ref, o_ref = dirs[d]
            s[i] = e_ref[b, 0, idx[i]][0:1, :] * s[i] + bm_ref[b, 0, idx[i]] - r[i][0:C]
            o_ref[b, idx[i] * C:(idx[i] + 1) * C, :] = r[i][C:2 * C] + ol_ref[b, 0, idx[i]]
    for i, (b, d) in enumerate(chains):
        s_ref[b, d] = s[i]

    @pl.when(n == pl.num_programs(1) - 1)
    def _():
        sout_ref[...] = s_ref[...]


def _delta_scan(pm, bm, ol, e, s0, bb, cs):
    B, _, N, _, W = pm.shape
    L = N * CHUNK
    steps = N // cs

    def spec(r, d):
        if d == 0:
            return pl.BlockSpec((bb, 1, cs, r, W), lambda b, n: (b, 0, n, 0, 0))
        return pl.BlockSpec((bb, 1, cs, r, W), lambda b, n: (b, 1, steps - 1 - n, 0, 0))

    in_specs = []
    for d in range(2):
        in_specs += [spec(2 * CHUNK, d), spec(CHUNK, d), spec(CHUNK, d), spec(HALO, d)]
    in_specs.append(pl.BlockSpec((bb, 2, CHUNK, W), lambda b, n: (b, 0, 0, 0)))
    return pl.pallas_call(
        _delta_scan_kernel,
        grid=(B // bb, steps),
        in_specs=in_specs,
        out_specs=[pl.BlockSpec((bb, cs * CHUNK, W), lambda b, n: (b, n, 0)),
                   pl.BlockSpec((bb, cs * CHUNK, W), lambda b, n: (b, steps - 1 - n, 0)),
                   pl.BlockSpec((bb, 2, CHUNK, W), lambda b, n: (b, 0, 0, 0))],
        out_shape=[jax.ShapeDtypeStruct((B, L, W), F32),
                   jax.ShapeDtypeStruct((B, L, W), F32),
                   jax.ShapeDtypeStruct((B, 2, CHUNK, W), F32)],
        scratch_shapes=[pltpu.VMEM((bb, 2, CHUNK, W), F32)],
        compiler_params=_cparams("parallel", "arbitrary"),
        name="delta_scan",
    )(pm, bm, ol, e, pm, bm, ol, e, s0)


def _merge_kernel(x_ref, mod_ref, up_ref, upp_ref, upn_ref, cnt_ref, gates_ref, oattn_ref, of_ref, ob_ref,
                  pw_ref, pscale_ref, onorm_ref, wout_ref, gpost_ref, o_ref, pad_ref):
    i = pl.program_id(1)
    tm = x_ref.shape[1]
    first = i == 0
    last = i == pl.num_programs(1) - 1
    gates = gates_ref[0].astype(F32)
    y = _dot((gates[:, D_POOL:D_POOL + D_ATTN] * oattn_ref[0].astype(F32)).astype(BF16),
             wout_ref[D_POOL:D_POOL + D_ATTN, :])
    pad_ref[0:HALO, :] = jnp.where(first, 0.0, upp_ref[0])
    pad_ref[HALO:HALO + tm, :] = up_ref[0]
    pad_ref[HALO + tm:2 * HALO + tm, :] = jnp.where(last, 0.0, upn_ref[0])

    def window_sum(col, lo, hi):
        padded = pad_ref[:, col]
        rows = padded.shape[0]
        acc = None
        for j in range(lo, hi):
            term = (padded if j == 0 else pltpu.roll(padded, (-j) % rows, 0))[HALO:HALO + tm, :]
            acc = term if acc is None else acc + term
        return acc

    lane = lax.broadcasted_iota(jnp.int32, (tm, LANES), 1)
    lo_mask = lane < HEAD_DIM

    pooled = []
    for col_blk, (w_lo, w_hi) in enumerate(((2, 4), (8, 16))):
        col = slice(col_blk * LANES, (col_blk + 1) * LANES)
        s_lo = window_sum(col, -(w_lo // 2), w_lo // 2)
        s_hi = s_lo + window_sum(col, -(w_hi // 2), -(w_lo // 2)) + window_sum(col, w_lo // 2, w_hi // 2)
        mean = jnp.where(lo_mask, s_lo, s_hi) / cnt_ref[:, col]
        pooled.append(mean - up_ref[0, :, col])
    pooled = jnp.concatenate(pooled, axis=1)
    o_pool = _dot(pooled.astype(BF16), pw_ref[...]) * pscale_ref[...]

    od = of_ref[0] + ob_ref[0]
    odn = []
    for j in range(D_DELTA // LANES):
        blk = od[:, j * LANES:(j + 1) * LANES]
        odn.append(blk * _half_rms_scale(blk, lo_mask) * onorm_ref[...])
    o_delta = jnp.concatenate(odn, axis=1)

    y = y + _dot((gates[:, 0:D_POOL] * o_pool).astype(BF16), wout_ref[0:D_POOL, :])
    y = y + _dot((gates[:, D_POOL + D_ATTN:] * o_delta).astype(BF16), wout_ref[D_POOL + D_ATTN:, :])
    ms = jnp.mean(y * y, axis=-1, keepdims=True)
    yn = (y * lax.rsqrt(ms + EPS)) * gpost_ref[...]
    gate = mod_ref[0][:, 2 * D_MODEL:3 * D_MODEL]
    o_ref[0] = x_ref[0] + gate * yn


def _window_counts(seq_len):
    t = jnp.arange(seq_len)[:, None]
    w = jnp.repeat(jnp.array(POOL_WINDOWS), D_POOL // len(POOL_WINDOWS))[None, :]
    return (jnp.minimum(t - w // 2 + w, seq_len) - jnp.maximum(t - w // 2, 0)).astype(F32)


def _merge(x, mod, up, gates, oattn, o_f, o_b, pw_bd, pscale, onorm_row, wout, gpost, tm):
    B, L, _ = x.shape
    nb = tm // HALO
    last_blk = L // HALO - 1
    shared_mod = mod.shape[0] == 1
    row = lambda b, i: (b, i, 0)
    const2 = lambda b, i: (0, 0)
    return pl.pallas_call(
        _merge_kernel,
        grid=(B, L // tm),
        in_specs=[pl.BlockSpec((1, tm, D_MODEL), row),
                  pl.BlockSpec((1, 1, 3 * D_MODEL), (lambda b, i: (0, 0, 0)) if shared_mod else (lambda b, i: (b, 0, 0))),
                  pl.BlockSpec((1, tm, D_POOL), row),
                  pl.BlockSpec((1, HALO, D_POOL), lambda b, i: (b, jnp.maximum(i * nb - 1, 0), 0)),
                  pl.BlockSpec((1, HALO, D_POOL), lambda b, i: (b, jnp.minimum((i + 1) * nb, last_blk), 0)),
                  pl.BlockSpec((tm, D_POOL), lambda b, i: (i, 0)),
                  pl.BlockSpec((1, tm, D_MODEL), row),
                  pl.BlockSpec((1, tm, D_ATTN), row),
                  pl.BlockSpec((1, tm, D_DELTA), row),
                  pl.BlockSpec((1, tm, D_DELTA), row),
                  pl.BlockSpec((D_POOL, D_POOL), const2),
                  pl.BlockSpec((1, D_POOL), const2),
                  pl.BlockSpec((1, LANES), const2),
                  pl.BlockSpec((D_MODEL, D_MODEL), const2),
                  pl.BlockSpec((1, D_MODEL), const2)],
        out_specs=pl.BlockSpec((1, tm, D_MODEL), row),
        out_shape=jax.ShapeDtypeStruct((B, L, D_MODEL), F32),
        scratch_shapes=[pltpu.VMEM((tm + 2 * HALO, D_POOL), F32)],
        compiler_params=_cparams("parallel", "parallel"),
        name="merge",
    )(x, mod, up, up, up, _window_counts(L), gates, oattn, o_f, o_b, pw_bd, pscale, onorm_row, wout, gpost)


def _rope_tables(num_tokens):
    rows = num_tokens // GRID_W
    row = jnp.repeat(jnp.arange(rows, dtype=F32), GRID_W)
    col = (jnp.arange(rows * GRID_W) % GRID_W).astype(F32)
    axis_dim = HEAD_DIM // 2
    inv = ROPE_THETA ** (-jnp.arange(0, axis_dim, 2, dtype=F32) / axis_dim)
    ang = jnp.concatenate([row[:, None] * inv, col[:, None] * inv], axis=-1)
    cos = jnp.repeat(jnp.cos(ang), 2, axis=-1)
    sin = jnp.repeat(jnp.sin(ang), 2, axis=-1)
    sign = jnp.tile(jnp.array([-1.0, 1.0], F32), HEAD_DIM // 2)
    return jnp.tile(cos, (1, 2)), jnp.tile(sin * sign, (1, 2))


def _permute_w_in(w):
    u_pool, g_pool, q, k, v, g_attn, qkv_d, b_d, a_d, g_delta = jnp.split(
        w, [256, 512, 1024, 1152, 1280, 1792, 2560, 2568, 2576], axis=1)
    pad = jnp.zeros((w.shape[0], LANES - 4 * N_DELTA_HEADS), w.dtype)
    return jnp.concatenate([u_pool, g_pool, g_attn, g_delta, q, k, v, qkv_d, b_d, a_d, pad], axis=1).astype(BF16)


def _lane_row(vec, offset):
    return jnp.zeros((1, LANES), F32).at[0, offset:offset + vec.size].set(vec.reshape(-1))


def _cached_kv_variants(k, v):
    lane = jnp.arange(LANES)
    pairs = _kv_variants(k, jnp.roll(k, HEAD_DIM, axis=-1), v, jnp.roll(v, HEAD_DIM, axis=-1), lane)
    return (jnp.concatenate([p[0] for p in pairs], axis=-1).astype(BF16),
            jnp.concatenate([p[1] for p in pairs], axis=-1).astype(BF16))


def _layer(x, mod, lw, rope_tabs, ctx, tiles):
    B, L, _ = x.shape
    latent = ctx is not None
    outs = _inproj(x, mod, lw["gpre"], lw["w_in"], lw["qkn"], rope_tabs, tiles["tm"])
    up, gates, q, k4, v4, ksq, qkvd, ba = outs[:8]
    sources = [(k4, v4)]
    if latent:
        sources.append(ctx["kv"])
    oattn = _attention(q, ksq, sources, tiles["tq"], tiles["tk"])

    act, bg = _delta_prep(qkvd, ba, lw["conv_w"], lw["alog_row"], lw["dtb_row"], tiles["tt"])
    gt = bg[:, :, 2 * N_DELTA_HEADS:4 * N_DELTA_HEADS].reshape(B, L // CHUNK, CHUNK, 2, N_DELTA_HEADS)
    gt = gt.transpose(0, 1, 3, 4, 2).reshape(B, L // CHUNK, 2, D_DELTA)
    pm, bm, ol, e = _delta_chunks(act, bg, gt, tiles["tc"])
    s0 = ctx["state"] if latent else jnp.zeros((B, 2, HEAD_DIM, D_DELTA), F32)
    o_f, o_b, s_out = _delta_scan(pm, bm, ol, e, s0, tiles["bb"], tiles["cs"])

    y = _merge(x, mod, up, gates, oattn, o_f, o_b, lw["pw_bd"], lw["pscale"], lw["onorm_row"],
               lw["w_out"], lw["gpost"], tiles["tm"])
    if latent:
        return y
    kn, v = outs[8], outs[9]
    s_out = s_out.reshape(B, 2, HEAD_DIM, N_DELTA_HEADS, HEAD_DIM).transpose(0, 1, 3, 2, 4)
    return y, kn, v, s_out


def kernel(x_prompt, x_sample, cache_attn_k, cache_attn_v, state_delta, c, c_ctx, w_mod, b_mod, norm_pre, norm_post,
           w_in, w_out, pool_w, pool_scale, q_norm, k_norm, conv_w, a_log, dt_bias, o_norm):
    B, L, _ = x_prompt.shape
    DB, DL, _ = x_sample.shape
    past = cache_attn_k.shape[2]

    conds = jnp.zeros((8, D_MODEL), F32).at[0].set(c_ctx).at[1:1 + DB].set(c)
    mod = _modulation(conds, w_mod, b_mod)
    rope_tabs = _rope_tables(DL)

    ctx_tiles = dict(tm=256, tq=256, tk=256, tt=256, tc=256, bb=8, cs=4)
    lat_tiles = dict(tm=512, tq=256, tk=1024, tt=512, tc=256, bb=4, cs=4)

    hp, hs = x_prompt, x_sample
    new_k, new_v, new_s = [], [], []
    for l in range(DEPTH):
        blocks = jnp.zeros((N_DELTA_HEADS, HEAD_DIM, N_DELTA_HEADS, HEAD_DIM), F32)
        blocks = blocks.at[jnp.arange(4), :, jnp.arange(4), :].set(pool_w[l])
        qkn = jnp.zeros((8, LANES), F32).at[0].set(jnp.tile(q_norm[l], 2)).at[1].set(jnp.tile(k_norm[l], 2))
        lw = dict(
            gpre=norm_pre[l].reshape(1, D_MODEL),
            gpost=norm_post[l].reshape(1, D_MODEL),
            w_in=_permute_w_in(w_in[l]),
            w_out=w_out[l].astype(BF16),
            qkn=qkn,
            conv_w=conv_w[l],
            alog_row=_lane_row(a_log[l], 2 * N_DELTA_HEADS),
            dtb_row=_lane_row(dt_bias[l], 2 * N_DELTA_HEADS),
            pw_bd=blocks.reshape(D_POOL, D_POOL).astype(BF16),
            pscale=pool_scale[l].reshape(1, D_POOL),
            onorm_row=jnp.tile(o_norm[l], 2).reshape(1, LANES),
        )
        hp, k_l, v_l, s_l = _layer(hp, mod[l, 0:1].reshape(1, 1, 3 * D_MODEL), lw, None, None, ctx_tiles)
        ctx = dict(kv=_cached_kv_variants(cache_attn_k[:, l].reshape(DB, past, D_KV),
                                          cache_attn_v[:, l].reshape(DB, past, D_KV)),
                   state=state_delta[:, l].transpose(0, 1, 3, 2, 4).reshape(DB, 2, HEAD_DIM, D_DELTA))
        hs = _layer(hs, mod[l, 1:1 + DB].reshape(DB, 1, 3 * D_MODEL), lw, rope_tabs, ctx, lat_tiles)
        new_k.append(k_l.reshape(B, L, 2, HEAD_DIM))
        new_v.append(v_l.reshape(B, L, 2, HEAD_DIM))
        new_s.append(s_l)
    return (hp, hs, jnp.stack(new_k, axis=1), jnp.stack(new_v, axis=1), jnp.stack(new_s, axis=1))
```

```python
import functools

import jax
import jax.numpy as jnp
from jax import lax
from jax.experimental import pallas as pl
from jax.experimental.pallas import tpu as pltpu

F32 = jnp.float32
BF16 = jnp.bfloat16

D_MODEL = 1024
DEPTH = 2
GRID_W = 64
HEAD_DIM = 64
D_POOL = 256
D_ATTN = 512
D_DELTA = 256
D_KV = 128
N_Q_HEADS = 8
N_DELTA_HEADS = 4
POOL_WINDOWS = (2, 4, 8, 16)
CHUNK = 64
ROPE_THETA = 10000.0
EPS = 1e-6
LANES = 128
HALO = 8
MASK_VALUE = -1e30
LOG2E = 1.4426950408889634
Q_SCALE = HEAD_DIM ** -0.5 * LOG2E
SHIFT_LIMIT = 60.0

OFF_UP = 0
OFF_GATES = 256
OFF_QK = 1280
OFF_V = 1920
OFF_QKVD = 2048
OFF_BA = 2816
D_IN_PAD = 2944

VMEM_LIMIT = 56 * 1024 * 1024


def _cparams(*sem):
    return pltpu.CompilerParams(dimension_semantics=sem, vmem_limit_bytes=VMEM_LIMIT)


def _dot(a, b):
    return jnp.dot(a, b, preferred_element_type=F32)


def _dot_nt(a, b):
    return lax.dot_general(a, b, (((1,), (1,)), ((), ())), preferred_element_type=F32)


def _dot_tn(a, b):
    return lax.dot_general(a, b, (((0,), (0,)), ((), ())), preferred_element_type=F32)


def _split(x):
    hi = x.astype(BF16)
    return hi, (x - hi.astype(F32)).astype(BF16)


def _silu(x):
    return x * jax.nn.sigmoid(x)


def _softplus(x):
    return jnp.maximum(x, 0.0) + jnp.log1p(jnp.exp(-jnp.abs(x)))


def _kv_variants(k, k_sw, v, v_sw, lane):
    lo = lane < HEAD_DIM
    one_hi = (lane == HEAD_DIM).astype(k.dtype)
    one_lo = (lane == 0).astype(k.dtype)
    ones = jnp.ones_like(v)
    return ((jnp.where(lo, k, one_hi), jnp.where(lo, v, ones)),
            (jnp.where(lo, one_lo, k_sw), jnp.where(lo, ones, v_sw)),
            (jnp.where(lo, k_sw, one_hi), jnp.where(lo, v_sw, ones)),
            (jnp.where(lo, one_lo, k), jnp.where(lo, ones, v)))


def _half_rms_scale(x, lo_mask):
    sq = x * x
    ss_lo = jnp.sum(jnp.where(lo_mask, sq, 0.0), axis=-1, keepdims=True)
    ss_hi = jnp.sum(jnp.where(lo_mask, 0.0, sq), axis=-1, keepdims=True)
    r_lo = lax.rsqrt(ss_lo * (1.0 / HEAD_DIM) + EPS)
    r_hi = lax.rsqrt(ss_hi * (1.0 / HEAD_DIM) + EPS)
    return jnp.where(lo_mask, r_lo, r_hi)


def _mod_kernel(c_ref, w_ref, b_ref, o_ref):
    s = _silu(c_ref[...])
    o_ref[0] = _dot(s.astype(BF16), w_ref[0].astype(BF16)) + b_ref[0]


def _modulation(conds, w_mod, b_mod):
    tn = 768
    return pl.pallas_call(
        _mod_kernel,
        grid=(DEPTH, 3 * D_MODEL // tn),
        in_specs=[pl.BlockSpec((8, D_MODEL), lambda l, j: (0, 0)),
                  pl.BlockSpec((1, D_MODEL, tn), lambda l, j: (l, 0, j)),
                  pl.BlockSpec((1, 1, tn), lambda l, j: (l, 0, j))],
        out_specs=pl.BlockSpec((1, 8, tn), lambda l, j: (l, 0, j)),
        out_shape=jax.ShapeDtypeStruct((DEPTH, 8, 3 * D_MODEL), F32),
        compiler_params=_cparams("parallel", "parallel"),
        name="modulation",
    )(conds, w_mod, b_mod.reshape(DEPTH, 1, 3 * D_MODEL))


def _inproj_kernel(rope, x_ref, xprev_ref, xnext_ref, mod_ref, gpre_ref, w_ref, qkn_ref, convw_ref, alog_ref, dtb_ref,
                   *rest):
    if rope:
        cos_ref, sin_ref, up_ref, gates_ref, q_ref, k4_ref, v4_ref, ksq_ref, act_ref, bg_ref, pad_ref = rest
    else:
        up_ref, gates_ref, q_ref, k4_ref, v4_ref, ksq_ref, act_ref, bg_ref, kn_ref, v_ref, pad_ref = rest
    i = pl.program_id(1)
    tm = x_ref.shape[1]
    mod = mod_ref[0]
    shift = mod[:, 0:D_MODEL]
    scale = mod[:, D_MODEL:2 * D_MODEL]

    def modulated_norm(x):
        ms = jnp.mean(x * x, axis=-1, keepdims=True)
        return ((x * lax.rsqrt(ms + EPS)) * gpre_ref[...] * (1.0 + scale) + shift).astype(BF16)

    hb = modulated_norm(x_ref[0])

    zqkv = _dot(hb, w_ref[:, OFF_QK:OFF_QK + D_ATTN + 2 * D_KV])
    zqk, v = zqkv[:, 0:D_ATTN + D_KV], zqkv[:, D_ATTN + D_KV:]
    wd = w_ref[:, OFF_QKVD:OFF_QKVD + 3 * D_DELTA]
    pad_ref[0:HALO, :] = jnp.where(i == 0, 0.0, _dot(modulated_norm(xprev_ref[0]), wd))
    pad_ref[HALO:HALO + tm, :] = _dot(hb, wd)
    pad_ref[HALO + tm:2 * HALO + tm, :] = jnp.where(i == pl.num_programs(1) - 1, 0.0, _dot(modulated_norm(xnext_ref[0]), wd))
    zgates = _dot(hb, w_ref[:, OFF_GATES:OFF_GATES + D_MODEL])

    lane = lax.broadcasted_iota(jnp.int32, (tm, LANES), 1)
    lo_mask = lane < HEAD_DIM
    even = (lane % 2) == 0

    def head_norm_rope(blk, gain):
        y = blk * _half_rms_scale(blk, lo_mask) * gain
        if rope:
            swapped = jnp.where(even, pltpu.roll(y, LANES - 1, 1), pltpu.roll(y, 1, 1))
            return y, y * cos_ref[...] + swapped * sin_ref[...]
        return y, y

    for qb in range(D_ATTN // LANES):
        _, qr = head_norm_rope(zqk[:, qb * LANES:(qb + 1) * LANES], qkn_ref[0:1, :])
        q_ref[0, :, qb * LANES:(qb + 1) * LANES] = (qr * Q_SCALE).astype(BF16)
    gates_ref[0] = _silu(zgates).astype(BF16)
    up_ref[0] = _dot(hb, w_ref[:, OFF_UP:OFF_UP + D_POOL])
    ba = _dot(hb, w_ref[:, OFF_BA:OFF_BA + LANES])

    xp = pad_ref[...]
    rows = xp.shape[0]
    acc = None
    for j in range(4):
        shifted = xp if j == 2 else pltpu.roll(xp, (2 - j) % rows, 0)
        term = convw_ref[j:j + 1, :] * shifted[HALO:HALO + tm, :]
        acc = term if acc is None else acc + term
    act_ref[0] = _silu(acc)
    g = -jnp.exp(alog_ref[...]) * _softplus(ba + dtb_ref[...])
    bg_ref[0] = jnp.where(lane < 2 * N_DELTA_HEADS, jax.nn.sigmoid(ba), jnp.where(lane < 4 * N_DELTA_HEADS, g, 0.0))

    kn, kr = head_norm_rope(zqk[:, D_ATTN:D_ATTN + D_KV], qkn_ref[1:2, :])
    for a, (kx, vx) in enumerate(_kv_variants(kr, pltpu.roll(kr, HEAD_DIM, 1), v, pltpu.roll(v, HEAD_DIM, 1), lane)):
        k4_ref[0, :, a * LANES:(a + 1) * LANES] = kx.astype(BF16)
        v4_ref[0, :, a * LANES:(a + 1) * LANES] = vx.astype(BF16)
    ksq = kr * kr
    ksq = jnp.where(lo_mask, jnp.sum(jnp.where(lo_mask, ksq, 0.0), axis=-1, keepdims=True),
                    jnp.sum(jnp.where(lo_mask, 0.0, ksq), axis=-1, keepdims=True))
    ksq_ref[0, 0] = jnp.broadcast_to(jnp.max(ksq, axis=0, keepdims=True), (HALO, LANES))
    if not rope:
        kn_ref[0] = kn
        v_ref[0] = v


def _inproj(x, mod, gpre, w, qkn, conv_w, alog_row, dtb_row, rope_tabs, tm):
    B, L, _ = x.shape
    rope = rope_tabs is not None
    shared_mod = mod.shape[0] == 1
    nb = tm // HALO
    last_blk = L // HALO - 1
    row = lambda b, i: (b, i, 0)
    const2 = lambda b, i: (0, 0)
    in_specs = [pl.BlockSpec((1, tm, D_MODEL), row),
                pl.BlockSpec((1, HALO, D_MODEL), lambda b, i: (b, jnp.maximum(i * nb - 1, 0), 0)),
                pl.BlockSpec((1, HALO, D_MODEL), lambda b, i: (b, jnp.minimum((i + 1) * nb, last_blk), 0)),
                pl.BlockSpec((1, 1, 3 * D_MODEL), (lambda b, i: (0, 0, 0)) if shared_mod else (lambda b, i: (b, 0, 0))),
                pl.BlockSpec((1, D_MODEL), const2),
                pl.BlockSpec((D_MODEL, D_IN_PAD), const2),
                pl.BlockSpec((8, LANES), const2),
                pl.BlockSpec((4, 3 * D_DELTA), const2),
                pl.BlockSpec((1, LANES), const2),
                pl.BlockSpec((1, LANES), const2)]
    args = [x, x, x, mod, gpre, w, qkn, conv_w, alog_row, dtb_row]
    if rope:
        in_specs += [pl.BlockSpec((tm, LANES), lambda b, i: (i, 0))] * 2
        args += list(rope_tabs)
    widths = [(D_POOL, F32), (D_MODEL, BF16), (D_ATTN, BF16), (4 * LANES, BF16), (4 * LANES, BF16), None,
              (3 * D_DELTA, F32), (LANES, F32)]
    if not rope:
        widths += [(D_KV, F32), (D_KV, F32)]
    out_specs = [pl.BlockSpec((1, tm, wd[0]), row) if wd else pl.BlockSpec((1, 1, HALO, LANES), lambda b, i: (b, i, 0, 0))
                 for wd in widths]
    out_shape = [jax.ShapeDtypeStruct((B, L, wd[0]), wd[1]) if wd else jax.ShapeDtypeStruct((B, L // tm, HALO, LANES), F32)
                 for wd in widths]
    return pl.pallas_call(
        functools.partial(_inproj_kernel, rope),
        grid=(B, L // tm),
        in_specs=in_specs,
        out_specs=out_specs,
        out_shape=out_shape,
        scratch_shapes=[pltpu.VMEM((tm + 2 * HALO, 3 * D_DELTA), F32)],
        compiler_params=_cparams("parallel", "parallel"),
        name="inproj_rope" if rope else "inproj",
    )(*args)


_VARIANT_HEADS = ((0, 2), (1, 3), (4, 6), (5, 7))


def _attn_kernel(n_src, tk, q_ref, ksq_ref, *rest):
    kv_refs = rest[:2 * n_src]
    o_ref, acc_ref, m_ref = rest[2 * n_src:]
    tq = q_ref.shape[1]
    n_var = len(_VARIANT_HEADS)
    lane = lax.broadcasted_iota(jnp.int32, (tq, LANES), 1)
    lo_mask = lane < HEAD_DIM

    ksq = jnp.max(ksq_ref[0], axis=0)[0:1, :]
    lo_row = lo_mask[0:1, :]
    ksq_head = [jnp.max(jnp.where(lo_row, ksq, 0.0), axis=-1, keepdims=True),
                jnp.max(jnp.where(lo_row, 0.0, ksq), axis=-1, keepdims=True)]
    for j in range(1, n_src):
        for kv_head in range(2):
            kx = kv_refs[2 * j][0, :, 2 * kv_head * LANES:(2 * kv_head + 1) * LANES].astype(F32)
            sq = jnp.sum(jnp.where(lane[0:1, :] < HEAD_DIM, kx * kx, 0.0), axis=-1, keepdims=True)
            ksq_head[kv_head] = jnp.maximum(ksq_head[kv_head], jnp.max(sq, axis=0, keepdims=True))

    def own_half(h):
        blk = q_ref[0, :, (h // 2) * LANES:(h // 2 + 1) * LANES]
        return jnp.where(lo_mask if h % 2 == 0 else jnp.logical_not(lo_mask), blk, jnp.zeros_like(blk))

    q_shift, bound_max = [], None
    for heads in _VARIANT_HEADS:
        shifted = []
        for h in heads:
            qf = own_half(h).astype(F32)
            bound = jnp.sqrt(jnp.sum(qf * qf, axis=-1, keepdims=True) * ksq_head[h // 4])
            bias_lane = HEAD_DIM if h % 2 == 0 else 0
            shifted.append(jnp.where(lane == bias_lane, -bound, qf).astype(BF16))
            bound_max = bound if bound_max is None else jnp.maximum(bound_max, bound)
        q_shift.append(jnp.concatenate(shifted, axis=0))
    safe = jnp.max(bound_max) <= SHIFT_LIMIT

    cols = [slice(a * LANES, (a + 1) * LANES) for a in range(n_var)]
    chunks = []
    for j in range(n_src):
        k4_ref, v4_ref = kv_refs[2 * j], kv_refs[2 * j + 1]
        S = k4_ref.shape[1]
        size = min(tk, S)
        chunks += [(k4_ref, v4_ref, c * size, size) for c in range(S // size)]

    @pl.when(safe)
    def _():
        def scores(ch):
            k4_ref, _, start, size = ch
            return [_dot_nt(q_shift[a], k4_ref[0, start:start + size, cols[a]]) for a in range(n_var)]

        acc = [None] * n_var
        s_next = scores(chunks[0])
        for ci, (_, v4_ref, start, size) in enumerate(chunks):
            s = s_next
            if ci + 1 < len(chunks):
                s_next = scores(chunks[ci + 1])
            for a in range(n_var):
                pv = _dot(jnp.exp2(s[a]).astype(BF16), v4_ref[0, start:start + size, cols[a]])
                acc[a] = pv if acc[a] is None else acc[a] + pv
        for a in range(n_var):
            acc_ref[a] = acc[a]

    @pl.when(jnp.logical_not(safe))
    def _():
        m_ref[...] = jnp.full(m_ref.shape, MASK_VALUE, F32)
        acc_ref[...] = jnp.zeros(acc_ref.shape, F32)
        q_plain = [jnp.concatenate([own_half(h) for h in heads], axis=0) for heads in _VARIANT_HEADS]
        for k4_ref, v4_ref, start, size in chunks:
            for a in range(n_var):
                s = _dot_nt(q_plain[a], k4_ref[0, start:start + size, cols[a]])
                m_prev = m_ref[a]
                m_new = jnp.maximum(m_prev, jnp.max(s, axis=-1, keepdims=True))
                p = jnp.exp2(s - jnp.tile(m_new, (1, size // LANES)))
                m_ref[a] = m_new
                acc_ref[a] = jnp.exp2(m_prev - m_new) * acc_ref[a] + _dot(p.astype(BF16), v4_ref[0, start:start + size, cols[a]])

    def head_out(h):
        a = 2 * (h // 4) + h % 2
        r = _VARIANT_HEADS[a].index(h) * tq
        rows = acc_ref[a, r:r + tq, :]
        return rows / pltpu.roll(rows, HEAD_DIM, 1)

    for i in range(N_Q_HEADS // 2):
        o_ref[0, :, i * LANES:(i + 1) * LANES] = jnp.where(lo_mask, head_out(2 * i), head_out(2 * i + 1)).astype(BF16)


def _attention(q, ksq, sources, tq, tk):
    B, L, _ = q.shape
    n_var = len(_VARIANT_HEADS)
    in_specs = [pl.BlockSpec((1, tq, D_ATTN), lambda b, i: (b, i, 0)),
                pl.BlockSpec((1,) + ksq.shape[1:], lambda b, i: (b, 0, 0, 0))]
    args = [q, ksq]
    for k4, v4 in sources:
        S = k4.shape[1]
        in_specs += [pl.BlockSpec((1, S, n_var * LANES), lambda b, i: (b, 0, 0))] * 2
        args += [k4, v4]
    rows = N_Q_HEADS // n_var * tq
    return pl.pallas_call(
        functools.partial(_attn_kernel, len(sources), tk),
        grid=(B, L // tq),
        in_specs=in_specs,
        out_specs=pl.BlockSpec((1, tq, D_ATTN), lambda b, i: (b, i, 0)),
        out_shape=jax.ShapeDtypeStruct((B, L, D_ATTN), BF16),
        scratch_shapes=[pltpu.VMEM((n_var, rows, LANES), F32)] * 2,
        compiler_params=_cparams("parallel", "parallel"),
        name="attention",
    )(*args)


def _block_diag(x, bd_mask):
    return jnp.where(bd_mask, jnp.concatenate([x] * N_DELTA_HEADS, axis=0), jnp.zeros((), x.dtype))


def _dot_split_bd(a, b, bd_mask):
    (ah, al), (bh, bl) = a, b
    m = ah.shape[0]
    both = _dot(jnp.concatenate([ah, al], axis=0), _block_diag(bh, bd_mask))
    return both[0:m] + both[m:2 * m] + _dot(ah, _block_diag(bl, bd_mask))


def _delta_chunk_kernel(act_ref, bg_ref, gt_ref, pm_ref, bm_ref, ol_ref, e_ref):
    C, W, H = CHUNK, D_DELTA, N_DELTA_HEADS
    n_chunks = act_ref.shape[1] // C
    ri = lax.broadcasted_iota(jnp.int32, (C, W), 0)
    lane = lax.broadcasted_iota(jnp.int32, (C, W), 1)
    lj = lane % HEAD_DIM
    blk = lane // HEAD_DIM
    bd_mask = (lax.broadcasted_iota(jnp.int32, (W, W), 0) // HEAD_DIM) == (lax.broadcasted_iota(jnp.int32, (W, W), 1) // HEAD_DIM)
    diag = ri == lj
    eye = diag.astype(F32)

    def expand(cols):
        res = cols[H - 1]
        for h in range(H - 2, -1, -1):
            res = jnp.where(blk == h, cols[h], res)
        return res

    def seg_sum(x):
        return expand([jnp.sum(jnp.where(blk == h, x, 0.0), axis=-1, keepdims=True) for h in range(H)])

    def l2n(x):
        return x * lax.rsqrt(seg_sum(x * x) + EPS)

    def row_form(x):
        res = x[(H - 1) * C:H * C]
        for h in range(H - 2, -1, -1):
            res = jnp.where(blk == h, x[h * C:(h + 1) * C], res)
        return res

    units = []
    for c in range(n_chunks):
        rows = slice(c * C, (c + 1) * C)
        q = l2n(act_ref[0, rows, 0:W]) * (HEAD_DIM ** -0.5)
        k = l2n(act_ref[0, rows, W:2 * W])
        v = act_ref[0, rows, 2 * W:3 * W]
        kb = k.astype(BF16)
        kkqk = _dot_nt(jnp.concatenate([kb, q.astype(BF16)], axis=0), _block_diag(kb, bd_mask))
        kk, qk = kkqk[0:C], kkqk[C:2 * C]
        bg = bg_ref[0, rows, :]
        gt = gt_ref[0, c]
        for d in range(2):
            beta = expand([bg[:, d * H + h:d * H + h + 1] for h in range(H)])
            g = expand([bg[:, 2 * H + d * H + h:2 * H + d * H + h + 1] for h in range(H)])
            g_row = gt[d:d + 1, :]
            incl = (ri >= lj) if d == 0 else (ri <= lj)
            incl_t = (ri <= lj) if d == 0 else (ri >= lj)
            strict = (ri > lj) if d == 0 else (ri < lj)
            gc = seg_sum(jnp.where(incl, g_row, 0.0))
            gc_row = jnp.sum(jnp.where(incl_t, g, 0.0), axis=0, keepdims=True)
            g_tot = jnp.sum(g, axis=0, keepdims=True)
            decay = jnp.where(incl, jnp.exp(jnp.where(incl, gc - gc_row, 0.0)), 0.0)
            t = jnp.where(strict, -(beta * kk * decay), 0.0)
            units.append(dict(c=c, d=d, q=q, k=k, v=v, beta=beta, gc=gc, g_tot=g_tot,
                              attn=(qk * decay).astype(BF16), t=t, p=eye + t))

    for un in units:
        un["t"] = _dot_split_bd(_split(un["t"]), _split(un["t"]), bd_mask)
    for level in range(4):
        for un in units:
            tp = jnp.concatenate([un["t"], un["p"]], axis=0)
            if level < 3:
                both = _dot_split_bd(_split(tp), _split(un["t"]), bd_mask)
            else:
                both = _dot(tp.astype(BF16), _block_diag(un["t"].astype(BF16), bd_mask))
            un["t"] = both[0:C]
            un["p"] = un["p"] + both[C:2 * C]
    for un in units:
        un["p"] = (un["p"] + _dot(un["p"].astype(BF16), _block_diag(un["t"].astype(BF16), bd_mask))).astype(BF16)
    for un in units:
        un["egc"] = jnp.exp(un["gc"])
        un["u"] = _dot(un["p"], _block_diag((un["v"] * un["beta"]).astype(BF16), bd_mask)).astype(BF16)
        un["w"] = _dot(un["p"], _block_diag((un["k"] * (un["beta"] * un["egc"])).astype(BF16), bd_mask)).astype(BF16)
    for un in units:
        c, d = un["c"], un["d"]
        kdec = (un["k"] * jnp.exp(un["g_tot"] - un["gc"])).astype(BF16)
        bm_ref[0, d, c] = row_form(_dot_tn(kdec, un["u"]))
        mp = row_form(_dot_tn(kdec, un["w"]))
        qt = un["q"] * un["egc"] - _dot(un["attn"], _block_diag(un["w"], bd_mask))
        ol_ref[0, d, c] = _dot(un["attn"], _block_diag(un["u"], bd_mask))
        pm_ref[0, d, c] = jnp.concatenate([mp, qt], axis=0).astype(BF16)
        e_ref[0, d, c] = jnp.broadcast_to(jnp.exp(un["g_tot"]), (HALO, W))


def _delta_chunks(act, bg, gt, tc):
    B, L, _ = act.shape
    N = L // CHUNK
    cb = tc // CHUNK
    W = D_DELTA
    ospec = lambda r: pl.BlockSpec((1, 2, cb, r, W), lambda b, i: (b, 0, i, 0, 0))
    return pl.pallas_call(
        _delta_chunk_kernel,
        grid=(B, L // tc),
        in_specs=[pl.BlockSpec((1, tc, 3 * W), lambda b, i: (b, i, 0)),
                  pl.BlockSpec((1, tc, LANES), lambda b, i: (b, i, 0)),
                  pl.BlockSpec((1, cb, 2, W), lambda b, i: (b, i, 0, 0))],
        out_specs=[ospec(2 * CHUNK), ospec(CHUNK), ospec(CHUNK), ospec(HALO)],
        out_shape=[jax.ShapeDtypeStruct((B, 2, N, 2 * CHUNK, W), BF16),
                   jax.ShapeDtypeStruct((B, 2, N, CHUNK, W), F32),
                   jax.ShapeDtypeStruct((B, 2, N, CHUNK, W), F32),
                   jax.ShapeDtypeStruct((B, 2, N, HALO, W), F32)],
        compiler_params=_cparams("parallel", "parallel"),
        name="delta_chunks",
    )(act, bg, gt)


def _delta_scan_kernel(pmf_ref, bmf_ref, olf_ref, ef_ref, pmb_ref, bmb_ref, olb_ref, eb_ref, s0_ref,
                       of_ref, ob_ref, sout_ref, s_ref):
    n = pl.program_id(1)
    bb = s_ref.shape[0]
    C, W = CHUNK, D_DELTA
    bd_mask = (lax.broadcasted_iota(jnp.int32, (W, W), 0) // HEAD_DIM) == (lax.broadcasted_iota(jnp.int32, (W, W), 1) // HEAD_DIM)

    @pl.when(n == 0)
    def _():
        s_ref[...] = s0_ref[...]

    dirs = ((pmf_ref, bmf_ref, olf_ref, ef_ref, of_ref), (pmb_ref, bmb_ref, olb_ref, eb_ref, ob_ref))
    cs = pmf_ref.shape[2]
    chains = [(b, d) for b in range(bb) for d in range(2)]
    s = [s_ref[b, d] for b, d in chains]
    for step in range(cs):
        idx = [step if d == 0 else cs - 1 - step for _, d in chains]
        r = [_dot(dirs[d][0][b, 0, idx[i]], _block_diag(s[i].astype(BF16), bd_mask)) for i, (b, d) in enumerate(chains)]
        for i, (b, d) in enumerate(chains):
            _, bm_ref, ol_ref, e_ref, o_ref = dirs[d]
            s[i] = e_ref[b, 0, idx[i]][0:1, :] * s[i] + bm_ref[b, 0, idx[i]] - r[i][0:C]
            o_ref[b, idx[i] * C:(idx[i] + 1) * C, :] = r[i][C:2 * C] + ol_ref[b, 0, idx[i]]
    for i, (b, d) in enumerate(chains):
        s_ref[b, d] = s[i]

    @pl.when(n == pl.num_programs(1) - 1)
    def _():
        sout_ref[...] = s_ref[...]


def _delta_scan(pm, bm, ol, e, s0, bb, cs):
    B, _, N, _, W = pm.shape
    L = N * CHUNK
    steps = N // cs

    def spec(r, d):
        if d == 0:
            return pl.BlockSpec((bb, 1, cs, r, W), lambda b, n: (b, 0, n, 0, 0))
        return pl.BlockSpec((bb, 1, cs, r, W), lambda b, n: (b, 1, steps - 1 - n, 0, 0))

    in_specs = []
    for d in range(2):
        in_specs += [spec(2 * CHUNK, d), spec(CHUNK, d), spec(CHUNK, d), spec(HALO, d)]
    in_specs.append(pl.BlockSpec((bb, 2, CHUNK, W), lambda b, n: (b, 0, 0, 0)))
    return pl.pallas_call(
        _delta_scan_kernel,
        grid=(B // bb, steps),
        in_specs=in_specs,
        out_specs=[pl.BlockSpec((bb, cs * CHUNK, W), lambda b, n: (b, n, 0)),
                   pl.BlockSpec((bb, cs * CHUNK, W), lambda b, n: (b, steps - 1 - n, 0)),
                   pl.BlockSpec((bb, 2, CHUNK, W), lambda b, n: (b, 0, 0, 0))],
        out_shape=[jax.ShapeDtypeStruct((B, L, W), F32),
                   jax.ShapeDtypeStruct((B, L, W), F32),
                   jax.ShapeDtypeStruct((B, 2, CHUNK, W), F32)],
        scratch_shapes=[pltpu.VMEM((bb, 2, CHUNK, W), F32)],
        compiler_params=_cparams("parallel", "arbitrary"),
        name="delta_scan",
    )(pm, bm, ol, e, pm, bm, ol, e, s0)


def _merge_kernel(x_ref, mod_ref, up_ref, upp_ref, upn_ref, cnt_ref, gates_ref, oattn_ref, of_ref, ob_ref,
                  pw_ref, pscale_ref, onorm_ref, wout_ref, gpost_ref, o_ref, pad_ref):
    i = pl.program_id(1)
    tm = x_ref.shape[1]
    first = i == 0
    last = i == pl.num_programs(1) - 1
    gates = gates_ref[0].astype(F32)
    y = _dot((gates[:, D_POOL:D_POOL + D_ATTN] * oattn_ref[0].astype(F32)).astype(BF16),
             wout_ref[D_POOL:D_POOL + D_ATTN, :])
    pad_ref[0:HALO, :] = jnp.where(first, 0.0, upp_ref[0])
    pad_ref[HALO:HALO + tm, :] = up_ref[0]
    pad_ref[HALO + tm:2 * HALO + tm, :] = jnp.where(last, 0.0, upn_ref[0])

    def window_sum(col, lo, hi):
        padded = pad_ref[:, col]
        rows = padded.shape[0]
        acc = None
        for j in range(lo, hi):
            term = (padded if j == 0 else pltpu.roll(padded, (-j) % rows, 0))[HALO:HALO + tm, :]
            acc = term if acc is None else acc + term
        return acc

    lane = lax.broadcasted_iota(jnp.int32, (tm, LANES), 1)
    lo_mask = lane < HEAD_DIM

    pooled = []
    for col_blk, (w_lo, w_hi) in enumerate(((2, 4), (8, 16))):
        col = slice(col_blk * LANES, (col_blk + 1) * LANES)
        s_lo = window_sum(col, -(w_lo // 2), w_lo // 2)
        s_hi = s_lo + window_sum(col, -(w_hi // 2), -(w_lo // 2)) + window_sum(col, w_lo // 2, w_hi // 2)
        mean = jnp.where(lo_mask, s_lo, s_hi) / cnt_ref[:, col]
        pooled.append(mean - up_ref[0, :, col])
    pooled = jnp.concatenate(pooled, axis=1)
    o_pool = _dot(pooled.astype(BF16), pw_ref[...]) * pscale_ref[...]

    od = of_ref[0] + ob_ref[0]
    odn = []
    for j in range(D_DELTA // LANES):
        blk = od[:, j * LANES:(j + 1) * LANES]
        odn.append(blk * _half_rms_scale(blk, lo_mask) * onorm_ref[...])
    o_delta = jnp.concatenate(odn, axis=1)

    y = y + _dot((gates[:, 0:D_POOL] * o_pool).astype(BF16), wout_ref[0:D_POOL, :])
    y = y + _dot((gates[:, D_POOL + D_ATTN:] * o_delta).astype(BF16), wout_ref[D_POOL + D_ATTN:, :])
    ms = jnp.mean(y * y, axis=-1, keepdims=True)
    yn = (y * lax.rsqrt(ms + EPS)) * gpost_ref[...]
    gate = mod_ref[0][:, 2 * D_MODEL:3 * D_MODEL]
    o_ref[0] = x_ref[0] + gate * yn


def _window_counts(seq_len):
    t = jnp.arange(seq_len)[:, None]
    w = jnp.repeat(jnp.array(POOL_WINDOWS), D_POOL // len(POOL_WINDOWS))[None, :]
    return (jnp.minimum(t - w // 2 + w, seq_len) - jnp.maximum(t - w // 2, 0)).astype(F32)


def _merge(x, mod, up, gates, oattn, o_f, o_b, pw_bd, pscale, onorm_row, wout, gpost, tm):
    B, L, _ = x.shape
    nb = tm // HALO
    last_blk = L // HALO - 1
    shared_mod = mod.shape[0] == 1
    row = lambda b, i: (b, i, 0)
    const2 = lambda b, i: (0, 0)
    return pl.pallas_call(
        _merge_kernel,
        grid=(B, L // tm),
        in_specs=[pl.BlockSpec((1, tm, D_MODEL), row),
                  pl.BlockSpec((1, 1, 3 * D_MODEL), (lambda b, i: (0, 0, 0)) if shared_mod else (lambda b, i: (b, 0, 0))),
                  pl.BlockSpec((1, tm, D_POOL), row),
                  pl.BlockSpec((1, HALO, D_POOL), lambda b, i: (b, jnp.maximum(i * nb - 1, 0), 0)),
                  pl.BlockSpec((1, HALO, D_POOL), lambda b, i: (b, jnp.minimum((i + 1) * nb, last_blk), 0)),
                  pl.BlockSpec((tm, D_POOL), lambda b, i: (i, 0)),
                  pl.BlockSpec((1, tm, D_MODEL), row),
                  pl.BlockSpec((1, tm, D_ATTN), row),
                  pl.BlockSpec((1, tm, D_DELTA), row),
                  pl.BlockSpec((1, tm, D_DELTA), row),
                  pl.BlockSpec((D_POOL, D_POOL), const2),
                  pl.BlockSpec((1, D_POOL), const2),
                  pl.BlockSpec((1, LANES), const2),
                  pl.BlockSpec((D_MODEL, D_MODEL), const2),
                  pl.BlockSpec((1, D_MODEL), const2)],
        out_specs=pl.BlockSpec((1, tm, D_MODEL), row),
        out_shape=jax.ShapeDtypeStruct((B, L, D_MODEL), F32),
        scratch_shapes=[pltpu.VMEM((tm + 2 * HALO, D_POOL), F32)],
        compiler_params=_cparams("parallel", "parallel"),
        name="merge",
    )(x, mod, up, up, up, _window_counts(L), gates, oattn, o_f, o_b, pw_bd, pscale, onorm_row, wout, gpost)


def _rope_tables(num_tokens):
    rows = num_tokens // GRID_W
    row = jnp.repeat(jnp.arange(rows, dtype=F32), GRID_W)
    col = (jnp.arange(rows * GRID_W) % GRID_W).astype(F32)
    axis_dim = HEAD_DIM // 2
    inv = ROPE_THETA ** (-jnp.arange(0, axis_dim, 2, dtype=F32) / axis_dim)
    ang = jnp.concatenate([row[:, None] * inv, col[:, None] * inv], axis=-1)
    cos = jnp.repeat(jnp.cos(ang), 2, axis=-1)
    sin = jnp.repeat(jnp.sin(ang), 2, axis=-1)
    sign = jnp.tile(jnp.array([-1.0, 1.0], F32), HEAD_DIM // 2)
    return jnp.tile(cos, (1, 2)), jnp.tile(sin * sign, (1, 2))


def _permute_w_in(w):
    u_pool, g_pool, q, k, v, g_attn, qkv_d, b_d, a_d, g_delta = jnp.split(
        w, [256, 512, 1024, 1152, 1280, 1792, 2560, 2568, 2576], axis=1)
    pad = jnp.zeros((w.shape[0], LANES - 4 * N_DELTA_HEADS), w.dtype)
    return jnp.concatenate([u_pool, g_pool, g_attn, g_delta, q, k, v, qkv_d, b_d, a_d, pad], axis=1).astype(BF16)


def _lane_row(vec, offset):
    return jnp.zeros((1, LANES), F32).at[0, offset:offset + vec.size].set(vec.reshape(-1))


def _cached_kv_variants(k, v):
    lane = jnp.arange(LANES)
    pairs = _kv_variants(k, jnp.roll(k, HEAD_DIM, axis=-1), v, jnp.roll(v, HEAD_DIM, axis=-1), lane)
    return (jnp.concatenate([p[0] for p in pairs], axis=-1).astype(BF16),
            jnp.concatenate([p[1] for p in pairs], axis=-1).astype(BF16))


def _layer(x, mod, lw, rope_tabs, ctx, tiles):
    B, L, _ = x.shape
    latent = ctx is not None
    outs = _inproj(x, mod, lw["gpre"], lw["w_in"], lw["qkn"], lw["conv_w"], lw["alog_row"], lw["dtb_row"],
                   rope_tabs, tiles["tm"])
    up, gates, q, k4, v4, ksq, act, bg = outs[:8]
    sources = [(k4, v4)]
    if latent:
        sources.append(ctx["kv"])
    oattn = _attention(q, ksq, sources, tiles["tq"], tiles["tk"])

    gt = bg[:, :, 2 * N_DELTA_HEADS:4 * N_DELTA_HEADS].reshape(B, L // CHUNK, CHUNK, 2, N_DELTA_HEADS)
    gt = gt.transpose(0, 1, 3, 4, 2).reshape(B, L // CHUNK, 2, D_DELTA)
    pm, bm, ol, e = _delta_chunks(act, bg, gt, tiles["tc"])
    s0 = ctx["state"] if latent else jnp.zeros((B, 2, HEAD_DIM, D_DELTA), F32)
    o_f, o_b, s_out = _delta_scan(pm, bm, ol, e, s0, tiles["bb"], tiles["cs"])

    y = _merge(x, mod, up, gates, oattn, o_f, o_b, lw["pw_bd"], lw["pscale"], lw["onorm_row"],
               lw["w_out"], lw["gpost"], tiles["tm"])
    if latent:
        return y
    kn, v = outs[8], outs[9]
    s_out = s_out.reshape(B, 2, HEAD_DIM, N_DELTA_HEADS, HEAD_DIM).transpose(0, 1, 3, 2, 4)
    return y, kn, v, s_out


def kernel(x_prompt, x_sample, cache_attn_k, cache_attn_v, state_delta, c, c_ctx, w_mod, b_mod, norm_pre, norm_post,
           w_in, w_out, pool_w, pool_scale, q_norm, k_norm, conv_w, a_log, dt_bias, o_norm):
    B, L, _ = x_prompt.shape
    DB, DL, _ = x_sample.shape
    past = cache_attn_k.shape[2]

    conds = jnp.zeros((8, D_MODEL), F32).at[0].set(c_ctx).at[1:1 + DB].set(c)
    mod = _modulation(conds, w_mod, b_mod)
    rope_tabs = _rope_tables(DL)

    ctx_tiles = dict(tm=256, tq=256, tk=256, tc=256, bb=8, cs=4)
    lat_tiles = dict(tm=512, tq=256, tk=1024, tc=256, bb=4, cs=4)

    hp, hs = x_prompt, x_sample
    new_k, new_v, new_s = [], [], []
    for l in range(DEPTH):
        blocks = jnp.zeros((N_DELTA_HEADS, HEAD_DIM, N_DELTA_HEADS, HEAD_DIM), F32)
        blocks = blocks.at[jnp.arange(4), :, jnp.arange(4), :].set(pool_w[l])
        qkn = jnp.zeros((8, LANES), F32).at[0].set(jnp.tile(q_norm[l], 2)).at[1].set(jnp.tile(k_norm[l], 2))
        lw = dict(
            gpre=norm_pre[l].reshape(1, D_MODEL),
            gpost=norm_post[l].reshape(1, D_MODEL),
            w_in=_permute_w_in(w_in[l]),
            w_out=w_out[l].astype(BF16),
            qkn=qkn,
            conv_w=conv_w[l],
            alog_row=_lane_row(a_log[l], 2 * N_DELTA_HEADS),
            dtb_row=_lane_row(dt_bias[l], 2 * N_DELTA_HEADS),
            pw_bd=blocks.reshape(D_POOL, D_POOL).astype(BF16),
            pscale=pool_scale[l].reshape(1, D_POOL),
            onorm_row=jnp.tile(o_norm[l], 2).reshape(1, LANES),
        )
        hp, k_l, v_l, s_l = _layer(hp, mod[l, 0:1].reshape(1, 1, 3 * D_MODEL), lw, None, None, ctx_tiles)
        ctx = dict(kv=_cached_kv_variants(cache_attn_k[:, l].reshape(DB, past, D_KV),
                                          cache_attn_v[:, l].reshape(DB, past, D_KV)),
                   state=state_delta[:, l].transpose(0, 1, 3, 2, 4).reshape(DB, 2, HEAD_DIM, D_DELTA))
        hs = _layer(hs, mod[l, 1:1 + DB].reshape(DB, 1, 3 * D_MODEL), lw, rope_tabs, ctx, lat_tiles)
        new_k.append(k_l.reshape(B, L, 2, HEAD_DIM))
        new_v.append(v_l.reshape(B, L, 2, HEAD_DIM))
        new_s.append(s_l)
    return (hp, hs, jnp.stack(new_k, axis=1), jnp.stack(new_v, axis=1), jnp.stack(new_s, axis=1))
```

```python
import functools

import jax
import jax.numpy as jnp
from jax import lax
from jax.experimental import pallas as pl
from jax.experimental.pallas import tpu as pltpu

F32 = jnp.float32
BF16 = jnp.bfloat16

D_MODEL = 1024
DEPTH = 2
GRID_W = 64
HEAD_DIM = 64
D_POOL = 256
D_ATTN = 512
D_DELTA = 256
D_KV = 128
N_Q_HEADS = 8
N_DELTA_HEADS = 4
POOL_WINDOWS = (2, 4, 8, 16)
CHUNK = 64
ROPE_THETA = 10000.0
EPS = 1e-6
LANES = 128
HALO = 8
MASK_VALUE = -1e30
LOG2E = 1.4426950408889634
Q_SCALE = HEAD_DIM ** -0.5 * LOG2E
SHIFT_LIMIT = 60.0

OFF_UP = 0
OFF_GATES = 256
OFF_QK = 1280
OFF_V = 1920
OFF_QKVD = 2048
OFF_BA = 2816
D_IN_PAD = 2944

VMEM_LIMIT = 56 * 1024 * 1024


def _cparams(*sem):
    return pltpu.CompilerParams(dimension_semantics=sem, vmem_limit_bytes=VMEM_LIMIT)


def _dot(a, b):
    return jnp.dot(a, b, preferred_element_type=F32)


def _dot_nt(a, b):
    return lax.dot_general(a, b, (((1,), (1,)), ((), ())), preferred_element_type=F32)


def _dot_tn(a, b):
    return lax.dot_general(a, b, (((0,), (0,)), ((), ())), preferred_element_type=F32)


def _split(x):
    hi = x.astype(BF16)
    return hi, (x - hi.astype(F32)).astype(BF16)


def _silu(x):
    return x * jax.nn.sigmoid(x)


def _softplus(x):
    return jnp.maximum(x, 0.0) + jnp.log1p(jnp.exp(-jnp.abs(x)))


def _kv_variants(k, k_sw, v, v_sw, lane):
    lo = lane < HEAD_DIM
    one_hi = (lane == HEAD_DIM).astype(k.dtype)
    one_lo = (lane == 0).astype(k.dtype)
    ones = jnp.ones_like(v)
    return ((jnp.where(lo, k, one_hi), jnp.where(lo, v, ones)),
            (jnp.where(lo, one_lo, k_sw), jnp.where(lo, ones, v_sw)),
            (jnp.where(lo, k_sw, one_hi), jnp.where(lo, v_sw, ones)),
            (jnp.where(lo, one_lo, k), jnp.where(lo, ones, v)))


def _half_rms_scale(x, lo_mask):
    sq = x * x
    ss_lo = jnp.sum(jnp.where(lo_mask, sq, 0.0), axis=-1, keepdims=True)
    ss_hi = jnp.sum(jnp.where(lo_mask, 0.0, sq), axis=-1, keepdims=True)
    r_lo = lax.rsqrt(ss_lo * (1.0 / HEAD_DIM) + EPS)
    r_hi = lax.rsqrt(ss_hi * (1.0 / HEAD_DIM) + EPS)
    return jnp.where(lo_mask, r_lo, r_hi)


def _mod_kernel(c_ref, w_ref, b_ref, o_ref):
    s = _silu(c_ref[...])
    o_ref[0] = _dot(s.astype(BF16), w_ref[0].astype(BF16)) + b_ref[0]


def _modulation(conds, w_mod, b_mod):
    tn = 768
    return pl.pallas_call(
        _mod_kernel,
        grid=(DEPTH, 3 * D_MODEL // tn),
        in_specs=[pl.BlockSpec((8, D_MODEL), lambda l, j: (0, 0)),
                  pl.BlockSpec((1, D_MODEL, tn), lambda l, j: (l, 0, j)),
                  pl.BlockSpec((1, 1, tn), lambda l, j: (l, 0, j))],
        out_specs=pl.BlockSpec((1, 8, tn), lambda l, j: (l, 0, j)),
        out_shape=jax.ShapeDtypeStruct((DEPTH, 8, 3 * D_MODEL), F32),
        compiler_params=_cparams("parallel", "parallel"),
        name="modulation",
    )(conds, w_mod, b_mod.reshape(DEPTH, 1, 3 * D_MODEL))


def _inproj_kernel(rope, x_ref, xprev_ref, xnext_ref, mod_ref, gpre_ref, w_ref, qkn_ref, convw_ref, alog_ref, dtb_ref,
                   *rest):
    if rope:
        cos_ref, sin_ref, up_ref, gates_ref, q_ref, k4_ref, v4_ref, ksq_ref, act_ref, bg_ref, pad_ref = rest
    else:
        up_ref, gates_ref, q_ref, k4_ref, v4_ref, ksq_ref, act_ref, bg_ref, kn_ref, v_ref, pad_ref = rest
    i = pl.program_id(1)
    tm = x_ref.shape[1]
    mod = mod_ref[0]
    shift = mod[:, 0:D_MODEL]
    scale = mod[:, D_MODEL:2 * D_MODEL]

    def modulated_norm(x):
        ms = jnp.mean(x * x, axis=-1, keepdims=True)
        return ((x * lax.rsqrt(ms + EPS)) * gpre_ref[...] * (1.0 + scale) + shift).astype(BF16)

    hb = modulated_norm(x_ref[0])

    zqkv = _dot(hb, w_ref[:, OFF_QK:OFF_QK + D_ATTN + 2 * D_KV])
    zqk, v = zqkv[:, 0:D_ATTN + D_KV], zqkv[:, D_ATTN + D_KV:]
    wd = w_ref[:, OFF_QKVD:OFF_QKVD + 3 * D_DELTA]
    pad_ref[0:HALO, :] = jnp.where(i == 0, 0.0, _dot(modulated_norm(xprev_ref[0]), wd))
    pad_ref[HALO:HALO + tm, :] = _dot(hb, wd)
    pad_ref[HALO + tm:2 * HALO + tm, :] = jnp.where(i == pl.num_programs(1) - 1, 0.0, _dot(modulated_norm(xnext_ref[0]), wd))
    zgates = _dot(hb, w_ref[:, OFF_GATES:OFF_GATES + D_MODEL])

    lane = lax.broadcasted_iota(jnp.int32, (tm, LANES), 1)
    lo_mask = lane < HEAD_DIM
    even = (lane % 2) == 0

    def head_norm_rope(blk, gain):
        y = blk * _half_rms_scale(blk, lo_mask) * gain
        if rope:
            swapped = jnp.where(even, pltpu.roll(y, LANES - 1, 1), pltpu.roll(y, 1, 1))
            return y, y * cos_ref[...] + swapped * sin_ref[...]
        return y, y

    for qb in range(D_ATTN // LANES):
        _, qr = head_norm_rope(zqk[:, qb * LANES:(qb + 1) * LANES], qkn_ref[0:1, :])
        q_ref[0, :, qb * LANES:(qb + 1) * LANES] = (qr * Q_SCALE).astype(BF16)
    gates_ref[0] = _silu(zgates).astype(BF16)
    up_ref[0] = _dot(hb, w_ref[:, OFF_UP:OFF_UP + D_POOL])
    ba = _dot(hb, w_ref[:, OFF_BA:OFF_BA + LANES])

    xp = pad_ref[...]
    rows = xp.shape[0]
    acc = None
    for j in range(4):
        shifted = xp if j == 2 else pltpu.roll(xp, (2 - j) % rows, 0)
        term = convw_ref[j:j + 1, :] * shifted[HALO:HALO + tm, :]
        acc = term if acc is None else acc + term
    act_ref[0] = _silu(acc)
    g = -jnp.exp(alog_ref[...]) * _softplus(ba + dtb_ref[...])
    bg_ref[0] = jnp.where(lane < 2 * N_DELTA_HEADS, jax.nn.sigmoid(ba), jnp.where(lane < 4 * N_DELTA_HEADS, g, 0.0))

    kn, kr = head_norm_rope(zqk[:, D_ATTN:D_ATTN + D_KV], qkn_ref[1:2, :])
    for a, (kx, vx) in enumerate(_kv_variants(kr, pltpu.roll(kr, HEAD_DIM, 1), v, pltpu.roll(v, HEAD_DIM, 1), lane)):
        k4_ref[0, :, a * LANES:(a + 1) * LANES] = kx.astype(BF16)
        v4_ref[0, :, a * LANES:(a + 1) * LANES] = vx.astype(BF16)
    ksq = kr * kr
    ksq = jnp.where(lo_mask, jnp.sum(jnp.where(lo_mask, ksq, 0.0), axis=-1, keepdims=True),
                    jnp.sum(jnp.where(lo_mask, 0.0, ksq), axis=-1, keepdims=True))
    ksq_ref[0, 0] = jnp.broadcast_to(jnp.max(ksq, axis=0, keepdims=True), (HALO, LANES))
    if not rope:
        kn_ref[0] = kn
        v_ref[0] = v


def _inproj(x, mod, gpre, w, qkn, conv_w, alog_row, dtb_row, rope_tabs, tm):
    B, L, _ = x.shape
    rope = rope_tabs is not None
    shared_mod = mod.shape[0] == 1
    nb = tm // HALO
    last_blk = L // HALO - 1
    row = lambda b, i: (b, i, 0)
    const2 = lambda b, i: (0, 0)
    in_specs = [pl.BlockSpec((1, tm, D_MODEL), row),
                pl.BlockSpec((1, HALO, D_MODEL), lambda b, i: (b, jnp.maximum(i * nb - 1, 0), 0)),
                pl.BlockSpec((1, HALO, D_MODEL), lambda b, i: (b, jnp.minimum((i + 1) * nb, last_blk), 0)),
                pl.BlockSpec((1, 1, 3 * D_MODEL), (lambda b, i: (0, 0, 0)) if shared_mod else (lambda b, i: (b, 0, 0))),
                pl.BlockSpec((1, D_MODEL), const2),
                pl.BlockSpec((None, D_MODEL, D_IN_PAD), lambda b, i: (w[1], 0, 0)),
                pl.BlockSpec((8, LANES), const2),
                pl.BlockSpec((4, 3 * D_DELTA), const2),
                pl.BlockSpec((1, LANES), const2),
                pl.BlockSpec((1, LANES), const2)]
    args = [x, x, x, mod, gpre, w[0], qkn, conv_w, alog_row, dtb_row]
    if rope:
        in_specs += [pl.BlockSpec((tm, LANES), lambda b, i: (i, 0))] * 2
        args += list(rope_tabs)
    widths = [(D_POOL, F32), (D_MODEL, BF16), (D_ATTN, BF16), (4 * LANES, BF16), (4 * LANES, BF16), None,
              (3 * D_DELTA, F32), (LANES, F32)]
    if not rope:
        widths += [(D_KV, F32), (D_KV, F32)]
    out_specs = [pl.BlockSpec((1, tm, wd[0]), row) if wd else pl.BlockSpec((1, 1, HALO, LANES), lambda b, i: (b, i, 0, 0))
                 for wd in widths]
    out_shape = [jax.ShapeDtypeStruct((B, L, wd[0]), wd[1]) if wd else jax.ShapeDtypeStruct((B, L // tm, HALO, LANES), F32)
                 for wd in widths]
    return pl.pallas_call(
        functools.partial(_inproj_kernel, rope),
        grid=(B, L // tm),
        in_specs=in_specs,
        out_specs=out_specs,
        out_shape=out_shape,
        scratch_shapes=[pltpu.VMEM((tm + 2 * HALO, 3 * D_DELTA), F32)],
        compiler_params=_cparams("parallel", "parallel"),
        name="inproj_rope" if rope else "inproj",
    )(*args)


_VARIANT_HEADS = ((0, 2), (1, 3), (4, 6), (5, 7))


def _attn_kernel(n_src, tk, q_ref, ksq_ref, *rest):
    kv_refs = rest[:2 * n_src]
    o_ref, acc_ref, m_ref = rest[2 * n_src:]
    tq = q_ref.shape[1]
    n_var = len(_VARIANT_HEADS)
    lane = lax.broadcasted_iota(jnp.int32, (tq, LANES), 1)
    lo_mask = lane < HEAD_DIM

    ksq = jnp.max(ksq_ref[0], axis=0)[0:1, :]
    lo_row = lo_mask[0:1, :]
    ksq_head = [jnp.max(jnp.where(lo_row, ksq, 0.0), axis=-1, keepdims=True),
                jnp.max(jnp.where(lo_row, 0.0, ksq), axis=-1, keepdims=True)]
    for j in range(1, n_src):
        for kv_head in range(2):
            kx = kv_refs[2 * j][0, :, 2 * kv_head * LANES:(2 * kv_head + 1) * LANES].astype(F32)
            sq = jnp.sum(jnp.where(lane[0:1, :] < HEAD_DIM, kx * kx, 0.0), axis=-1, keepdims=True)
            ksq_head[kv_head] = jnp.maximum(ksq_head[kv_head], jnp.max(sq, axis=0, keepdims=True))

    def own_half(h):
        blk = q_ref[0, :, (h // 2) * LANES:(h // 2 + 1) * LANES]
        return jnp.where(lo_mask if h % 2 == 0 else jnp.logical_not(lo_mask), blk, jnp.zeros_like(blk))

    q_shift, bound_max = [], None
    for heads in _VARIANT_HEADS:
        shifted = []
        for h in heads:
            qf = own_half(h).astype(F32)
            bound = jnp.sqrt(jnp.sum(qf * qf, axis=-1, keepdims=True) * ksq_head[h // 4])
            bias_lane = HEAD_DIM if h % 2 == 0 else 0
            shifted.append(jnp.where(lane == bias_lane, -bound, qf).astype(BF16))
            bound_max = bound if bound_max is None else jnp.maximum(bound_max, bound)
        q_shift.append(jnp.concatenate(shifted, axis=0))
    safe = jnp.max(bound_max) <= SHIFT_LIMIT

    cols = [slice(a * LANES, (a + 1) * LANES) for a in range(n_var)]
    chunks = []
    for j in range(n_src):
        k4_ref, v4_ref = kv_refs[2 * j], kv_refs[2 * j + 1]
        S = k4_ref.shape[1]
        size = min(tk, S)
        chunks += [(k4_ref, v4_ref, c * size, size) for c in range(S // size)]

    @pl.when(safe)
    def _():
        def scores(ch):
            k4_ref, _, start, size = ch
            return [_dot_nt(q_shift[a], k4_ref[0, start:start + size, cols[a]]) for a in range(n_var)]

        acc = [None] * n_var
        s_next = scores(chunks[0])
        for ci, (_, v4_ref, start, size) in enumerate(chunks):
            s = s_next
            if ci + 1 < len(chunks):
                s_next = scores(chunks[ci + 1])
            for a in range(n_var):
                pv = _dot(jnp.exp2(s[a]).astype(BF16), v4_ref[0, start:start + size, cols[a]])
                acc[a] = pv if acc[a] is None else acc[a] + pv
        for a in range(n_var):
            acc_ref[a] = acc[a]

    @pl.when(jnp.logical_not(safe))
    def _():
        m_ref[...] = jnp.full(m_ref.shape, MASK_VALUE, F32)
        acc_ref[...] = jnp.zeros(acc_ref.shape, F32)
        q_plain = [jnp.concatenate([own_half(h) for h in heads], axis=0) for heads in _VARIANT_HEADS]
        for k4_ref, v4_ref, start, size in chunks:
            for a in range(n_var):
                s = _dot_nt(q_plain[a], k4_ref[0, start:start + size, cols[a]])
                m_prev = m_ref[a]
                m_new = jnp.maximum(m_prev, jnp.max(s, axis=-1, keepdims=True))
                p = jnp.exp2(s - jnp.tile(m_new, (1, size // LANES)))
                m_ref[a] = m_new
                acc_ref[a] = jnp.exp2(m_prev - m_new) * acc_ref[a] + _dot(p.astype(BF16), v4_ref[0, start:start + size, cols[a]])

    def head_out(h):
        a = 2 * (h // 4) + h % 2
        r = _VARIANT_HEADS[a].index(h) * tq
        rows = acc_ref[a, r:r + tq, :]
        return rows / pltpu.roll(rows, HEAD_DIM, 1)

    for i in range(N_Q_HEADS // 2):
        o_ref[0, :, i * LANES:(i + 1) * LANES] = jnp.where(lo_mask, head_out(2 * i), head_out(2 * i + 1)).astype(BF16)


def _attention(q, ksq, sources, tq, tk):
    B, L, _ = q.shape
    n_var = len(_VARIANT_HEADS)
    in_specs = [pl.BlockSpec((1, tq, D_ATTN), lambda b, i: (b, i, 0)),
                pl.BlockSpec((1,) + ksq.shape[1:], lambda b, i: (b, 0, 0, 0))]
    args = [q, ksq]
    for k4, v4 in sources:
        S = k4.shape[1]
        in_specs += [pl.BlockSpec((1, S, n_var * LANES), lambda b, i: (b, 0, 0))] * 2
        args += [k4, v4]
    rows = N_Q_HEADS // n_var * tq
    return pl.pallas_call(
        functools.partial(_attn_kernel, len(sources), tk),
        grid=(B, L // tq),
        in_specs=in_specs,
        out_specs=pl.BlockSpec((1, tq, D_ATTN), lambda b, i: (b, i, 0)),
        out_shape=jax.ShapeDtypeStruct((B, L, D_ATTN), BF16),
        scratch_shapes=[pltpu.VMEM((n_var, rows, LANES), F32)] * 2,
        compiler_params=_cparams("parallel", "parallel"),
        name="attention",
    )(*args)


def _block_diag(x, bd_mask):
    return jnp.where(bd_mask, jnp.concatenate([x] * N_DELTA_HEADS, axis=0), jnp.zeros((), x.dtype))


def _dot_split_bd(a, b, bd_mask):
    (ah, al), (bh, bl) = a, b
    m = ah.shape[0]
    both = _dot(jnp.concatenate([ah, al], axis=0), _block_diag(bh, bd_mask))
    return both[0:m] + both[m:2 * m] + _dot(ah, _block_diag(bl, bd_mask))


def _delta_chunk_kernel(act_ref, bg_ref, gt_ref, pm_ref, bm_ref, ol_ref, e_ref):
    C, W, H = CHUNK, D_DELTA, N_DELTA_HEADS
    n_chunks = act_ref.shape[1] // C
    ri = lax.broadcasted_iota(jnp.int32, (C, W), 0)
    lane = lax.broadcasted_iota(jnp.int32, (C, W), 1)
    lj = lane % HEAD_DIM
    blk = lane // HEAD_DIM
    bd_mask = (lax.broadcasted_iota(jnp.int32, (W, W), 0) // HEAD_DIM) == (lax.broadcasted_iota(jnp.int32, (W, W), 1) // HEAD_DIM)
    diag = ri == lj
    eye = diag.astype(F32)

    def expand(cols):
        res = cols[H - 1]
        for h in range(H - 2, -1, -1):
            res = jnp.where(blk == h, cols[h], res)
        return res

    def seg_sum(x):
        return expand([jnp.sum(jnp.where(blk == h, x, 0.0), axis=-1, keepdims=True) for h in range(H)])

    def l2n(x):
        return x * lax.rsqrt(seg_sum(x * x) + EPS)

    def row_form(x):
        res = x[(H - 1) * C:H * C]
        for h in range(H - 2, -1, -1):
            res = jnp.where(blk == h, x[h * C:(h + 1) * C], res)
        return res

    units = []
    for c in range(n_chunks):
        rows = slice(c * C, (c + 1) * C)
        q = l2n(act_ref[0, rows, 0:W]) * (HEAD_DIM ** -0.5)
        k = l2n(act_ref[0, rows, W:2 * W])
        v = act_ref[0, rows, 2 * W:3 * W]
        kb = k.astype(BF16)
        kkqk = _dot_nt(jnp.concatenate([kb, q.astype(BF16)], axis=0), _block_diag(kb, bd_mask))
        kk, qk = kkqk[0:C], kkqk[C:2 * C]
        bg = bg_ref[0, rows, :]
        gt = gt_ref[0, c]
        for d in range(2):
            beta = expand([bg[:, d * H + h:d * H + h + 1] for h in range(H)])
            g = expand([bg[:, 2 * H + d * H + h:2 * H + d * H + h + 1] for h in range(H)])
            g_row = gt[d:d + 1, :]
            incl = (ri >= lj) if d == 0 else (ri <= lj)
            incl_t = (ri <= lj) if d == 0 else (ri >= lj)
            strict = (ri > lj) if d == 0 else (ri < lj)
            gc = seg_sum(jnp.where(incl, g_row, 0.0))
            gc_row = jnp.sum(jnp.where(incl_t, g, 0.0), axis=0, keepdims=True)
            g_tot = jnp.sum(g, axis=0, keepdims=True)
            decay = jnp.where(incl, jnp.exp(jnp.where(incl, gc - gc_row, 0.0)), 0.0)
            t = jnp.where(strict, -(beta * kk * decay), 0.0)
            units.append(dict(c=c, d=d, q=q, k=k, v=v, beta=beta, gc=gc, g_tot=g_tot,
                              attn=(qk * decay).astype(BF16), t=t, p=eye + t))

    for un in units:
        un["t"] = _dot_split_bd(_split(un["t"]), _split(un["t"]), bd_mask)
    for level in range(4):
        for un in units:
            tp = jnp.concatenate([un["t"], un["p"]], axis=0)
            if level < 3:
                both = _dot_split_bd(_split(tp), _split(un["t"]), bd_mask)
            else:
                both = _dot(tp.astype(BF16), _block_diag(un["t"].astype(BF16), bd_mask))
            un["t"] = both[0:C]
            un["p"] = un["p"] + both[C:2 * C]
    for un in units:
        un["p"] = (un["p"] + _dot(un["p"].astype(BF16), _block_diag(un["t"].astype(BF16), bd_mask))).astype(BF16)
    for un in units:
        un["egc"] = jnp.exp(un["gc"])
        un["u"] = _dot(un["p"], _block_diag((un["v"] * un["beta"]).astype(BF16), bd_mask)).astype(BF16)
        un["w"] = _dot(un["p"], _block_diag((un["k"] * (un["beta"] * un["egc"])).astype(BF16), bd_mask)).astype(BF16)
    for un in units:
        c, d = un["c"], un["d"]
        kdec = (un["k"] * jnp.exp(un["g_tot"] - un["gc"])).astype(BF16)
        bm_ref[0, d, c] = row_form(_dot_tn(kdec, un["u"])).astype(BF16)
        mp = row_form(_dot_tn(kdec, un["w"]))
        qt = un["q"] * un["egc"] - _dot(un["attn"], _block_diag(un["w"], bd_mask))
        ol_ref[0, d, c] = _dot(un["attn"], _block_diag(un["u"], bd_mask)).astype(BF16)
        pm_ref[0, d, c] = jnp.concatenate([mp, qt], axis=0).astype(BF16)
        e_ref[0, d, c] = jnp.broadcast_to(jnp.exp(un["g_tot"]), (HALO, W))


def _delta_chunks(act, bg, gt, tc):
    B, L, _ = act.shape
    N = L // CHUNK
    cb = tc // CHUNK
    W = D_DELTA
    ospec = lambda r: pl.BlockSpec((1, 2, cb, r, W), lambda b, i: (b, 0, i, 0, 0))
    return pl.pallas_call(
        _delta_chunk_kernel,
        grid=(B, L // tc),
        in_specs=[pl.BlockSpec((1, tc, 3 * W), lambda b, i: (b, i, 0)),
                  pl.BlockSpec((1, tc, LANES), lambda b, i: (b, i, 0)),
                  pl.BlockSpec((1, cb, 2, W), lambda b, i: (b, i, 0, 0))],
        out_specs=[ospec(2 * CHUNK), ospec(CHUNK), ospec(CHUNK), ospec(HALO)],
        out_shape=[jax.ShapeDtypeStruct((B, 2, N, 2 * CHUNK, W), BF16),
                   jax.ShapeDtypeStruct((B, 2, N, CHUNK, W), BF16),
                   jax.ShapeDtypeStruct((B, 2, N, CHUNK, W), BF16),
                   jax.ShapeDtypeStruct((B, 2, N, HALO, W), F32)],
        compiler_params=_cparams("parallel", "parallel"),
        name="delta_chunks",
    )(act, bg, gt)


def _delta_scan_kernel(pmf_ref, bmf_ref, olf_ref, ef_ref, pmb_ref, bmb_ref, olb_ref, eb_ref, s0_ref,
                       of_ref, ob_ref, sout_ref, s_ref):
    n = pl.program_id(1)
    bb = s_ref.shape[0]
    C, W = CHUNK, D_DELTA
    bd_mask = (lax.broadcasted_iota(jnp.int32, (W, W), 0) // HEAD_DIM) == (lax.broadcasted_iota(jnp.int32, (W, W), 1) // HEAD_DIM)

    @pl.when(n == 0)
    def _():
        s_ref[...] = s0_ref[...]

    dirs = ((pmf_ref, bmf_ref, olf_ref, ef_ref, of_ref), (pmb_ref, bmb_ref, olb_ref, eb_ref, ob_ref))
    cs = pmf_ref.shape[2]
    chains = [(b, d) for b in range(bb) for d in range(2)]
    s = [s_ref[b, d] for b, d in chains]
    for step in range(cs):
        idx = [step if d == 0 else cs - 1 - step for _, d in chains]
        r = [_dot(dirs[d][0][b, 0, idx[i]], _block_diag(s[i].astype(BF16), bd_mask)) for i, (b, d) in enumerate(chains)]
        for i, (b, d) in enumerate(chains):
            _, bm_ref, ol_ref, e_ref, o_ref = dirs[d]
            s[i] = e_ref[b, 0, idx[i]][0:1, :] * s[i] + bm_ref[b, 0, idx[i]].astype(F32) - r[i][0:C]
            o_ref[b, idx[i] * C:(idx[i] + 1) * C, :] = (r[i][C:2 * C] + ol_ref[b, 0, idx[i]].astype(F32)).astype(BF16)
    for i, (b, d) in enumerate(chains):
        s_ref[b, d] = s[i]

    @pl.when(n == pl.num_programs(1) - 1)
    def _():
        sout_ref[...] = s_ref[...]


def _delta_scan(pm, bm, ol, e, s0, bb, cs):
    B, _, N, _, W = pm.shape
    L = N * CHUNK
    steps = N // cs

    def spec(r, d):
        if d == 0:
            return pl.BlockSpec((bb, 1, cs, r, W), lambda b, n: (b, 0, n, 0, 0))
        return pl.BlockSpec((bb, 1, cs, r, W), lambda b, n: (b, 1, steps - 1 - n, 0, 0))

    in_specs = []
    for d in range(2):
        in_specs += [spec(2 * CHUNK, d), spec(CHUNK, d), spec(CHUNK, d), spec(HALO, d)]
    in_specs.append(pl.BlockSpec((bb, 2, CHUNK, W), lambda b, n: (b, 0, 0, 0)))
    return pl.pallas_call(
        _delta_scan_kernel,
        grid=(B // bb, steps),
        in_specs=in_specs,
        out_specs=[pl.BlockSpec((bb, cs * CHUNK, W), lambda b, n: (b, n, 0)),
                   pl.BlockSpec((bb, cs * CHUNK, W), lambda b, n: (b, steps - 1 - n, 0)),
                   pl.BlockSpec((bb, 2, CHUNK, W), lambda b, n: (b, 0, 0, 0))],
        out_shape=[jax.ShapeDtypeStruct((B, L, W), BF16),
                   jax.ShapeDtypeStruct((B, L, W), BF16),
                   jax.ShapeDtypeStruct((B, 2, CHUNK, W), F32)],
        scratch_shapes=[pltpu.VMEM((bb, 2, CHUNK, W), F32)],
        compiler_params=_cparams("parallel", "arbitrary"),
        name="delta_scan",
    )(pm, bm, ol, e, pm, bm, ol, e, s0)


def _merge_kernel(x_ref, mod_ref, up_ref, upp_ref, upn_ref, cnt_ref, gates_ref, oattn_ref, of_ref, ob_ref,
                  pw_ref, pscale_ref, onorm_ref, wout_ref, gpost_ref, o_ref, pad_ref):
    i = pl.program_id(1)
    tm = x_ref.shape[1]
    first = i == 0
    last = i == pl.num_programs(1) - 1
    gates = gates_ref[0].astype(F32)
    y = _dot((gates[:, D_POOL:D_POOL + D_ATTN] * oattn_ref[0].astype(F32)).astype(BF16),
             wout_ref[D_POOL:D_POOL + D_ATTN, :])
    pad_ref[0:HALO, :] = jnp.where(first, 0.0, upp_ref[0])
    pad_ref[HALO:HALO + tm, :] = up_ref[0]
    pad_ref[HALO + tm:2 * HALO + tm, :] = jnp.where(last, 0.0, upn_ref[0])

    def window_sum(col, lo, hi):
        padded = pad_ref[:, col]
        rows = padded.shape[0]
        acc = None
        for j in range(lo, hi):
            term = (padded if j == 0 else pltpu.roll(padded, (-j) % rows, 0))[HALO:HALO + tm, :]
            acc = term if acc is None else acc + term
        return acc

    lane = lax.broadcasted_iota(jnp.int32, (tm, LANES), 1)
    lo_mask = lane < HEAD_DIM

    pooled = []
    for col_blk, (w_lo, w_hi) in enumerate(((2, 4), (8, 16))):
        col = slice(col_blk * LANES, (col_blk + 1) * LANES)
        s_lo = window_sum(col, -(w_lo // 2), w_lo // 2)
        s_hi = s_lo + window_sum(col, -(w_hi // 2), -(w_lo // 2)) + window_sum(col, w_lo // 2, w_hi // 2)
        mean = jnp.where(lo_mask, s_lo, s_hi) / cnt_ref[:, col].astype(F32)
        pooled.append(mean - up_ref[0, :, col])
    pooled = jnp.concatenate(pooled, axis=1)
    o_pool = _dot(pooled.astype(BF16), pw_ref[...]) * pscale_ref[...]

    od = of_ref[0].astype(F32) + ob_ref[0].astype(F32)
    odn = []
    for j in range(D_DELTA // LANES):
        blk = od[:, j * LANES:(j + 1) * LANES]
        odn.append(blk * _half_rms_scale(blk, lo_mask) * onorm_ref[...])
    o_delta = jnp.concatenate(odn, axis=1)

    y = y + _dot((gates[:, 0:D_POOL] * o_pool).astype(BF16), wout_ref[0:D_POOL, :])
    y = y + _dot((gates[:, D_POOL + D_ATTN:] * o_delta).astype(BF16), wout_ref[D_POOL + D_ATTN:, :])
    ms = jnp.mean(y * y, axis=-1, keepdims=True)
    yn = (y * lax.rsqrt(ms + EPS)) * gpost_ref[...]
    gate = mod_ref[0][:, 2 * D_MODEL:3 * D_MODEL]
    o_ref[0] = x_ref[0] + gate * yn


def _window_counts(seq_len):
    t = jnp.arange(seq_len)[:, None]
    w = jnp.repeat(jnp.array(POOL_WINDOWS), D_POOL // len(POOL_WINDOWS))[None, :]
    return (jnp.minimum(t - w // 2 + w, seq_len) - jnp.maximum(t - w // 2, 0)).astype(BF16)


def _merge(x, mod, up, gates, oattn, o_f, o_b, pw_bd, pscale, onorm_row, wout, gpost, tm):
    B, L, _ = x.shape
    nb = tm // HALO
    last_blk = L // HALO - 1
    shared_mod = mod.shape[0] == 1
    row = lambda b, i: (b, i, 0)
    const2 = lambda b, i: (0, 0)
    return pl.pallas_call(
        _merge_kernel,
        grid=(B, L // tm),
        in_specs=[pl.BlockSpec((1, tm, D_MODEL), row),
                  pl.BlockSpec((1, 1, 3 * D_MODEL), (lambda b, i: (0, 0, 0)) if shared_mod else (lambda b, i: (b, 0, 0))),
                  pl.BlockSpec((1, tm, D_POOL), row),
                  pl.BlockSpec((1, HALO, D_POOL), lambda b, i: (b, jnp.maximum(i * nb - 1, 0), 0)),
                  pl.BlockSpec((1, HALO, D_POOL), lambda b, i: (b, jnp.minimum((i + 1) * nb, last_blk), 0)),
                  pl.BlockSpec((tm, D_POOL), lambda b, i: (i, 0)),
                  pl.BlockSpec((1, tm, D_MODEL), row),
                  pl.BlockSpec((1, tm, D_ATTN), row),
                  pl.BlockSpec((1, tm, D_DELTA), row),
                  pl.BlockSpec((1, tm, D_DELTA), row),
                  pl.BlockSpec((D_POOL, D_POOL), const2),
                  pl.BlockSpec((1, D_POOL), const2),
                  pl.BlockSpec((1, LANES), const2),
                  pl.BlockSpec((None, D_MODEL, D_MODEL), lambda b, i: (wout[1], 0, 0)),
                  pl.BlockSpec((1, D_MODEL), const2)],
        out_specs=pl.BlockSpec((1, tm, D_MODEL), row),
        out_shape=jax.ShapeDtypeStruct((B, L, D_MODEL), F32),
        scratch_shapes=[pltpu.VMEM((tm + 2 * HALO, D_POOL), F32)],
        compiler_params=_cparams("parallel", "parallel"),
        name="merge",
    )(x, mod, up, up, up, _window_counts(L), gates, oattn, o_f, o_b, pw_bd, pscale, onorm_row, wout[0], gpost)


def _rope_tables(num_tokens):
    rows = num_tokens // GRID_W
    row = jnp.repeat(jnp.arange(rows, dtype=F32), GRID_W)
    col = (jnp.arange(rows * GRID_W) % GRID_W).astype(F32)
    axis_dim = HEAD_DIM // 2
    inv = ROPE_THETA ** (-jnp.arange(0, axis_dim, 2, dtype=F32) / axis_dim)
    ang = jnp.concatenate([row[:, None] * inv, col[:, None] * inv], axis=-1)
    cos = jnp.repeat(jnp.cos(ang), 2, axis=-1)
    sin = jnp.repeat(jnp.sin(ang), 2, axis=-1)
    sign = jnp.tile(jnp.array([-1.0, 1.0], F32), HEAD_DIM // 2)
    return jnp.tile(cos, (1, 2)), jnp.tile(sin * sign, (1, 2))


def _permute_w_in(w):
    u_pool, g_pool, q, k, v, g_attn, qkv_d, b_d, a_d, g_delta = jnp.split(
        w, [256, 512, 1024, 1152, 1280, 1792, 2560, 2568, 2576], axis=-1)
    pad = jnp.zeros(w.shape[:-1] + (LANES - 4 * N_DELTA_HEADS,), w.dtype)
    return jnp.concatenate([u_pool, g_pool, g_attn, g_delta, q, k, v, qkv_d, b_d, a_d, pad], axis=-1).astype(BF16)


def _lane_row(vec, offset):
    return jnp.zeros((1, LANES), F32).at[0, offset:offset + vec.size].set(vec.reshape(-1))


def _cached_kv_variants(k, v):
    lane = jnp.arange(LANES)
    pairs = _kv_variants(k, jnp.roll(k, HEAD_DIM, axis=-1), v, jnp.roll(v, HEAD_DIM, axis=-1), lane)
    return (jnp.concatenate([p[0] for p in pairs], axis=-1).astype(BF16),
            jnp.concatenate([p[1] for p in pairs], axis=-1).astype(BF16))


def _layer(x, mod, lw, rope_tabs, ctx, tiles):
    B, L, _ = x.shape
    latent = ctx is not None
    outs = _inproj(x, mod, lw["gpre"], lw["w_in"], lw["qkn"], lw["conv_w"], lw["alog_row"], lw["dtb_row"],
                   rope_tabs, tiles["tm"])
    up, gates, q, k4, v4, ksq, act, bg = outs[:8]
    sources = [(k4, v4)]
    if latent:
        sources.append(ctx["kv"])
    oattn = _attention(q, ksq, sources, tiles["tq"], tiles["tk"])

    gt = bg[:, :, 2 * N_DELTA_HEADS:4 * N_DELTA_HEADS].reshape(B, L // CHUNK, CHUNK, 2, N_DELTA_HEADS)
    gt = gt.transpose(0, 1, 3, 4, 2).reshape(B, L // CHUNK, 2, D_DELTA)
    pm, bm, ol, e = _delta_chunks(act, bg, gt, tiles["tc"])
    s0 = ctx["state"] if latent else jnp.zeros((B, 2, HEAD_DIM, D_DELTA), F32)
    o_f, o_b, s_out = _delta_scan(pm, bm, ol, e, s0, tiles["bb"], tiles["cs"])

    y = _merge(x, mod, up, gates, oattn, o_f, o_b, lw["pw_bd"], lw["pscale"], lw["onorm_row"],
               lw["w_out"], lw["gpost"], tiles["tm"])
    if latent:
        return y
    kn, v = outs[8], outs[9]
    s_out = s_out.reshape(B, 2, HEAD_DIM, N_DELTA_HEADS, HEAD_DIM).transpose(0, 1, 3, 2, 4)
    return y, kn, v, s_out


def kernel(x_prompt, x_sample, cache_attn_k, cache_attn_v, state_delta, c, c_ctx, w_mod, b_mod, norm_pre, norm_post,
           w_in, w_out, pool_w, pool_scale, q_norm, k_norm, conv_w, a_log, dt_bias, o_norm):
    B, L, _ = x_prompt.shape
    DB, DL, _ = x_sample.shape
    past = cache_attn_k.shape[2]

    conds = jnp.zeros((8, D_MODEL), F32).at[0].set(c_ctx).at[1:1 + DB].set(c)
    mod = _modulation(conds, w_mod, b_mod)
    rope_tabs = _rope_tables(DL)

    ctx_tiles = dict(tm=256, tq=256, tk=256, tc=256, bb=8, cs=4)
    lat_tiles = dict(tm=512, tq=256, tk=1024, tc=256, bb=4, cs=4)

    hp, hs = x_prompt, x_sample
    new_k, new_v, new_s = [], [], []
    w_in_all = _permute_w_in(w_in)
    w_out_all = w_out.astype(BF16)
    for l in range(DEPTH):
        blocks = jnp.zeros((N_DELTA_HEADS, HEAD_DIM, N_DELTA_HEADS, HEAD_DIM), F32)
        blocks = blocks.at[jnp.arange(4), :, jnp.arange(4), :].set(pool_w[l])
        qkn = jnp.zeros((8, LANES), F32).at[0].set(jnp.tile(q_norm[l], 2)).at[1].set(jnp.tile(k_norm[l], 2))
        lw = dict(
            gpre=norm_pre[l].reshape(1, D_MODEL),
            gpost=norm_post[l].reshape(1, D_MODEL),
            w_in=(w_in_all, l),
            w_out=(w_out_all, l),
            qkn=qkn,
            conv_w=conv_w[l],
            alog_row=_lane_row(a_log[l], 2 * N_DELTA_HEADS),
            dtb_row=_lane_row(dt_bias[l], 2 * N_DELTA_HEADS),
            pw_bd=blocks.reshape(D_POOL, D_POOL).astype(BF16),
            pscale=pool_scale[l].reshape(1, D_POOL),
            onorm_row=jnp.tile(o_norm[l], 2).reshape(1, LANES),
        )
        hp, k_l, v_l, s_l = _layer(hp, mod[l, 0:1].reshape(1, 1, 3 * D_MODEL), lw, None, None, ctx_tiles)
        ctx = dict(kv=_cached_kv_variants(cache_attn_k[:, l].reshape(DB, past, D_KV),
                                          cache_attn_v[:, l].reshape(DB, past, D_KV)),
                   state=state_delta[:, l].transpose(0, 1, 3, 2, 4).reshape(DB, 2, HEAD_DIM, D_DELTA))
        hs = _layer(hs, mod[l, 1:1 + DB].reshape(DB, 1, 3 * D_MODEL), lw, rope_tabs, ctx, lat_tiles)
        new_k.append(k_l.reshape(B, L, 2, HEAD_DIM))
        new_v.append(v_l.reshape(B, L, 2, HEAD_DIM))
        new_s.append(s_l)
    return (hp, hs, jnp.stack(new_k, axis=1), jnp.stack(new_v, axis=1), jnp.stack(new_s, axis=1))
```

```python
import functools

import jax
import jax.numpy as jnp
from jax import lax
from jax.experimental import pallas as pl
from jax.experimental.pallas import tpu as pltpu

F32 = jnp.float32
BF16 = jnp.bfloat16

D_MODEL = 1024
DEPTH = 2
GRID_W = 64
HEAD_DIM = 64
D_POOL = 256
D_ATTN = 512
D_DELTA = 256
D_KV = 128
N_Q_HEADS = 8
N_DELTA_HEADS = 4
POOL_WINDOWS = (2, 4, 8, 16)
CHUNK = 64
ROPE_THETA = 10000.0
EPS = 1e-6
LANES = 128
HALO = 8
MASK_VALUE = -1e30
LOG2E = 1.4426950408889634
Q_SCALE = HEAD_DIM ** -0.5 * LOG2E
SHIFT_LIMIT = 60.0

D_IN = 2832
OFF_UP = 0
OFF_GPOOL = 256
OFF_QKV = 512
OFF_GATTN = 1280
OFF_QKVD = 1792
OFF_TAIL = 2560
TAIL_GDELTA = 0
TAIL_BA = 256
D_TAIL = 384

VMEM_LIMIT = 56 * 1024 * 1024


def _cparams(*sem):
    return pltpu.CompilerParams(dimension_semantics=sem, vmem_limit_bytes=VMEM_LIMIT)


def _dot(a, b):
    return jnp.dot(a, b, preferred_element_type=F32)


def _dot_nt(a, b):
    return lax.dot_general(a, b, (((1,), (1,)), ((), ())), preferred_element_type=F32)


def _dot_tn(a, b):
    return lax.dot_general(a, b, (((0,), (0,)), ((), ())), preferred_element_type=F32)


def _split(x):
    hi = x.astype(BF16)
    return hi, (x - hi.astype(F32)).astype(BF16)


def _silu(x):
    return x * jax.nn.sigmoid(x)


def _softplus(x):
    return jnp.maximum(x, 0.0) + jnp.log1p(jnp.exp(-jnp.abs(x)))


def _kv_variants(k, k_sw, v, v_sw, lane):
    lo = lane < HEAD_DIM
    one_hi = (lane == HEAD_DIM).astype(k.dtype)
    one_lo = (lane == 0).astype(k.dtype)
    ones = jnp.ones_like(v)
    return ((jnp.where(lo, k, one_hi), jnp.where(lo, v, ones)),
            (jnp.where(lo, one_lo, k_sw), jnp.where(lo, ones, v_sw)),
            (jnp.where(lo, k_sw, one_hi), jnp.where(lo, v_sw, ones)),
            (jnp.where(lo, one_lo, k), jnp.where(lo, ones, v)))


def _half_rms_scale(x, lo_mask):
    sq = x * x
    ss_lo = jnp.sum(jnp.where(lo_mask, sq, 0.0), axis=-1, keepdims=True)
    ss_hi = jnp.sum(jnp.where(lo_mask, 0.0, sq), axis=-1, keepdims=True)
    r_lo = lax.rsqrt(ss_lo * (1.0 / HEAD_DIM) + EPS)
    r_hi = lax.rsqrt(ss_hi * (1.0 / HEAD_DIM) + EPS)
    return jnp.where(lo_mask, r_lo, r_hi)


def _mod_kernel(c_ref, w_ref, b_ref, o_ref):
    s = _silu(c_ref[...])
    o_ref[0] = _dot(s.astype(BF16), w_ref[0].astype(BF16)) + b_ref[0]


def _modulation(conds, w_mod, b_mod):
    tn = 768
    return pl.pallas_call(
        _mod_kernel,
        grid=(DEPTH, 3 * D_MODEL // tn),
        in_specs=[pl.BlockSpec((8, D_MODEL), lambda l, j: (0, 0)),
                  pl.BlockSpec((1, D_MODEL, tn), lambda l, j: (l, 0, j)),
                  pl.BlockSpec((1, 1, tn), lambda l, j: (l, 0, j))],
        out_specs=pl.BlockSpec((1, 8, tn), lambda l, j: (l, 0, j)),
        out_shape=jax.ShapeDtypeStruct((DEPTH, 8, 3 * D_MODEL), F32),
        compiler_params=_cparams("parallel", "parallel"),
        name="modulation",
    )(conds, w_mod, b_mod.reshape(DEPTH, 1, 3 * D_MODEL))


def _inproj_kernel(rope, x_ref, xprev_ref, xnext_ref, mod_ref, gpre_ref, w_ref, wtail_ref, qkn_ref, convw_ref, alog_ref,
                   dtb_ref, *rest):
    if rope:
        cos_ref, sin_ref, up_ref, gates_ref, q_ref, k4_ref, v4_ref, ksq_ref, act_ref, bg_ref, pad_ref = rest
    else:
        up_ref, gates_ref, q_ref, k4_ref, v4_ref, ksq_ref, act_ref, bg_ref, kn_ref, v_ref, pad_ref = rest
    i = pl.program_id(1)
    tm = x_ref.shape[1]
    mod = mod_ref[0]
    shift = mod[:, 0:D_MODEL]
    scale = mod[:, D_MODEL:2 * D_MODEL]

    def modulated_norm(x):
        ms = jnp.mean(x * x, axis=-1, keepdims=True)
        return ((x * lax.rsqrt(ms + EPS)) * gpre_ref[...] * (1.0 + scale) + shift).astype(BF16)

    hb = modulated_norm(x_ref[0])

    zqkv = _dot(hb, w_ref[:, OFF_QKV:OFF_QKV + D_ATTN + 2 * D_KV])
    zqk, v = zqkv[:, 0:D_ATTN + D_KV], zqkv[:, D_ATTN + D_KV:]
    wd = w_ref[:, OFF_QKVD:OFF_QKVD + 3 * D_DELTA]
    pad_ref[0:HALO, :] = jnp.where(i == 0, 0.0, _dot(modulated_norm(xprev_ref[0]), wd))
    pad_ref[HALO:HALO + tm, :] = _dot(hb, wd)
    pad_ref[HALO + tm:2 * HALO + tm, :] = jnp.where(i == pl.num_programs(1) - 1, 0.0, _dot(modulated_norm(xnext_ref[0]), wd))
    zg_pool = _dot(hb, w_ref[:, OFF_GPOOL:OFF_GPOOL + D_POOL])
    zg_attn = _dot(hb, w_ref[:, OFF_GATTN:OFF_GATTN + D_ATTN])
    zg_delta = _dot(hb, wtail_ref[:, TAIL_GDELTA:TAIL_GDELTA + D_DELTA])

    lane = lax.broadcasted_iota(jnp.int32, (tm, LANES), 1)
    lo_mask = lane < HEAD_DIM
    even = (lane % 2) == 0

    def head_norm_rope(blk, gain):
        y = blk * _half_rms_scale(blk, lo_mask) * gain
        if rope:
            swapped = jnp.where(even, pltpu.roll(y, LANES - 1, 1), pltpu.roll(y, 1, 1))
            return y, y * cos_ref[...] + swapped * sin_ref[...]
        return y, y

    for qb in range(D_ATTN // LANES):
        _, qr = head_norm_rope(zqk[:, qb * LANES:(qb + 1) * LANES], qkn_ref[0:1, :])
        q_ref[0, :, qb * LANES:(qb + 1) * LANES] = (qr * Q_SCALE).astype(BF16)
    gates_ref[0, :, 0:D_POOL] = _silu(zg_pool).astype(BF16)
    gates_ref[0, :, D_POOL:D_POOL + D_ATTN] = _silu(zg_attn).astype(BF16)
    gates_ref[0, :, D_POOL + D_ATTN:] = _silu(zg_delta).astype(BF16)
    up_ref[0] = _dot(hb, w_ref[:, OFF_UP:OFF_UP + D_POOL])
    ba = _dot(hb, wtail_ref[:, TAIL_BA:TAIL_BA + LANES])

    xp = pad_ref[...]
    rows = xp.shape[0]
    acc = None
    for j in range(4):
        shifted = xp if j == 2 else pltpu.roll(xp, (2 - j) % rows, 0)
        term = convw_ref[j:j + 1, :] * shifted[HALO:HALO + tm, :]
        acc = term if acc is None else acc + term
    act_ref[0] = _silu(acc)
    g = -jnp.exp(alog_ref[...]) * _softplus(ba + dtb_ref[...])
    bg_ref[0] = jnp.where(lane < 2 * N_DELTA_HEADS, jax.nn.sigmoid(ba), jnp.where(lane < 4 * N_DELTA_HEADS, g, 0.0))

    kn, kr = head_norm_rope(zqk[:, D_ATTN:D_ATTN + D_KV], qkn_ref[1:2, :])
    for a, (kx, vx) in enumerate(_kv_variants(kr, pltpu.roll(kr, HEAD_DIM, 1), v, pltpu.roll(v, HEAD_DIM, 1), lane)):
        k4_ref[0, :, a * LANES:(a + 1) * LANES] = kx.astype(BF16)
        v4_ref[0, :, a * LANES:(a + 1) * LANES] = vx.astype(BF16)
    ksq = kr * kr
    ksq = jnp.where(lo_mask, jnp.sum(jnp.where(lo_mask, ksq, 0.0), axis=-1, keepdims=True),
                    jnp.sum(jnp.where(lo_mask, 0.0, ksq), axis=-1, keepdims=True))
    ksq_ref[0, 0] = jnp.broadcast_to(jnp.max(ksq, axis=0, keepdims=True), (HALO, LANES))
    if not rope:
        kn_ref[0] = kn
        v_ref[0] = v


def _inproj(x, mod, gpre, w, wtail, layer, qkn, conv_w, alog_row, dtb_row, rope_tabs, tm):
    B, L, _ = x.shape
    rope = rope_tabs is not None
    shared_mod = mod.shape[0] == 1
    nb = tm // HALO
    last_blk = L // HALO - 1
    row = lambda b, i: (b, i, 0)
    const2 = lambda b, i: (0, 0)
    in_specs = [pl.BlockSpec((1, tm, D_MODEL), row),
                pl.BlockSpec((1, HALO, D_MODEL), lambda b, i: (b, jnp.maximum(i * nb - 1, 0), 0)),
                pl.BlockSpec((1, HALO, D_MODEL), lambda b, i: (b, jnp.minimum((i + 1) * nb, last_blk), 0)),
                pl.BlockSpec((1, 1, 3 * D_MODEL), (lambda b, i: (0, 0, 0)) if shared_mod else (lambda b, i: (b, 0, 0))),
                pl.BlockSpec((1, D_MODEL), const2),
                pl.BlockSpec((None, D_MODEL, D_IN), lambda b, i: (layer, 0, 0)),
                pl.BlockSpec((None, D_MODEL, D_TAIL), lambda b, i: (layer, 0, 0)),
                pl.BlockSpec((8, LANES), const2),
                pl.BlockSpec((4, 3 * D_DELTA), const2),
                pl.BlockSpec((1, LANES), const2),
                pl.BlockSpec((1, LANES), const2)]
    args = [x, x, x, mod, gpre, w, wtail, qkn, conv_w, alog_row, dtb_row]
    if rope:
        in_specs += [pl.BlockSpec((tm, LANES), lambda b, i: (i, 0))] * 2
        args += list(rope_tabs)
    widths = [(D_POOL, F32), (D_MODEL, BF16), (D_ATTN, BF16), (4 * LANES, BF16), (4 * LANES, BF16), None,
              (3 * D_DELTA, F32), (LANES, F32)]
    if not rope:
        widths += [(D_KV, F32), (D_KV, F32)]
    out_specs = [pl.BlockSpec((1, tm, wd[0]), row) if wd else pl.BlockSpec((1, 1, HALO, LANES), lambda b, i: (b, i, 0, 0))
                 for wd in widths]
    out_shape = [jax.ShapeDtypeStruct((B, L, wd[0]), wd[1]) if wd else jax.ShapeDtypeStruct((B, L // tm, HALO, LANES), F32)
                 for wd in widths]
    return pl.pallas_call(
        functools.partial(_inproj_kernel, rope),
        grid=(B, L // tm),
        in_specs=in_specs,
        out_specs=out_specs,
        out_shape=out_shape,
        scratch_shapes=[pltpu.VMEM((tm + 2 * HALO, 3 * D_DELTA), F32)],
        compiler_params=_cparams("parallel", "parallel"),
        name="inproj_rope" if rope else "inproj",
    )(*args)


_VARIANT_HEADS = ((0, 2), (1, 3), (4, 6), (5, 7))


def _attn_kernel(n_src, tk, q_ref, ksq_ref, *rest):
    kv_refs = rest[:2 * n_src]
    o_ref, acc_ref, m_ref = rest[2 * n_src:]
    tq = q_ref.shape[1]
    n_var = len(_VARIANT_HEADS)
    lane = lax.broadcasted_iota(jnp.int32, (tq, LANES), 1)
    lo_mask = lane < HEAD_DIM

    ksq = jnp.max(ksq_ref[0], axis=0)[0:1, :]
    lo_row = lo_mask[0:1, :]
    ksq_head = [jnp.max(jnp.where(lo_row, ksq, 0.0), axis=-1, keepdims=True),
                jnp.max(jnp.where(lo_row, 0.0, ksq), axis=-1, keepdims=True)]
    for j in range(1, n_src):
        for kv_head in range(2):
            kx = kv_refs[2 * j][0, :, 2 * kv_head * LANES:(2 * kv_head + 1) * LANES].astype(F32)
            sq = jnp.sum(jnp.where(lane[0:1, :] < HEAD_DIM, kx * kx, 0.0), axis=-1, keepdims=True)
            ksq_head[kv_head] = jnp.maximum(ksq_head[kv_head], jnp.max(sq, axis=0, keepdims=True))

    def own_half(h):
        blk = q_ref[0, :, (h // 2) * LANES:(h // 2 + 1) * LANES]
        return jnp.where(lo_mask if h % 2 == 0 else jnp.logical_not(lo_mask), blk, jnp.zeros_like(blk))

    q_shift, bound_max = [], None
    for heads in _VARIANT_HEADS:
        shifted = []
        for h in heads:
            qf = own_half(h).astype(F32)
            bound = jnp.sqrt(jnp.sum(qf * qf, axis=-1, keepdims=True) * ksq_head[h // 4])
            bias_lane = HEAD_DIM if h % 2 == 0 else 0
            shifted.append(jnp.where(lane == bias_lane, -bound, qf).astype(BF16))
            bound_max = bound if bound_max is None else jnp.maximum(bound_max, bound)
        q_shift.append(jnp.concatenate(shifted, axis=0))
    safe = jnp.max(bound_max) <= SHIFT_LIMIT

    cols = [slice(a * LANES, (a + 1) * LANES) for a in range(n_var)]
    chunks = []
    for j in range(n_src):
        k4_ref, v4_ref = kv_refs[2 * j], kv_refs[2 * j + 1]
        S = k4_ref.shape[1]
        size = min(tk, S)
        chunks += [(k4_ref, v4_ref, c * size, size) for c in range(S // size)]

    @pl.when(safe)
    def _():
        def scores(ch):
            k4_ref, _, start, size = ch
            return [_dot_nt(q_shift[a], k4_ref[0, start:start + size, cols[a]]) for a in range(n_var)]

        acc = [None] * n_var
        s_next = scores(chunks[0])
        for ci, (_, v4_ref, start, size) in enumerate(chunks):
            s = s_next
            if ci + 1 < len(chunks):
                s_next = scores(chunks[ci + 1])
            for a in range(n_var):
                pv = _dot(jnp.exp2(s[a]).astype(BF16), v4_ref[0, start:start + size, cols[a]])
                acc[a] = pv if acc[a] is None else acc[a] + pv
        for a in range(n_var):
            acc_ref[a] = acc[a]

    @pl.when(jnp.logical_not(safe))
    def _():
        m_ref[...] = jnp.full(m_ref.shape, MASK_VALUE, F32)
        acc_ref[...] = jnp.zeros(acc_ref.shape, F32)
        q_plain = [jnp.concatenate([own_half(h) for h in heads], axis=0) for heads in _VARIANT_HEADS]
        for k4_ref, v4_ref, start, size in chunks:
            for a in range(n_var):
                s = _dot_nt(q_plain[a], k4_ref[0, start:start + size, cols[a]])
                m_prev = m_ref[a]
                m_new = jnp.maximum(m_prev, jnp.max(s, axis=-1, keepdims=True))
                p = jnp.exp2(s - jnp.tile(m_new, (1, size // LANES)))
                m_ref[a] = m_new
                acc_ref[a] = jnp.exp2(m_prev - m_new) * acc_ref[a] + _dot(p.astype(BF16), v4_ref[0, start:start + size, cols[a]])

    def head_out(h):
        a = 2 * (h // 4) + h % 2
        r = _VARIANT_HEADS[a].index(h) * tq
        rows = acc_ref[a, r:r + tq, :]
        return rows / pltpu.roll(rows, HEAD_DIM, 1)

    for i in range(N_Q_HEADS // 2):
        o_ref[0, :, i * LANES:(i + 1) * LANES] = jnp.where(lo_mask, head_out(2 * i), head_out(2 * i + 1)).astype(BF16)


def _attention(q, ksq, sources, tq, tk):
    B, L, _ = q.shape
    n_var = len(_VARIANT_HEADS)
    in_specs = [pl.BlockSpec((1, tq, D_ATTN), lambda b, i: (b, i, 0)),
                pl.BlockSpec((1,) + ksq.shape[1:], lambda b, i: (b, 0, 0, 0))]
    args = [q, ksq]
    for k4, v4 in sources:
        S = k4.shape[1]
        in_specs += [pl.BlockSpec((1, S, n_var * LANES), lambda b, i: (b, 0, 0))] * 2
        args += [k4, v4]
    rows = N_Q_HEADS // n_var * tq
    return pl.pallas_call(
        functools.partial(_attn_kernel, len(sources), tk),
        grid=(B, L // tq),
        in_specs=in_specs,
        out_specs=pl.BlockSpec((1, tq, D_ATTN), lambda b, i: (b, i, 0)),
        out_shape=jax.ShapeDtypeStruct((B, L, D_ATTN), BF16),
        scratch_shapes=[pltpu.VMEM((n_var, rows, LANES), F32)] * 2,
        compiler_params=_cparams("parallel", "parallel"),
        name="attention",
    )(*args)


def _block_diag(x, bd_mask):
    return jnp.where(bd_mask, jnp.concatenate([x] * N_DELTA_HEADS, axis=0), jnp.zeros((), x.dtype))


def _dot_split_bd(a, b, bd_mask):
    (ah, al), (bh, bl) = a, b
    m = ah.shape[0]
    both = _dot(jnp.concatenate([ah, al], axis=0), _block_diag(bh, bd_mask))
    return both[0:m] + both[m:2 * m] + _dot(ah, _block_diag(bl, bd_mask))


def _delta_chunk_kernel(act_ref, bg_ref, gt_ref, pm_ref, bm_ref, ol_ref, e_ref):
    C, W, H = CHUNK, D_DELTA, N_DELTA_HEADS
    n_chunks = act_ref.shape[1] // C
    ri = lax.broadcasted_iota(jnp.int32, (C, W), 0)
    lane = lax.broadcasted_iota(jnp.int32, (C, W), 1)
    lj = lane % HEAD_DIM
    blk = lane // HEAD_DIM
    bd_mask = (lax.broadcasted_iota(jnp.int32, (W, W), 0) // HEAD_DIM) == (lax.broadcasted_iota(jnp.int32, (W, W), 1) // HEAD_DIM)
    diag = ri == lj
    eye = diag.astype(F32)

    def expand(cols):
        res = cols[H - 1]
        for h in range(H - 2, -1, -1):
            res = jnp.where(blk == h, cols[h], res)
        return res

    def seg_sum(x):
        return expand([jnp.sum(jnp.where(blk == h, x, 0.0), axis=-1, keepdims=True) for h in range(H)])

    def l2n(x):
        return x * lax.rsqrt(seg_sum(x * x) + EPS)

    def row_form(x):
        res = x[(H - 1) * C:H * C]
        for h in range(H - 2, -1, -1):
            res = jnp.where(blk == h, x[h * C:(h + 1) * C], res)
        return res

    units = []
    for c in range(n_chunks):
        rows = slice(c * C, (c + 1) * C)
        q = l2n(act_ref[0, rows, 0:W]) * (HEAD_DIM ** -0.5)
        k = l2n(act_ref[0, rows, W:2 * W])
        v = act_ref[0, rows, 2 * W:3 * W]
        kb = k.astype(BF16)
        kkqk = _dot_nt(jnp.concatenate([kb, q.astype(BF16)], axis=0), _block_diag(kb, bd_mask))
        kk, qk = kkqk[0:C], kkqk[C:2 * C]
        bg = bg_ref[0, rows, :]
        gt = gt_ref[0, c]
        for d in range(2):
            beta = expand([bg[:, d * H + h:d * H + h + 1] for h in range(H)])
            g = expand([bg[:, 2 * H + d * H + h:2 * H + d * H + h + 1] for h in range(H)])
            g_row = gt[d:d + 1, :]
            incl = (ri >= lj) if d == 0 else (ri <= lj)
            incl_t = (ri <= lj) if d == 0 else (ri >= lj)
            strict = (ri > lj) if d == 0 else (ri < lj)
            gc = seg_sum(jnp.where(incl, g_row, 0.0))
            gc_row = jnp.sum(jnp.where(incl_t, g, 0.0), axis=0, keepdims=True)
            g_tot = jnp.sum(g, axis=0, keepdims=True)
            decay = jnp.where(incl, jnp.exp(jnp.where(incl, gc - gc_row, 0.0)), 0.0)
            t = jnp.where(strict, -(beta * kk * decay), 0.0)
            units.append(dict(c=c, d=d, q=q, k=k, v=v, beta=beta, gc=gc, g_tot=g_tot,
                              attn=(qk * decay).astype(BF16), t=t, p=eye + t))

    for un in units:
        un["t"] = _dot_split_bd(_split(un["t"]), _split(un["t"]), bd_mask)
    for level in range(4):
        for un in units:
            tp = jnp.concatenate([un["t"], un["p"]], axis=0)
            if level < 3:
                both = _dot_split_bd(_split(tp), _split(un["t"]), bd_mask)
            else:
                both = _dot(tp.astype(BF16), _block_diag(un["t"].astype(BF16), bd_mask))
            un["t"] = both[0:C]
            un["p"] = un["p"] + both[C:2 * C]
    for un in units:
        un["p"] = (un["p"] + _dot(un["p"].astype(BF16), _block_diag(un["t"].astype(BF16), bd_mask))).astype(BF16)
    for un in units:
        un["egc"] = jnp.exp(un["gc"])
        un["u"] = _dot(un["p"], _block_diag((un["v"] * un["beta"]).astype(BF16), bd_mask)).astype(BF16)
        un["w"] = _dot(un["p"], _block_diag((un["k"] * (un["beta"] * un["egc"])).astype(BF16), bd_mask)).astype(BF16)
    for un in units:
        c, d = un["c"], un["d"]
        kdec = (un["k"] * jnp.exp(un["g_tot"] - un["gc"])).astype(BF16)
        bm_ref[0, d, c] = row_form(_dot_tn(kdec, un["u"])).astype(BF16)
        mp = row_form(_dot_tn(kdec, un["w"]))
        qt = un["q"] * un["egc"] - _dot(un["attn"], _block_diag(un["w"], bd_mask))
        ol_ref[0, d, c] = _dot(un["attn"], _block_diag(un["u"], bd_mask)).astype(BF16)
        pm_ref[0, d, c] = jnp.concatenate([mp, qt], axis=0).astype(BF16)
        e_ref[0, d, c] = jnp.broadcast_to(jnp.exp(un["g_tot"]), (HALO, W))


def _delta_chunks(act, bg, gt, tc):
    B, L, _ = act.shape
    N = L // CHUNK
    cb = tc // CHUNK
    W = D_DELTA
    ospec = lambda r: pl.BlockSpec((1, 2, cb, r, W), lambda b, i: (b, 0, i, 0, 0))
    return pl.pallas_call(
        _delta_chunk_kernel,
        grid=(B, L // tc),
        in_specs=[pl.BlockSpec((1, tc, 3 * W), lambda b, i: (b, i, 0)),
                  pl.BlockSpec((1, tc, LANES), lambda b, i: (b, i, 0)),
                  pl.BlockSpec((1, cb, 2, W), lambda b, i: (b, i, 0, 0))],
        out_specs=[ospec(2 * CHUNK), ospec(CHUNK), ospec(CHUNK), ospec(HALO)],
        out_shape=[jax.ShapeDtypeStruct((B, 2, N, 2 * CHUNK, W), BF16),
                   jax.ShapeDtypeStruct((B, 2, N, CHUNK, W), BF16),
                   jax.ShapeDtypeStruct((B, 2, N, CHUNK, W), BF16),
                   jax.ShapeDtypeStruct((B, 2, N, HALO, W), F32)],
        compiler_params=_cparams("parallel", "parallel"),
        name="delta_chunks",
    )(act, bg, gt)


def _delta_scan_kernel(pmf_ref, bmf_ref, olf_ref, ef_ref, pmb_ref, bmb_ref, olb_ref, eb_ref, s0_ref,
                       of_ref, ob_ref, sout_ref, s_ref):
    n = pl.program_id(1)
    bb = s_ref.shape[0]
    C, W = CHUNK, D_DELTA
    bd_mask = (lax.broadcasted_iota(jnp.int32, (W, W), 0) // HEAD_DIM) == (lax.broadcasted_iota(jnp.int32, (W, W), 1) // HEAD_DIM)

    @pl.when(n == 0)
    def _():
        s_ref[...] = s0_ref[...]

    dirs = ((pmf_ref, bmf_ref, olf_ref, ef_ref, of_ref), (pmb_ref, bmb_ref, olb_ref, eb_ref, ob_ref))
    cs = pmf_ref.shape[2]
    chains = [(b, d) for b in range(bb) for d in range(2)]
    s = [s_ref[b, d] for b, d in chains]
    for step in range(cs):
        idx = [step if d == 0 else cs - 1 - step for _, d in chains]
        r = [_dot(dirs[d][0][b, 0, idx[i]], _block_diag(s[i].astype(BF16), bd_mask)) for i, (b, d) in enumerate(chains)]
        for i, (b, d) in enumerate(chains):
            _, bm_ref, ol_ref, e_ref, o_ref = dirs[d]
            s[i] = e_ref[b, 0, idx[i]][0:1, :] * s[i] + bm_ref[b, 0, idx[i]].astype(F32) - r[i][0:C]
            o_ref[b, idx[i] * C:(idx[i] + 1) * C, :] = (r[i][C:2 * C] + ol_ref[b, 0, idx[i]].astype(F32)).astype(BF16)
    for i, (b, d) in enumerate(chains):
        s_ref[b, d] = s[i]

    @pl.when(n == pl.num_programs(1) - 1)
    def _():
        sout_ref[...] = s_ref[...]


def _delta_scan(pm, bm, ol, e, s0, bb, cs):
    B, _, N, _, W = pm.shape
    L = N * CHUNK
    steps = N // cs

    def spec(r, d):
        if d == 0:
            return pl.BlockSpec((bb, 1, cs, r, W), lambda b, n: (b, 0, n, 0, 0))
        return pl.BlockSpec((bb, 1, cs, r, W), lambda b, n: (b, 1, steps - 1 - n, 0, 0))

    in_specs = []
    for d in range(2):
        in_specs += [spec(2 * CHUNK, d), spec(CHUNK, d), spec(CHUNK, d), spec(HALO, d)]
    in_specs.append(pl.BlockSpec((bb, 2, CHUNK, W), lambda b, n: (b, 0, 0, 0)))
    return pl.pallas_call(
        _delta_scan_kernel,
        grid=(B // bb, steps),
        in_specs=in_specs,
        out_specs=[pl.BlockSpec((bb, cs * CHUNK, W), lambda b, n: (b, n, 0)),
                   pl.BlockSpec((bb, cs * CHUNK, W), lambda b, n: (b, steps - 1 - n, 0)),
                   pl.BlockSpec((bb, 2, CHUNK, W), lambda b, n: (b, 0, 0, 0))],
        out_shape=[jax.ShapeDtypeStruct((B, L, W), BF16),
                   jax.ShapeDtypeStruct((B, L, W), BF16),
                   jax.ShapeDtypeStruct((B, 2, CHUNK, W), F32)],
        scratch_shapes=[pltpu.VMEM((bb, 2, CHUNK, W), F32)],
        compiler_params=_cparams("parallel", "arbitrary"),
        name="delta_scan",
    )(pm, bm, ol, e, pm, bm, ol, e, s0)


def _merge_kernel(x_ref, mod_ref, up_ref, upp_ref, upn_ref, cnt_ref, gates_ref, oattn_ref, of_ref, ob_ref,
                  pw_ref, pscale_ref, onorm_ref, wout_ref, gpost_ref, o_ref, pad_ref):
    i = pl.program_id(1)
    tm = x_ref.shape[1]
    first = i == 0
    last = i == pl.num_programs(1) - 1
    gates = gates_ref[0].astype(F32)
    y = _dot((gates[:, D_POOL:D_POOL + D_ATTN] * oattn_ref[0].astype(F32)).astype(BF16),
             wout_ref[D_POOL:D_POOL + D_ATTN, :])
    pad_ref[0:HALO, :] = jnp.where(first, 0.0, upp_ref[0])
    pad_ref[HALO:HALO + tm, :] = up_ref[0]
    pad_ref[HALO + tm:2 * HALO + tm, :] = jnp.where(last, 0.0, upn_ref[0])

    def window_sum(col, lo, hi):
        padded = pad_ref[:, col]
        rows = padded.shape[0]
        acc = None
        for j in range(lo, hi):
            term = (padded if j == 0 else pltpu.roll(padded, (-j) % rows, 0))[HALO:HALO + tm, :]
            acc = term if acc is None else acc + term
        return acc

    lane = lax.broadcasted_iota(jnp.int32, (tm, LANES), 1)
    lo_mask = lane < HEAD_DIM

    pooled = []
    for col_blk, (w_lo, w_hi) in enumerate(((2, 4), (8, 16))):
        col = slice(col_blk * LANES, (col_blk + 1) * LANES)
        s_lo = window_sum(col, -(w_lo // 2), w_lo // 2)
        s_hi = s_lo + window_sum(col, -(w_hi // 2), -(w_lo // 2)) + window_sum(col, w_lo // 2, w_hi // 2)
        mean = jnp.where(lo_mask, s_lo, s_hi) / cnt_ref[:, col].astype(F32)
        pooled.append(mean - up_ref[0, :, col])
    pooled = jnp.concatenate(pooled, axis=1)
    o_pool = _dot(pooled.astype(BF16), pw_ref[...]) * pscale_ref[...]

    od = of_ref[0].astype(F32) + ob_ref[0].astype(F32)
    odn = []
    for j in range(D_DELTA // LANES):
        blk = od[:, j * LANES:(j + 1) * LANES]
        odn.append(blk * _half_rms_scale(blk, lo_mask) * onorm_ref[...])
    o_delta = jnp.concatenate(odn, axis=1)

    y = y + _dot((gates[:, 0:D_POOL] * o_pool).astype(BF16), wout_ref[0:D_POOL, :])
    y = y + _dot((gates[:, D_POOL + D_ATTN:] * o_delta).astype(BF16), wout_ref[D_POOL + D_ATTN:, :])
    ms = jnp.mean(y * y, axis=-1, keepdims=True)
    yn = (y * lax.rsqrt(ms + EPS)) * gpost_ref[...]
    gate = mod_ref[0][:, 2 * D_MODEL:3 * D_MODEL]
    o_ref[0] = x_ref[0] + gate * yn


def _window_counts(seq_len):
    t = jnp.arange(seq_len)[:, None]
    w = jnp.repeat(jnp.array(POOL_WINDOWS), D_POOL // len(POOL_WINDOWS))[None, :]
    return (jnp.minimum(t - w // 2 + w, seq_len) - jnp.maximum(t - w // 2, 0)).astype(BF16)


def _merge(x, mod, up, gates, oattn, o_f, o_b, pw_bd, pscale, onorm_row, wout, layer, gpost, tm):
    B, L, _ = x.shape
    nb = tm // HALO
    last_blk = L // HALO - 1
    shared_mod = mod.shape[0] == 1
    row = lambda b, i: (b, i, 0)
    const2 = lambda b, i: (0, 0)
    return pl.pallas_call(
        _merge_kernel,
        grid=(B, L // tm),
        in_specs=[pl.BlockSpec((1, tm, D_MODEL), row),
                  pl.BlockSpec((1, 1, 3 * D_MODEL), (lambda b, i: (0, 0, 0)) if shared_mod else (lambda b, i: (b, 0, 0))),
                  pl.BlockSpec((1, tm, D_POOL), row),
                  pl.BlockSpec((1, HALO, D_POOL), lambda b, i: (b, jnp.maximum(i * nb - 1, 0), 0)),
                  pl.BlockSpec((1, HALO, D_POOL), lambda b, i: (b, jnp.minimum((i + 1) * nb, last_blk), 0)),
                  pl.BlockSpec((tm, D_POOL), lambda b, i: (i, 0)),
                  pl.BlockSpec((1, tm, D_MODEL), row),
                  pl.BlockSpec((1, tm, D_ATTN), row),
                  pl.BlockSpec((1, tm, D_DELTA), row),
                  pl.BlockSpec((1, tm, D_DELTA), row),
                  pl.BlockSpec((D_POOL, D_POOL), const2),
                  pl.BlockSpec((1, D_POOL), const2),
                  pl.BlockSpec((1, LANES), const2),
                  pl.BlockSpec((None, D_MODEL, D_MODEL), lambda b, i: (layer, 0, 0)),
                  pl.BlockSpec((1, D_MODEL), const2)],
        out_specs=pl.BlockSpec((1, tm, D_MODEL), row),
        out_shape=jax.ShapeDtypeStruct((B, L, D_MODEL), F32),
        scratch_shapes=[pltpu.VMEM((tm + 2 * HALO, D_POOL), F32)],
        compiler_params=_cparams("parallel", "parallel"),
        name="merge",
    )(x, mod, up, up, up, _window_counts(L), gates, oattn, o_f, o_b, pw_bd, pscale, onorm_row, wout, gpost)


def _rope_tables(num_tokens):
    rows = num_tokens // GRID_W
    row = jnp.repeat(jnp.arange(rows, dtype=F32), GRID_W)
    col = (jnp.arange(rows * GRID_W) % GRID_W).astype(F32)
    axis_dim = HEAD_DIM // 2
    inv = ROPE_THETA ** (-jnp.arange(0, axis_dim, 2, dtype=F32) / axis_dim)
    ang = jnp.concatenate([row[:, None] * inv, col[:, None] * inv], axis=-1)
    cos = jnp.repeat(jnp.cos(ang), 2, axis=-1)
    sin = jnp.repeat(jnp.sin(ang), 2, axis=-1)
    sign = jnp.tile(jnp.array([-1.0, 1.0], F32), HEAD_DIM // 2)
    return jnp.tile(cos, (1, 2)), jnp.tile(sin * sign, (1, 2))


def _w_in_tail(w):
    ba = w[..., OFF_TAIL:OFF_TAIL + 4 * N_DELTA_HEADS]
    g_delta = w[..., OFF_TAIL + 4 * N_DELTA_HEADS:]
    pad = jnp.zeros(w.shape[:-1] + (LANES - 4 * N_DELTA_HEADS,), w.dtype)
    return jnp.concatenate([g_delta, ba, pad], axis=-1).astype(BF16)


def _lane_row(vec, offset):
    return jnp.zeros((1, LANES), F32).at[0, offset:offset + vec.size].set(vec.reshape(-1))


def _cached_kv_variants(k, v):
    lane = jnp.arange(LANES)
    pairs = _kv_variants(k, jnp.roll(k, HEAD_DIM, axis=-1), v, jnp.roll(v, HEAD_DIM, axis=-1), lane)
    return (jnp.concatenate([p[0] for p in pairs], axis=-1).astype(BF16),
            jnp.concatenate([p[1] for p in pairs], axis=-1).astype(BF16))


def _layer(x, mod, lw, rope_tabs, ctx, tiles):
    B, L, _ = x.shape
    latent = ctx is not None
    outs = _inproj(x, mod, lw["gpre"], lw["w_in"], lw["w_in_tail"], lw["layer"], lw["qkn"], lw["conv_w"],
                   lw["alog_row"], lw["dtb_row"], rope_tabs, tiles["tm"])
    up, gates, q, k4, v4, ksq, act, bg = outs[:8]
    sources = [(k4, v4)]
    if latent:
        sources.append(ctx["kv"])
    oattn = _attention(q, ksq, sources, tiles["tq"], tiles["tk"])

    gt = bg[:, :, 2 * N_DELTA_HEADS:4 * N_DELTA_HEADS].reshape(B, L // CHUNK, CHUNK, 2, N_DELTA_HEADS)
    gt = gt.transpose(0, 1, 3, 4, 2).reshape(B, L // CHUNK, 2, D_DELTA)
    pm, bm, ol, e = _delta_chunks(act, bg, gt, tiles["tc"])
    s0 = ctx["state"] if latent else jnp.zeros((B, 2, HEAD_DIM, D_DELTA), F32)
    o_f, o_b, s_out = _delta_scan(pm, bm, ol, e, s0, tiles["bb"], tiles["cs"])

    y = _merge(x, mod, up, gates, oattn, o_f, o_b, lw["pw_bd"], lw["pscale"], lw["onorm_row"],
               lw["w_out"], lw["layer"], lw["gpost"], tiles["tm"])
    if latent:
        return y
    kn, v = outs[8], outs[9]
    s_out = s_out.reshape(B, 2, HEAD_DIM, N_DELTA_HEADS, HEAD_DIM).transpose(0, 1, 3, 2, 4)
    return y, kn, v, s_out


def kernel(x_prompt, x_sample, cache_attn_k, cache_attn_v, state_delta, c, c_ctx, w_mod, b_mod, norm_pre, norm_post,
           w_in, w_out, pool_w, pool_scale, q_norm, k_norm, conv_w, a_log, dt_bias, o_norm):
    B, L, _ = x_prompt.shape
    DB, DL, _ = x_sample.shape
    past = cache_attn_k.shape[2]

    conds = jnp.zeros((8, D_MODEL), F32).at[0].set(c_ctx).at[1:1 + DB].set(c)
    mod = _modulation(conds, w_mod, b_mod)
    rope_tabs = _rope_tables(DL)

    ctx_tiles = dict(tm=256, tq=256, tk=256, tc=256, bb=8, cs=4)
    lat_tiles = dict(tm=512, tq=256, tk=1024, tc=256, bb=4, cs=4)

    hp, hs = x_prompt, x_sample
    new_k, new_v, new_s = [], [], []
    w_in_all = w_in.astype(BF16)
    w_in_tail = _w_in_tail(w_in)
    w_out_all = w_out.astype(BF16)
    for l in range(DEPTH):
        blocks = jnp.zeros((N_DELTA_HEADS, HEAD_DIM, N_DELTA_HEADS, HEAD_DIM), F32)
        blocks = blocks.at[jnp.arange(4), :, jnp.arange(4), :].set(pool_w[l])
        qkn = jnp.zeros((8, LANES), F32).at[0].set(jnp.tile(q_norm[l], 2)).at[1].set(jnp.tile(k_norm[l], 2))
        lw = dict(
            layer=l,
            gpre=norm_pre[l].reshape(1, D_MODEL),
            gpost=norm_post[l].reshape(1, D_MODEL),
            w_in=w_in_all,
            w_in_tail=w_in_tail,
            w_out=w_out_all,
            qkn=qkn,
            conv_w=conv_w[l],
            alog_row=_lane_row(a_log[l], 2 * N_DELTA_HEADS),
            dtb_row=_lane_row(dt_bias[l], 2 * N_DELTA_HEADS),
            pw_bd=blocks.reshape(D_POOL, D_POOL).astype(BF16),
            pscale=pool_scale[l].reshape(1, D_POOL),
            onorm_row=jnp.tile(o_norm[l], 2).reshape(1, LANES),
        )
        hp, k_l, v_l, s_l = _layer(hp, mod[l, 0:1].reshape(1, 1, 3 * D_MODEL), lw, None, None, ctx_tiles)
        ctx = dict(kv=_cached_kv_variants(cache_attn_k[:, l].reshape(DB, past, D_KV),
                                          cache_attn_v[:, l].reshape(DB, past, D_KV)),
                   state=state_delta[:, l].transpose(0, 1, 3, 2, 4).reshape(DB, 2, HEAD_DIM, D_DELTA))
        hs = _layer(hs, mod[l, 1:1 + DB].reshape(DB, 1, 3 * D_MODEL), lw, rope_tabs, ctx, lat_tiles)
        new_k.append(k_l.reshape(B, L, 2, HEAD_DIM))
        new_v.append(v_l.reshape(B, L, 2, HEAD_DIM))
        new_s.append(s_l)
    return (hp, hs, jnp.stack(new_k, axis=1), jnp.stack(new_v, axis=1), jnp.stack(new_s, axis=1))
```

```python
import functools

import jax
import jax.numpy as jnp
from jax import lax
from jax.experimental import pallas as pl
from jax.experimental.pallas import tpu as pltpu

F32 = jnp.float32
BF16 = jnp.bfloat16

D_MODEL = 1024
DEPTH = 2
GRID_W = 64
HEAD_DIM = 64
D_POOL = 256
D_ATTN = 512
D_DELTA = 256
D_KV = 128
N_Q_HEADS = 8
N_DELTA_HEADS = 4
POOL_WINDOWS = (2, 4, 8, 16)
CHUNK = 64
ROPE_THETA = 10000.0
EPS = 1e-6
LANES = 128
HALO = 8
MASK_VALUE = -1e30
LOG2E = 1.4426950408889634
Q_SCALE = HEAD_DIM ** -0.5 * LOG2E
SHIFT_LIMIT = 60.0

D_IN = 2832
OFF_UP = 0
OFF_GPOOL = 256
OFF_QKV = 512
OFF_GATTN = 1280
OFF_QKVD = 1792
OFF_TAIL = 2560
TAIL_GDELTA = 0
TAIL_BA = 256
D_TAIL = 384

VMEM_LIMIT = 56 * 1024 * 1024


def _cparams(*sem):
    return pltpu.CompilerParams(dimension_semantics=sem, vmem_limit_bytes=VMEM_LIMIT)


def _dot(a, b):
    return jnp.dot(a, b, preferred_element_type=F32)


def _dot_nt(a, b):
    return lax.dot_general(a, b, (((1,), (1,)), ((), ())), preferred_element_type=F32)


def _dot_tn(a, b):
    return lax.dot_general(a, b, (((0,), (0,)), ((), ())), preferred_element_type=F32)


def _split(x):
    hi = x.astype(BF16)
    return hi, (x - hi.astype(F32)).astype(BF16)


def _silu(x):
    return x * jax.nn.sigmoid(x)


def _softplus(x):
    return jnp.maximum(x, 0.0) + jnp.log1p(jnp.exp(-jnp.abs(x)))


def _kv_variants(k, k_sw, v, v_sw, lane):
    lo = lane < HEAD_DIM
    one_hi = (lane == HEAD_DIM).astype(k.dtype)
    one_lo = (lane == 0).astype(k.dtype)
    ones = jnp.ones_like(v)
    return ((jnp.where(lo, k, one_hi), jnp.where(lo, v, ones)),
            (jnp.where(lo, one_lo, k_sw), jnp.where(lo, ones, v_sw)),
            (jnp.where(lo, k_sw, one_hi), jnp.where(lo, v_sw, ones)),
            (jnp.where(lo, one_lo, k), jnp.where(lo, ones, v)))


def _half_rms_scale(x, lo_mask):
    sq = x * x
    ss_lo = jnp.sum(jnp.where(lo_mask, sq, 0.0), axis=-1, keepdims=True)
    ss_hi = jnp.sum(jnp.where(lo_mask, 0.0, sq), axis=-1, keepdims=True)
    r_lo = lax.rsqrt(ss_lo * (1.0 / HEAD_DIM) + EPS)
    r_hi = lax.rsqrt(ss_hi * (1.0 / HEAD_DIM) + EPS)
    return jnp.where(lo_mask, r_lo, r_hi)


def _mod_kernel(c_ref, w_ref, b_ref, o_ref):
    s = _silu(c_ref[...])
    o_ref[0] = _dot(s.astype(BF16), w_ref[0].astype(BF16)) + b_ref[0]


def _modulation(conds, w_mod, b_mod):
    tn = 768
    return pl.pallas_call(
        _mod_kernel,
        grid=(DEPTH, 3 * D_MODEL // tn),
        in_specs=[pl.BlockSpec((8, D_MODEL), lambda l, j: (0, 0)),
                  pl.BlockSpec((1, D_MODEL, tn), lambda l, j: (l, 0, j)),
                  pl.BlockSpec((1, 1, tn), lambda l, j: (l, 0, j))],
        out_specs=pl.BlockSpec((1, 8, tn), lambda l, j: (l, 0, j)),
        out_shape=jax.ShapeDtypeStruct((DEPTH, 8, 3 * D_MODEL), F32),
        compiler_params=_cparams("parallel", "parallel"),
        name="modulation",
    )(conds, w_mod, b_mod.reshape(DEPTH, 1, 3 * D_MODEL))


def _inproj_kernel(rope, x_ref, xprev_ref, xnext_ref, mod_ref, gpre_ref, w_ref, wtail_ref, qkn_ref, convw_ref, alog_ref,
                   dtb_ref, *rest):
    if rope:
        cos_ref, sin_ref, up_ref, gates_ref, q_ref, k4_ref, v4_ref, ksq_ref, act_ref, bg_ref, pad_ref = rest
    else:
        up_ref, gates_ref, q_ref, k4_ref, v4_ref, ksq_ref, act_ref, bg_ref, kn_ref, v_ref, pad_ref = rest
    i = pl.program_id(1)
    tm = x_ref.shape[1]
    mod = mod_ref[0]
    shift = mod[:, 0:D_MODEL]
    scale = mod[:, D_MODEL:2 * D_MODEL]

    def modulated_norm(x):
        ms = jnp.mean(x * x, axis=-1, keepdims=True)
        return ((x * lax.rsqrt(ms + EPS)) * gpre_ref[...] * (1.0 + scale) + shift).astype(BF16)

    hb = modulated_norm(x_ref[0])

    zqkv = _dot(hb, w_ref[:, OFF_QKV:OFF_QKV + D_ATTN + 2 * D_KV])
    zqk, v = zqkv[:, 0:D_ATTN + D_KV], zqkv[:, D_ATTN + D_KV:]
    wd = w_ref[:, OFF_QKVD:OFF_QKVD + 3 * D_DELTA]
    pad_ref[0:HALO, :] = jnp.where(i == 0, 0.0, _dot(modulated_norm(xprev_ref[0]), wd))
    pad_ref[HALO:HALO + tm, :] = _dot(hb, wd)
    pad_ref[HALO + tm:2 * HALO + tm, :] = jnp.where(i == pl.num_programs(1) - 1, 0.0, _dot(modulated_norm(xnext_ref[0]), wd))
    zg_pool = _dot(hb, w_ref[:, OFF_GPOOL:OFF_GPOOL + D_POOL])
    zg_attn = _dot(hb, w_ref[:, OFF_GATTN:OFF_GATTN + D_ATTN])
    zg_delta = _dot(hb, wtail_ref[:, TAIL_GDELTA:TAIL_GDELTA + D_DELTA])

    lane = lax.broadcasted_iota(jnp.int32, (tm, LANES), 1)
    lo_mask = lane < HEAD_DIM
    even = (lane % 2) == 0

    def head_norm_rope(blk, gain):
        y = blk * _half_rms_scale(blk, lo_mask) * gain
        if rope:
            swapped = jnp.where(even, pltpu.roll(y, LANES - 1, 1), pltpu.roll(y, 1, 1))
            return y, y * cos_ref[...] + swapped * sin_ref[...]
        return y, y

    for qb in range(D_ATTN // LANES):
        _, qr = head_norm_rope(zqk[:, qb * LANES:(qb + 1) * LANES], qkn_ref[0:1, :])
        qs = qr * Q_SCALE
        q_ref[0, :, 2 * qb * LANES:(2 * qb + 1) * LANES] = jnp.where(lo_mask, qs, 0.0).astype(BF16)
        q_ref[0, :, (2 * qb + 1) * LANES:(2 * qb + 2) * LANES] = jnp.where(lo_mask, 0.0, qs).astype(BF16)
    gates_ref[0, :, 0:D_POOL] = _silu(zg_pool).astype(BF16)
    gates_ref[0, :, D_POOL:D_POOL + D_ATTN] = _silu(zg_attn).astype(BF16)
    gates_ref[0, :, D_POOL + D_ATTN:] = _silu(zg_delta).astype(BF16)
    up_ref[0] = _dot(hb, w_ref[:, OFF_UP:OFF_UP + D_POOL])
    ba = _dot(hb, wtail_ref[:, TAIL_BA:TAIL_BA + LANES])

    xp = pad_ref[...]
    rows = xp.shape[0]
    acc = None
    for j in range(4):
        shifted = xp if j == 2 else pltpu.roll(xp, (2 - j) % rows, 0)
        term = convw_ref[j:j + 1, :] * shifted[HALO:HALO + tm, :]
        acc = term if acc is None else acc + term
    act_ref[0] = _silu(acc)
    g = -jnp.exp(alog_ref[...]) * _softplus(ba + dtb_ref[...])
    bg_ref[0] = jnp.where(lane < 2 * N_DELTA_HEADS, jax.nn.sigmoid(ba), jnp.where(lane < 4 * N_DELTA_HEADS, g, 0.0))

    kn, kr = head_norm_rope(zqk[:, D_ATTN:D_ATTN + D_KV], qkn_ref[1:2, :])
    for a, (kx, vx) in enumerate(_kv_variants(kr, pltpu.roll(kr, HEAD_DIM, 1), v, pltpu.roll(v, HEAD_DIM, 1), lane)):
        k4_ref[0, :, a * LANES:(a + 1) * LANES] = kx.astype(BF16)
        v4_ref[0, :, a * LANES:(a + 1) * LANES] = vx.astype(BF16)
    ksq = kr * kr
    ksq = jnp.where(lo_mask, jnp.sum(jnp.where(lo_mask, ksq, 0.0), axis=-1, keepdims=True),
                    jnp.sum(jnp.where(lo_mask, 0.0, ksq), axis=-1, keepdims=True))
    ksq_ref[0, 0] = jnp.broadcast_to(jnp.max(ksq, axis=0, keepdims=True), (HALO, LANES))
    if not rope:
        kn_ref[0] = kn
        v_ref[0] = v


def _inproj(x, mod, gpre, w, wtail, layer, qkn, conv_w, alog_row, dtb_row, rope_tabs, tm):
    B, L, _ = x.shape
    rope = rope_tabs is not None
    shared_mod = mod.shape[0] == 1
    nb = tm // HALO
    last_blk = L // HALO - 1
    row = lambda b, i: (b, i, 0)
    const2 = lambda b, i: (0, 0)
    in_specs = [pl.BlockSpec((1, tm, D_MODEL), row),
                pl.BlockSpec((1, HALO, D_MODEL), lambda b, i: (b, jnp.maximum(i * nb - 1, 0), 0)),
                pl.BlockSpec((1, HALO, D_MODEL), lambda b, i: (b, jnp.minimum((i + 1) * nb, last_blk), 0)),
                pl.BlockSpec((1, 1, 3 * D_MODEL), (lambda b, i: (0, 0, 0)) if shared_mod else (lambda b, i: (b, 0, 0))),
                pl.BlockSpec((1, D_MODEL), const2),
                pl.BlockSpec((None, D_MODEL, D_IN), lambda b, i: (layer, 0, 0)),
                pl.BlockSpec((None, D_MODEL, D_TAIL), lambda b, i: (layer, 0, 0)),
                pl.BlockSpec((8, LANES), const2),
                pl.BlockSpec((4, 3 * D_DELTA), const2),
                pl.BlockSpec((1, LANES), const2),
                pl.BlockSpec((1, LANES), const2)]
    args = [x, x, x, mod, gpre, w, wtail, qkn, conv_w, alog_row, dtb_row]
    if rope:
        in_specs += [pl.BlockSpec((tm, LANES), lambda b, i: (i, 0))] * 2
        args += list(rope_tabs)
    widths = [(D_POOL, F32), (D_MODEL, BF16), (N_Q_HEADS * LANES, BF16), (4 * LANES, BF16), (4 * LANES, BF16), None,
              (3 * D_DELTA, F32), (LANES, F32)]
    if not rope:
        widths += [(D_KV, F32), (D_KV, F32)]
    out_specs = [pl.BlockSpec((1, tm, wd[0]), row) if wd else pl.BlockSpec((1, 1, HALO, LANES), lambda b, i: (b, i, 0, 0))
                 for wd in widths]
    out_shape = [jax.ShapeDtypeStruct((B, L, wd[0]), wd[1]) if wd else jax.ShapeDtypeStruct((B, L // tm, HALO, LANES), F32)
                 for wd in widths]
    return pl.pallas_call(
        functools.partial(_inproj_kernel, rope),
        grid=(B, L // tm),
        in_specs=in_specs,
        out_specs=out_specs,
        out_shape=out_shape,
        scratch_shapes=[pltpu.VMEM((tm + 2 * HALO, 3 * D_DELTA), F32)],
        compiler_params=_cparams("parallel", "parallel"),
        name="inproj_rope" if rope else "inproj",
    )(*args)


_VARIANT_HEADS = ((0, 2), (1, 3), (4, 6), (5, 7))


def _attn_kernel(n_src, tk, q_ref, qmax_ref, ksq_ref, *rest):
    kv_refs = rest[:2 * n_src]
    o_ref, acc_ref, m_ref = rest[2 * n_src:]
    tq = q_ref.shape[1]
    n_var = len(_VARIANT_HEADS)
    lane = lax.broadcasted_iota(jnp.int32, (tq, LANES), 1)
    lo_mask = lane < HEAD_DIM

    ksq = jnp.max(ksq_ref[0], axis=0)[0:1, :]
    lo_row = lo_mask[0:1, :]
    ksq_head = [jnp.max(jnp.where(lo_row, ksq, 0.0), axis=-1, keepdims=True),
                jnp.max(jnp.where(lo_row, 0.0, ksq), axis=-1, keepdims=True)]
    for j in range(1, n_src):
        for kv_head in range(2):
            kx = kv_refs[2 * j][0, :, 2 * kv_head * LANES:(2 * kv_head + 1) * LANES].astype(F32)
            sq = jnp.sum(jnp.where(lane[0:1, :] < HEAD_DIM, kx * kx, 0.0), axis=-1, keepdims=True)
            ksq_head[kv_head] = jnp.maximum(ksq_head[kv_head], jnp.max(sq, axis=0, keepdims=True))

    bound = [qmax_ref[0:1, 0:1] * jnp.sqrt(ksq_head[kv_head]) for kv_head in range(2)]
    safe = jnp.max(jnp.maximum(bound[0], bound[1])) <= SHIFT_LIMIT

    def head_block(h):
        return q_ref[0, :, h * LANES:(h + 1) * LANES]

    def shifted_block(h):
        blk = head_block(h)
        return jnp.where(lane == (HEAD_DIM if h % 2 == 0 else 0), (-bound[h // 4]).astype(BF16), blk)

    q_shift = [jnp.concatenate([shifted_block(h) for h in heads], axis=0) for heads in _VARIANT_HEADS]

    cols = [slice(a * LANES, (a + 1) * LANES) for a in range(n_var)]
    chunks = []
    for j in range(n_src):
        k4_ref, v4_ref = kv_refs[2 * j], kv_refs[2 * j + 1]
        S = k4_ref.shape[1]
        size = min(tk, S)
        chunks += [(k4_ref, v4_ref, c * size, size) for c in range(S // size)]

    @pl.when(safe)
    def _():
        def scores(ch):
            k4_ref, _, start, size = ch
            return [_dot_nt(q_shift[a], k4_ref[0, start:start + size, cols[a]]) for a in range(n_var)]

        acc = [None] * n_var
        s_next = scores(chunks[0])
        for ci, (_, v4_ref, start, size) in enumerate(chunks):
            s = s_next
            if ci + 1 < len(chunks):
                s_next = scores(chunks[ci + 1])
            for a in range(n_var):
                pv = _dot(jnp.exp2(s[a]).astype(BF16), v4_ref[0, start:start + size, cols[a]])
                acc[a] = pv if acc[a] is None else acc[a] + pv
        for a in range(n_var):
            acc_ref[a] = acc[a]

    @pl.when(jnp.logical_not(safe))
    def _():
        m_ref[...] = jnp.full(m_ref.shape, MASK_VALUE, F32)
        acc_ref[...] = jnp.zeros(acc_ref.shape, F32)
        q_plain = [jnp.concatenate([head_block(h) for h in heads], axis=0) for heads in _VARIANT_HEADS]
        for k4_ref, v4_ref, start, size in chunks:
            for a in range(n_var):
                s = _dot_nt(q_plain[a], k4_ref[0, start:start + size, cols[a]])
                m_prev = m_ref[a]
                m_new = jnp.maximum(m_prev, jnp.max(s, axis=-1, keepdims=True))
                p = jnp.exp2(s - jnp.tile(m_new, (1, size // LANES)))
                m_ref[a] = m_new
                acc_ref[a] = jnp.exp2(m_prev - m_new) * acc_ref[a] + _dot(p.astype(BF16), v4_ref[0, start:start + size, cols[a]])

    def head_out(h):
        a = 2 * (h // 4) + h % 2
        r = _VARIANT_HEADS[a].index(h) * tq
        rows = acc_ref[a, r:r + tq, :]
        return rows / pltpu.roll(rows, HEAD_DIM, 1)

    for i in range(N_Q_HEADS // 2):
        o_ref[0, :, i * LANES:(i + 1) * LANES] = jnp.where(lo_mask, head_out(2 * i), head_out(2 * i + 1)).astype(BF16)


def _attention(q, qmax, ksq, sources, tq, tk):
    B, L, _ = q.shape
    n_var = len(_VARIANT_HEADS)
    in_specs = [pl.BlockSpec((1, tq, N_Q_HEADS * LANES), lambda b, i: (b, i, 0)),
                pl.BlockSpec((1, LANES), lambda b, i: (0, 0)),
                pl.BlockSpec((1,) + ksq.shape[1:], lambda b, i: (b, 0, 0, 0))]
    args = [q, qmax, ksq]
    for k4, v4 in sources:
        S = k4.shape[1]
        in_specs += [pl.BlockSpec((1, S, n_var * LANES), lambda b, i: (b, 0, 0))] * 2
        args += [k4, v4]
    rows = N_Q_HEADS // n_var * tq
    return pl.pallas_call(
        functools.partial(_attn_kernel, len(sources), tk),
        grid=(B, L // tq),
        in_specs=in_specs,
        out_specs=pl.BlockSpec((1, tq, D_ATTN), lambda b, i: (b, i, 0)),
        out_shape=jax.ShapeDtypeStruct((B, L, D_ATTN), BF16),
        scratch_shapes=[pltpu.VMEM((n_var, rows, LANES), F32)] * 2,
        compiler_params=_cparams("parallel", "parallel"),
        name="attention",
    )(*args)


def _block_diag(x, bd_mask):
    return jnp.where(bd_mask, jnp.concatenate([x] * N_DELTA_HEADS, axis=0), jnp.zeros((), x.dtype))


def _dot_split_bd(a, b, bd_mask):
    (ah, al), (bh, bl) = a, b
    m = ah.shape[0]
    both = _dot(jnp.concatenate([ah, al], axis=0), _block_diag(bh, bd_mask))
    return both[0:m] + both[m:2 * m] + _dot(ah, _block_diag(bl, bd_mask))


def _delta_chunk_kernel(act_ref, bg_ref, gt_ref, pm_ref, bm_ref, ol_ref, e_ref):
    C, W, H = CHUNK, D_DELTA, N_DELTA_HEADS
    n_chunks = act_ref.shape[1] // C
    ri = lax.broadcasted_iota(jnp.int32, (C, W), 0)
    lane = lax.broadcasted_iota(jnp.int32, (C, W), 1)
    lj = lane % HEAD_DIM
    blk = lane // HEAD_DIM
    bd_mask = (lax.broadcasted_iota(jnp.int32, (W, W), 0) // HEAD_DIM) == (lax.broadcasted_iota(jnp.int32, (W, W), 1) // HEAD_DIM)
    diag = ri == lj
    eye = diag.astype(F32)

    def expand(cols):
        res = cols[H - 1]
        for h in range(H - 2, -1, -1):
            res = jnp.where(blk == h, cols[h], res)
        return res

    def seg_sum(x):
        return expand([jnp.sum(jnp.where(blk == h, x, 0.0), axis=-1, keepdims=True) for h in range(H)])

    def l2n(x):
        return x * lax.rsqrt(seg_sum(x * x) + EPS)

    def row_form(x):
        res = x[(H - 1) * C:H * C]
        for h in range(H - 2, -1, -1):
            res = jnp.where(blk == h, x[h * C:(h + 1) * C], res)
        return res

    units = []
    for c in range(n_chunks):
        rows = slice(c * C, (c + 1) * C)
        q = l2n(act_ref[0, rows, 0:W]) * (HEAD_DIM ** -0.5)
        k = l2n(act_ref[0, rows, W:2 * W])
        v = act_ref[0, rows, 2 * W:3 * W]
        kb = k.astype(BF16)
        kkqk = _dot_nt(jnp.concatenate([kb, q.astype(BF16)], axis=0), _block_diag(kb, bd_mask))
        kk, qk = kkqk[0:C], kkqk[C:2 * C]
        bg = bg_ref[0, rows, :]
        gt = gt_ref[0, c]
        for d in range(2):
            beta = expand([bg[:, d * H + h:d * H + h + 1] for h in range(H)])
            g = expand([bg[:, 2 * H + d * H + h:2 * H + d * H + h + 1] for h in range(H)])
            g_row = gt[d:d + 1, :]
            incl = (ri >= lj) if d == 0 else (ri <= lj)
            incl_t = (ri <= lj) if d == 0 else (ri >= lj)
            strict = (ri > lj) if d == 0 else (ri < lj)
            gc = seg_sum(jnp.where(incl, g_row, 0.0))
            gc_row = jnp.sum(jnp.where(incl_t, g, 0.0), axis=0, keepdims=True)
            g_tot = jnp.sum(g, axis=0, keepdims=True)
            decay = jnp.where(incl, jnp.exp(jnp.where(incl, gc - gc_row, 0.0)), 0.0)
            t = jnp.where(strict, -(beta * kk * decay), 0.0)
            units.append(dict(c=c, d=d, q=q, k=k, v=v, beta=beta, gc=gc, g_tot=g_tot,
                              attn=(qk * decay).astype(BF16), t=t, p=eye + t))

    for un in units:
        un["t"] = _dot_split_bd(_split(un["t"]), _split(un["t"]), bd_mask)
    for level in range(4):
        for un in units:
            tp = jnp.concatenate([un["t"], un["p"]], axis=0)
            if level < 3:
                both = _dot_split_bd(_split(tp), _split(un["t"]), bd_mask)
            else:
                both = _dot(tp.astype(BF16), _block_diag(un["t"].astype(BF16), bd_mask))
            un["t"] = both[0:C]
            un["p"] = un["p"] + both[C:2 * C]
    for un in units:
        un["p"] = (un["p"] + _dot(un["p"].astype(BF16), _block_diag(un["t"].astype(BF16), bd_mask))).astype(BF16)
    for un in units:
        un["egc"] = jnp.exp(un["gc"])
        un["u"] = _dot(un["p"], _block_diag((un["v"] * un["beta"]).astype(BF16), bd_mask)).astype(BF16)
        un["w"] = _dot(un["p"], _block_diag((un["k"] * (un["beta"] * un["egc"])).astype(BF16), bd_mask)).astype(BF16)
    for un in units:
        c, d = un["c"], un["d"]
        kdec = (un["k"] * jnp.exp(un["g_tot"] - un["gc"])).astype(BF16)
        bm_ref[0, d, c] = row_form(_dot_tn(kdec, un["u"])).astype(BF16)
        mp = row_form(_dot_tn(kdec, un["w"]))
        qt = un["q"] * un["egc"] - _dot(un["attn"], _block_diag(un["w"], bd_mask))
        ol_ref[0, d, c] = _dot(un["attn"], _block_diag(un["u"], bd_mask)).astype(BF16)
        pm_ref[0, d, c] = jnp.concatenate([mp, qt], axis=0).astype(BF16)
        e_ref[0, d, c] = jnp.broadcast_to(jnp.exp(un["g_tot"]), (HALO, W))


def _delta_chunks(act, bg, gt, tc):
    B, L, _ = act.shape
    N = L // CHUNK
    cb = tc // CHUNK
    W = D_DELTA
    ospec = lambda r: pl.BlockSpec((1, 2, cb, r, W), lambda b, i: (b, 0, i, 0, 0))
    return pl.pallas_call(
        _delta_chunk_kernel,
        grid=(B, L // tc),
        in_specs=[pl.BlockSpec((1, tc, 3 * W), lambda b, i: (b, i, 0)),
                  pl.BlockSpec((1, tc, LANES), lambda b, i: (b, i, 0)),
                  pl.BlockSpec((1, cb, 2, W), lambda b, i: (b, i, 0, 0))],
        out_specs=[ospec(2 * CHUNK), ospec(CHUNK), ospec(CHUNK), ospec(HALO)],
        out_shape=[jax.ShapeDtypeStruct((B, 2, N, 2 * CHUNK, W), BF16),
                   jax.ShapeDtypeStruct((B, 2, N, CHUNK, W), BF16),
                   jax.ShapeDtypeStruct((B, 2, N, CHUNK, W), BF16),
                   jax.ShapeDtypeStruct((B, 2, N, HALO, W), F32)],
        compiler_params=_cparams("parallel", "parallel"),
        name="delta_chunks",
    )(act, bg, gt)


def _delta_scan_kernel(pmf_ref, bmf_ref, olf_ref, ef_ref, pmb_ref, bmb_ref, olb_ref, eb_ref, s0_ref,
                       of_ref, ob_ref, sout_ref, s_ref):
    n = pl.program_id(1)
    bb = s_ref.shape[0]
    C, W = CHUNK, D_DELTA
    bd_mask = (lax.broadcasted_iota(jnp.int32, (W, W), 0) // HEAD_DIM) == (lax.broadcasted_iota(jnp.int32, (W, W), 1) // HEAD_DIM)

    @pl.when(n == 0)
    def _():
        s_ref[...] = s0_ref[...]

    dirs = ((pmf_ref, bmf_ref, olf_ref, ef_ref, of_ref), (pmb_ref, bmb_ref, olb_ref, eb_ref, ob_ref))
    cs = pmf_ref.shape[2]
    chains = [(b, d) for b in range(bb) for d in range(2)]
    s = [s_ref[b, d] for b, d in chains]
    for step in range(cs):
        idx = [step if d == 0 else cs - 1 - step for _, d in chains]
        r = [_dot(dirs[d][0][b, 0, idx[i]], _block_diag(s[i].astype(BF16), bd_mask)) for i, (b, d) in enumerate(chains)]
        for i, (b, d) in enumerate(chains):
            _, bm_ref, ol_ref, e_ref, o_ref = dirs[d]
            s[i] = e_ref[b, 0, idx[i]][0:1, :] * s[i] + bm_ref[b, 0, idx[i]].astype(F32) - r[i][0:C]
            o_ref[b, idx[i] * C:(idx[i] + 1) * C, :] = (r[i][C:2 * C] + ol_ref[b, 0, idx[i]].astype(F32)).astype(BF16)
    for i, (b, d) in enumerate(chains):
        s_ref[b, d] = s[i]

    @pl.when(n == pl.num_programs(1) - 1)
    def _():
        sout_ref[...] = s_ref[...]


def _delta_scan(pm, bm, ol, e, s0, bb, cs):
    B, _, N, _, W = pm.shape
    L = N * CHUNK
    steps = N // cs

    def spec(r, d):
        if d == 0:
            return pl.BlockSpec((bb, 1, cs, r, W), lambda b, n: (b, 0, n, 0, 0))
        return pl.BlockSpec((bb, 1, cs, r, W), lambda b, n: (b, 1, steps - 1 - n, 0, 0))

    in_specs = []
    for d in range(2):
        in_specs += [spec(2 * CHUNK, d), spec(CHUNK, d), spec(CHUNK, d), spec(HALO, d)]
    in_specs.append(pl.BlockSpec((bb, 2, CHUNK, W), lambda b, n: (b, 0, 0, 0)))
    return pl.pallas_call(
        _delta_scan_kernel,
        grid=(B // bb, steps),
        in_specs=in_specs,
        out_specs=[pl.BlockSpec((bb, cs * CHUNK, W), lambda b, n: (b, n, 0)),
                   pl.BlockSpec((bb, cs * CHUNK, W), lambda b, n: (b, steps - 1 - n, 0)),
                   pl.BlockSpec((bb, 2, CHUNK, W), lambda b, n: (b, 0, 0, 0))],
        out_shape=[jax.ShapeDtypeStruct((B, L, W), BF16),
                   jax.ShapeDtypeStruct((B, L, W), BF16),
                   jax.ShapeDtypeStruct((B, 2, CHUNK, W), F32)],
        scratch_shapes=[pltpu.VMEM((bb, 2, CHUNK, W), F32)],
        compiler_params=_cparams("parallel", "arbitrary"),
        name="delta_scan",
    )(pm, bm, ol, e, pm, bm, ol, e, s0)


def _merge_kernel(x_ref, mod_ref, up_ref, upp_ref, upn_ref, cnt_ref, gates_ref, oattn_ref, of_ref, ob_ref,
                  pw_ref, pscale_ref, onorm_ref, wout_ref, gpost_ref, o_ref, pad_ref):
    i = pl.program_id(1)
    tm = x_ref.shape[1]
    first = i == 0
    last = i == pl.num_programs(1) - 1
    gates = gates_ref[0].astype(F32)
    y = _dot((gates[:, D_POOL:D_POOL + D_ATTN] * oattn_ref[0].astype(F32)).astype(BF16),
             wout_ref[D_POOL:D_POOL + D_ATTN, :])
    pad_ref[0:HALO, :] = jnp.where(first, 0.0, upp_ref[0])
    pad_ref[HALO:HALO + tm, :] = up_ref[0]
    pad_ref[HALO + tm:2 * HALO + tm, :] = jnp.where(last, 0.0, upn_ref[0])

    def window_sum(col, lo, hi):
        padded = pad_ref[:, col]
        rows = padded.shape[0]
        acc = None
        for j in range(lo, hi):
            term = (padded if j == 0 else pltpu.roll(padded, (-j) % rows, 0))[HALO:HALO + tm, :]
            acc = term if acc is None else acc + term
        return acc

    lane = lax.broadcasted_iota(jnp.int32, (tm, LANES), 1)
    lo_mask = lane < HEAD_DIM

    pooled = []
    for col_blk, (w_lo, w_hi) in enumerate(((2, 4), (8, 16))):
        col = slice(col_blk * LANES, (col_blk + 1) * LANES)
        s_lo = window_sum(col, -(w_lo // 2), w_lo // 2)
        s_hi = s_lo + window_sum(col, -(w_hi // 2), -(w_lo // 2)) + window_sum(col, w_lo // 2, w_hi // 2)
        mean = jnp.where(lo_mask, s_lo, s_hi) / cnt_ref[:, col].astype(F32)
        pooled.append(mean - up_ref[0, :, col])
    pooled = jnp.concatenate(pooled, axis=1)
    o_pool = _dot(pooled.astype(BF16), pw_ref[...]) * pscale_ref[...]

    od = of_ref[0].astype(F32) + ob_ref[0].astype(F32)
    odn = []
    for j in range(D_DELTA // LANES):
        blk = od[:, j * LANES:(j + 1) * LANES]
        odn.append(blk * _half_rms_scale(blk, lo_mask) * onorm_ref[...])
    o_delta = jnp.concatenate(odn, axis=1)

    y = y + _dot((gates[:, 0:D_POOL] * o_pool).astype(BF16), wout_ref[0:D_POOL, :])
    y = y + _dot((gates[:, D_POOL + D_ATTN:] * o_delta).astype(BF16), wout_ref[D_POOL + D_ATTN:, :])
    ms = jnp.mean(y * y, axis=-1, keepdims=True)
    yn = (y * lax.rsqrt(ms + EPS)) * gpost_ref[...]
    gate = mod_ref[0][:, 2 * D_MODEL:3 * D_MODEL]
    o_ref[0] = x_ref[0] + gate * yn


def _window_counts(seq_len):
    t = jnp.arange(seq_len)[:, None]
    w = jnp.repeat(jnp.array(POOL_WINDOWS), D_POOL // len(POOL_WINDOWS))[None, :]
    return (jnp.minimum(t - w // 2 + w, seq_len) - jnp.maximum(t - w // 2, 0)).astype(BF16)


def _merge(x, mod, up, gates, oattn, o_f, o_b, pw_bd, pscale, onorm_row, wout, layer, gpost, tm):
    B, L, _ = x.shape
    nb = tm // HALO
    last_blk = L // HALO - 1
    shared_mod = mod.shape[0] == 1
    row = lambda b, i: (b, i, 0)
    const2 = lambda b, i: (0, 0)
    return pl.pallas_call(
        _merge_kernel,
        grid=(B, L // tm),
        in_specs=[pl.BlockSpec((1, tm, D_MODEL), row),
                  pl.BlockSpec((1, 1, 3 * D_MODEL), (lambda b, i: (0, 0, 0)) if shared_mod else (lambda b, i: (b, 0, 0))),
                  pl.BlockSpec((1, tm, D_POOL), row),
                  pl.BlockSpec((1, HALO, D_POOL), lambda b, i: (b, jnp.maximum(i * nb - 1, 0), 0)),
                  pl.BlockSpec((1, HALO, D_POOL), lambda b, i: (b, jnp.minimum((i + 1) * nb, last_blk), 0)),
                  pl.BlockSpec((tm, D_POOL), lambda b, i: (i, 0)),
                  pl.BlockSpec((1, tm, D_MODEL), row),
                  pl.BlockSpec((1, tm, D_ATTN), row),
                  pl.BlockSpec((1, tm, D_DELTA), row),
                  pl.BlockSpec((1, tm, D_DELTA), row),
                  pl.BlockSpec((D_POOL, D_POOL), const2),
                  pl.BlockSpec((1, D_POOL), const2),
                  pl.BlockSpec((1, LANES), const2),
                  pl.BlockSpec((None, D_MODEL, D_MODEL), lambda b, i: (layer, 0, 0)),
                  pl.BlockSpec((1, D_MODEL), const2)],
        out_specs=pl.BlockSpec((1, tm, D_MODEL), row),
        out_shape=jax.ShapeDtypeStruct((B, L, D_MODEL), F32),
        scratch_shapes=[pltpu.VMEM((tm + 2 * HALO, D_POOL), F32)],
        compiler_params=_cparams("parallel", "parallel"),
        name="merge",
    )(x, mod, up, up, up, _window_counts(L), gates, oattn, o_f, o_b, pw_bd, pscale, onorm_row, wout, gpost)


def _rope_tables(num_tokens):
    rows = num_tokens // GRID_W
    row = jnp.repeat(jnp.arange(rows, dtype=F32), GRID_W)
    col = (jnp.arange(rows * GRID_W) % GRID_W).astype(F32)
    axis_dim = HEAD_DIM // 2
    inv = ROPE_THETA ** (-jnp.arange(0, axis_dim, 2, dtype=F32) / axis_dim)
    ang = jnp.concatenate([row[:, None] * inv, col[:, None] * inv], axis=-1)
    cos = jnp.repeat(jnp.cos(ang), 2, axis=-1)
    sin = jnp.repeat(jnp.sin(ang), 2, axis=-1)
    sign = jnp.tile(jnp.array([-1.0, 1.0], F32), HEAD_DIM // 2)
    return jnp.tile(cos, (1, 2)), jnp.tile(sin * sign, (1, 2))


def _w_in_tail(w):
    ba = w[..., OFF_TAIL:OFF_TAIL + 4 * N_DELTA_HEADS]
    g_delta = w[..., OFF_TAIL + 4 * N_DELTA_HEADS:]
    pad = jnp.zeros(w.shape[:-1] + (LANES - 4 * N_DELTA_HEADS,), w.dtype)
    return jnp.concatenate([g_delta, ba, pad], axis=-1).astype(BF16)


def _lane_row(vec, offset):
    return jnp.zeros((1, LANES), F32).at[0, offset:offset + vec.size].set(vec.reshape(-1))


def _cached_kv_variants(k, v):
    lane = jnp.arange(LANES)
    pairs = _kv_variants(k, jnp.roll(k, HEAD_DIM, axis=-1), v, jnp.roll(v, HEAD_DIM, axis=-1), lane)
    return (jnp.concatenate([p[0] for p in pairs], axis=-1).astype(BF16),
            jnp.concatenate([p[1] for p in pairs], axis=-1).astype(BF16))


def _layer(x, mod, lw, rope_tabs, ctx, tiles):
    B, L, _ = x.shape
    latent = ctx is not None
    outs = _inproj(x, mod, lw["gpre"], lw["w_in"], lw["w_in_tail"], lw["layer"], lw["qkn"], lw["conv_w"],
                   lw["alog_row"], lw["dtb_row"], rope_tabs, tiles["tm"])
    up, gates, q, k4, v4, ksq, act, bg = outs[:8]
    sources = [(k4, v4)]
    if latent:
        sources.append(ctx["kv"])
    oattn = _attention(q, lw["qmax"], ksq, sources, tiles["tq"], tiles["tk"])

    gt = bg[:, :, 2 * N_DELTA_HEADS:4 * N_DELTA_HEADS].reshape(B, L // CHUNK, CHUNK, 2, N_DELTA_HEADS)
    gt = gt.transpose(0, 1, 3, 4, 2).reshape(B, L // CHUNK, 2, D_DELTA)
    pm, bm, ol, e = _delta_chunks(act, bg, gt, tiles["tc"])
    s0 = ctx["state"] if latent else jnp.zeros((B, 2, HEAD_DIM, D_DELTA), F32)
    o_f, o_b, s_out = _delta_scan(pm, bm, ol, e, s0, tiles["bb"], tiles["cs"])

    y = _merge(x, mod, up, gates, oattn, o_f, o_b, lw["pw_bd"], lw["pscale"], lw["onorm_row"],
               lw["w_out"], lw["layer"], lw["gpost"], tiles["tm"])
    if latent:
        return y
    kn, v = outs[8], outs[9]
    s_out = s_out.reshape(B, 2, HEAD_DIM, N_DELTA_HEADS, HEAD_DIM).transpose(0, 1, 3, 2, 4)
    return y, kn, v, s_out


def kernel(x_prompt, x_sample, cache_attn_k, cache_attn_v, state_delta, c, c_ctx, w_mod, b_mod, norm_pre, norm_post,
           w_in, w_out, pool_w, pool_scale, q_norm, k_norm, conv_w, a_log, dt_bias, o_norm):
    B, L, _ = x_prompt.shape
    DB, DL, _ = x_sample.shape
    past = cache_attn_k.shape[2]

    conds = jnp.zeros((8, D_MODEL), F32).at[0].set(c_ctx).at[1:1 + DB].set(c)
    mod = _modulation(conds, w_mod, b_mod)
    rope_tabs = _rope_tables(DL)

    ctx_tiles = dict(tm=256, tq=256, tk=256, tc=256, bb=8, cs=4)
    lat_tiles = dict(tm=512, tq=256, tk=1024, tc=256, bb=4, cs=4)

    hp, hs = x_prompt, x_sample
    new_k, new_v, new_s = [], [], []
    w_in_all = w_in.astype(BF16)
    w_in_tail = _w_in_tail(w_in)
    w_out_all = w_out.astype(BF16)
    for l in range(DEPTH):
        blocks = jnp.zeros((N_DELTA_HEADS, HEAD_DIM, N_DELTA_HEADS, HEAD_DIM), F32)
        blocks = blocks.at[jnp.arange(4), :, jnp.arange(4), :].set(pool_w[l])
        qkn = jnp.zeros((8, LANES), F32).at[0].set(jnp.tile(q_norm[l], 2)).at[1].set(jnp.tile(k_norm[l], 2))
        lw = dict(
            layer=l,
            gpre=norm_pre[l].reshape(1, D_MODEL),
            gpost=norm_post[l].reshape(1, D_MODEL),
            w_in=w_in_all,
            w_in_tail=w_in_tail,
            w_out=w_out_all,
            qkn=qkn,
            conv_w=conv_w[l],
            alog_row=_lane_row(a_log[l], 2 * N_DELTA_HEADS),
            dtb_row=_lane_row(dt_bias[l], 2 * N_DELTA_HEADS),
            pw_bd=blocks.reshape(D_POOL, D_POOL).astype(BF16),
            pscale=pool_scale[l].reshape(1, D_POOL),
            onorm_row=jnp.tile(o_norm[l], 2).reshape(1, LANES),
            qmax=jnp.full((1, LANES), HEAD_DIM ** 0.5 * Q_SCALE * 1.01, F32) * jnp.max(jnp.abs(q_norm[l])),
        )
        hp, k_l, v_l, s_l = _layer(hp, mod[l, 0:1].reshape(1, 1, 3 * D_MODEL), lw, None, None, ctx_tiles)
        ctx = dict(kv=_cached_kv_variants(cache_attn_k[:, l].reshape(DB, past, D_KV),
                                          cache_attn_v[:, l].reshape(DB, past, D_KV)),
                   state=state_delta[:, l].transpose(0, 1, 3, 2, 4).reshape(DB, 2, HEAD_DIM, D_DELTA))
        hs = _layer(hs, mod[l, 1:1 + DB].reshape(DB, 1, 3 * D_MODEL), lw, rope_tabs, ctx, lat_tiles)
        new_k.append(k_l.reshape(B, L, 2, HEAD_DIM))
        new_v.append(v_l.reshape(B, L, 2, HEAD_DIM))
        new_s.append(s_l)
    return (hp, hs, jnp.stack(new_k, axis=1), jnp.stack(new_v, axis=1), jnp.stack(new_s, axis=1))
```

```python
import functools

import jax
import jax.numpy as jnp
from jax import lax
from jax.experimental import pallas as pl
from jax.experimental.pallas import tpu as pltpu

F32 = jnp.float32
BF16 = jnp.bfloat16

D_MODEL = 1024
DEPTH = 2
GRID_W = 64
HEAD_DIM = 64
D_POOL = 256
D_ATTN = 512
D_DELTA = 256
D_KV = 128
N_Q_HEADS = 8
N_DELTA_HEADS = 4
POOL_WINDOWS = (2, 4, 8, 16)
CHUNK = 64
ROPE_THETA = 10000.0
EPS = 1e-6
LANES = 128
HALO = 8
MASK_VALUE = -1e30
LOG2E = 1.4426950408889634
Q_SCALE = HEAD_DIM ** -0.5 * LOG2E
SHIFT_LIMIT = 60.0

D_IN = 2832
OFF_UP = 0
OFF_GPOOL = 256
OFF_QKV = 512
OFF_GATTN = 1280
OFF_QKVD = 1792
OFF_TAIL = 2560
TAIL_GDELTA = 0
TAIL_BA = 256
D_TAIL = 384

VMEM_LIMIT = 56 * 1024 * 1024


def _cparams(*sem):
    return pltpu.CompilerParams(dimension_semantics=sem, vmem_limit_bytes=VMEM_LIMIT)


def _dot(a, b):
    return jnp.dot(a, b, preferred_element_type=F32)


def _dot_nt(a, b):
    return lax.dot_general(a, b, (((1,), (1,)), ((), ())), preferred_element_type=F32)


def _dot_tn(a, b):
    return lax.dot_general(a, b, (((0,), (0,)), ((), ())), preferred_element_type=F32)


def _split(x):
    hi = x.astype(BF16)
    return hi, (x - hi.astype(F32)).astype(BF16)


def _silu(x):
    return x * jax.nn.sigmoid(x)


def _softplus(x):
    return jnp.maximum(x, 0.0) + jnp.log1p(jnp.exp(-jnp.abs(x)))


def _kv_variants(k, k_sw, v, v_sw, lane):
    lo = lane < HEAD_DIM
    one_hi = (lane == HEAD_DIM).astype(k.dtype)
    one_lo = (lane == 0).astype(k.dtype)
    ones = jnp.ones_like(v)
    return ((jnp.where(lo, k, one_hi), jnp.where(lo, v, ones)),
            (jnp.where(lo, one_lo, k_sw), jnp.where(lo, ones, v_sw)),
            (jnp.where(lo, k_sw, one_hi), jnp.where(lo, v_sw, ones)),
            (jnp.where(lo, one_lo, k), jnp.where(lo, ones, v)))


def _half_rms_scale(x, lo_mask):
    sq = x * x
    ss_lo = jnp.sum(jnp.where(lo_mask, sq, 0.0), axis=-1, keepdims=True)
    ss_hi = jnp.sum(jnp.where(lo_mask, 0.0, sq), axis=-1, keepdims=True)
    r_lo = lax.rsqrt(ss_lo * (1.0 / HEAD_DIM) + EPS)
    r_hi = lax.rsqrt(ss_hi * (1.0 / HEAD_DIM) + EPS)
    return jnp.where(lo_mask, r_lo, r_hi)


def _mod_kernel(c_ref, w_ref, b_ref, o_ref):
    s = _silu(c_ref[...])
    o_ref[0] = _dot(s.astype(BF16), w_ref[0].astype(BF16)) + b_ref[0]


def _modulation(conds, w_mod, b_mod):
    tn = 768
    return pl.pallas_call(
        _mod_kernel,
        grid=(DEPTH, 3 * D_MODEL // tn),
        in_specs=[pl.BlockSpec((8, D_MODEL), lambda l, j: (0, 0)),
                  pl.BlockSpec((1, D_MODEL, tn), lambda l, j: (l, 0, j)),
                  pl.BlockSpec((1, 1, tn), lambda l, j: (l, 0, j))],
        out_specs=pl.BlockSpec((1, 8, tn), lambda l, j: (l, 0, j)),
        out_shape=jax.ShapeDtypeStruct((DEPTH, 8, 3 * D_MODEL), F32),
        compiler_params=_cparams("parallel", "parallel"),
        name="modulation",
    )(conds, w_mod, b_mod.reshape(DEPTH, 1, 3 * D_MODEL))


def _inproj_kernel(rope, x_ref, xprev_ref, xnext_ref, mod_ref, gpre_ref, w_ref, wtail_ref, qkn_ref, convw_ref, alog_ref,
                   dtb_ref, *rest):
    if rope:
        cos_ref, sin_ref, up_ref, gates_ref, q_ref, k4_ref, v4_ref, ksq_ref, act_ref, bg_ref, pad_ref = rest
    else:
        up_ref, gates_ref, q_ref, k4_ref, v4_ref, ksq_ref, act_ref, bg_ref, kn_ref, v_ref, pad_ref = rest
    i = pl.program_id(1)
    tm = x_ref.shape[1]
    mod = mod_ref[0]
    shift = mod[:, 0:D_MODEL]
    scale = mod[:, D_MODEL:2 * D_MODEL]

    def modulated_norm(x):
        ms = jnp.mean(x * x, axis=-1, keepdims=True)
        return ((x * lax.rsqrt(ms + EPS)) * gpre_ref[...] * (1.0 + scale) + shift).astype(BF16)

    hb = modulated_norm(x_ref[0])

    zqkv = _dot(hb, w_ref[:, OFF_QKV:OFF_QKV + D_ATTN + 2 * D_KV])
    zqk, v = zqkv[:, 0:D_ATTN + D_KV], zqkv[:, D_ATTN + D_KV:]
    wd = w_ref[:, OFF_QKVD:OFF_QKVD + 3 * D_DELTA]
    pad_ref[0:HALO, :] = jnp.where(i == 0, 0.0, _dot(modulated_norm(xprev_ref[0]), wd))
    pad_ref[HALO:HALO + tm, :] = _dot(hb, wd)
    pad_ref[HALO + tm:2 * HALO + tm, :] = jnp.where(i == pl.num_programs(1) - 1, 0.0, _dot(modulated_norm(xnext_ref[0]), wd))
    zg_pool = _dot(hb, w_ref[:, OFF_GPOOL:OFF_GPOOL + D_POOL])
    zg_attn = _dot(hb, w_ref[:, OFF_GATTN:OFF_GATTN + D_ATTN])
    zg_delta = _dot(hb, wtail_ref[:, TAIL_GDELTA:TAIL_GDELTA + D_DELTA])

    lane = lax.broadcasted_iota(jnp.int32, (tm, LANES), 1)
    lo_mask = lane < HEAD_DIM
    even = (lane % 2) == 0

    def head_norm_rope(blk, gain):
        y = blk * _half_rms_scale(blk, lo_mask) * gain
        if rope:
            swapped = jnp.where(even, pltpu.roll(y, LANES - 1, 1), pltpu.roll(y, 1, 1))
            return y, y * cos_ref[...] + swapped * sin_ref[...]
        return y, y

    for qb in range(D_ATTN // LANES):
        _, qr = head_norm_rope(zqk[:, qb * LANES:(qb + 1) * LANES], qkn_ref[0:1, :])
        qs = qr * Q_SCALE
        q_ref[0, :, 2 * qb * LANES:(2 * qb + 1) * LANES] = jnp.where(lo_mask, qs, 0.0).astype(BF16)
        q_ref[0, :, (2 * qb + 1) * LANES:(2 * qb + 2) * LANES] = jnp.where(lo_mask, 0.0, qs).astype(BF16)
    gates_ref[0, :, 0:D_POOL] = _silu(zg_pool).astype(BF16)
    gates_ref[0, :, D_POOL:D_POOL + D_ATTN] = _silu(zg_attn).astype(BF16)
    gates_ref[0, :, D_POOL + D_ATTN:] = _silu(zg_delta).astype(BF16)
    up_ref[0] = _dot(hb, w_ref[:, OFF_UP:OFF_UP + D_POOL])
    ba = _dot(hb, wtail_ref[:, TAIL_BA:TAIL_BA + LANES])

    xp = pad_ref[...]
    rows = xp.shape[0]
    acc = None
    for j in range(4):
        shifted = xp if j == 2 else pltpu.roll(xp, (2 - j) % rows, 0)
        term = convw_ref[j:j + 1, :] * shifted[HALO:HALO + tm, :]
        acc = term if acc is None else acc + term
    act_ref[0] = _silu(acc)
    g = -jnp.exp(alog_ref[...]) * _softplus(ba + dtb_ref[...])
    bg_ref[0] = jnp.where(lane < 2 * N_DELTA_HEADS, jax.nn.sigmoid(ba), jnp.where(lane < 4 * N_DELTA_HEADS, g, 0.0))

    kn, kr = head_norm_rope(zqk[:, D_ATTN:D_ATTN + D_KV], qkn_ref[1:2, :])
    for a, (kx, vx) in enumerate(_kv_variants(kr, pltpu.roll(kr, HEAD_DIM, 1), v, pltpu.roll(v, HEAD_DIM, 1), lane)):
        k4_ref[0, :, a * LANES:(a + 1) * LANES] = kx.astype(BF16)
        v4_ref[0, :, a * LANES:(a + 1) * LANES] = vx.astype(BF16)
    ksq = kr * kr
    ksq = jnp.where(lo_mask, jnp.sum(jnp.where(lo_mask, ksq, 0.0), axis=-1, keepdims=True),
                    jnp.sum(jnp.where(lo_mask, 0.0, ksq), axis=-1, keepdims=True))
    ksq_ref[0, 0] = jnp.broadcast_to(jnp.max(ksq, axis=0, keepdims=True), (HALO, LANES))
    if not rope:
        kn_ref[0] = kn
        v_ref[0] = v


def _inproj(x, mod, gpre, w, wtail, layer, qkn, conv_w, alog_row, dtb_row, rope_tabs, tm):
    B, L, _ = x.shape
    rope = rope_tabs is not None
    shared_mod = mod.shape[0] == 1
    nb = tm // HALO
    last_blk = L // HALO - 1
    row = lambda b, i: (b, i, 0)
    const2 = lambda b, i: (0, 0)
    in_specs = [pl.BlockSpec((1, tm, D_MODEL), row),
                pl.BlockSpec((1, HALO, D_MODEL), lambda b, i: (b, jnp.maximum(i * nb - 1, 0), 0)),
                pl.BlockSpec((1, HALO, D_MODEL), lambda b, i: (b, jnp.minimum((i + 1) * nb, last_blk), 0)),
                pl.BlockSpec((1, 1, 3 * D_MODEL), (lambda b, i: (0, 0, 0)) if shared_mod else (lambda b, i: (b, 0, 0))),
                pl.BlockSpec((1, D_MODEL), const2),
                pl.BlockSpec((None, D_MODEL, D_IN), lambda b, i: (layer, 0, 0)),
                pl.BlockSpec((None, D_MODEL, D_TAIL), lambda b, i: (layer, 0, 0)),
                pl.BlockSpec((8, LANES), const2),
                pl.BlockSpec((4, 3 * D_DELTA), const2),
                pl.BlockSpec((1, LANES), const2),
                pl.BlockSpec((1, LANES), const2)]
    args = [x, x, x, mod, gpre, w, wtail, qkn, conv_w, alog_row, dtb_row]
    if rope:
        in_specs += [pl.BlockSpec((tm, LANES), lambda b, i: (i, 0))] * 2
        args += list(rope_tabs)
    widths = [(D_POOL, F32), (D_MODEL, BF16), (N_Q_HEADS * LANES, BF16), (4 * LANES, BF16), (4 * LANES, BF16), None,
              (3 * D_DELTA, F32), (LANES, F32)]
    if not rope:
        widths += [(D_KV, F32), (D_KV, F32)]
    out_specs = [pl.BlockSpec((1, tm, wd[0]), row) if wd else pl.BlockSpec((1, 1, HALO, LANES), lambda b, i: (b, i, 0, 0))
                 for wd in widths]
    out_shape = [jax.ShapeDtypeStruct((B, L, wd[0]), wd[1]) if wd else jax.ShapeDtypeStruct((B, L // tm, HALO, LANES), F32)
                 for wd in widths]
    return pl.pallas_call(
        functools.partial(_inproj_kernel, rope),
        grid=(B, L // tm),
        in_specs=in_specs,
        out_specs=out_specs,
        out_shape=out_shape,
        scratch_shapes=[pltpu.VMEM((tm + 2 * HALO, 3 * D_DELTA), F32)],
        compiler_params=_cparams("parallel", "parallel"),
        name="inproj_rope" if rope else "inproj",
    )(*args)


_VARIANT_HEADS = ((0, 2), (1, 3), (4, 6), (5, 7))


def _attn_kernel(n_src, tk, q_ref, qmax_ref, ksq_ref, *rest):
    kv_refs = rest[:2 * n_src]
    o_ref, acc_ref, m_ref = rest[2 * n_src:]
    tq = q_ref.shape[1]
    n_var = len(_VARIANT_HEADS)
    lane = lax.broadcasted_iota(jnp.int32, (tq, LANES), 1)
    lo_mask = lane < HEAD_DIM

    ksq = jnp.max(ksq_ref[0], axis=0)[0:1, :]
    lo_row = lo_mask[0:1, :]
    ksq_head = [jnp.max(jnp.where(lo_row, ksq, 0.0), axis=-1, keepdims=True),
                jnp.max(jnp.where(lo_row, 0.0, ksq), axis=-1, keepdims=True)]
    for j in range(1, n_src):
        for kv_head in range(2):
            kx = kv_refs[2 * j][0, :, 2 * kv_head * LANES:(2 * kv_head + 1) * LANES].astype(F32)
            sq = jnp.sum(jnp.where(lane[0:1, :] < HEAD_DIM, kx * kx, 0.0), axis=-1, keepdims=True)
            ksq_head[kv_head] = jnp.maximum(ksq_head[kv_head], jnp.max(sq, axis=0, keepdims=True))

    bound = [qmax_ref[0:1, 0:1] * jnp.sqrt(ksq_head[kv_head]) for kv_head in range(2)]
    safe = jnp.max(jnp.maximum(bound[0], bound[1])) <= SHIFT_LIMIT

    def head_block(h):
        return q_ref[0, :, h * LANES:(h + 1) * LANES]

    def shifted_block(h):
        blk = head_block(h)
        return jnp.where(lane == (HEAD_DIM if h % 2 == 0 else 0), (-bound[h // 4]).astype(BF16), blk)

    q_shift = [jnp.concatenate([shifted_block(h) for h in heads], axis=0) for heads in _VARIANT_HEADS]

    cols = [slice(a * LANES, (a + 1) * LANES) for a in range(n_var)]
    chunks = []
    for j in range(n_src):
        k4_ref, v4_ref = kv_refs[2 * j], kv_refs[2 * j + 1]
        S = k4_ref.shape[1]
        size = min(tk, S)
        chunks += [(k4_ref, v4_ref, c * size, size) for c in range(S // size)]

    @pl.when(safe)
    def _():
        def scores(ch):
            k4_ref, _, start, size = ch
            return [_dot_nt(q_shift[a], k4_ref[0, start:start + size, cols[a]]) for a in range(n_var)]

        acc = [None] * n_var
        s_next = scores(chunks[0])
        for ci, (_, v4_ref, start, size) in enumerate(chunks):
            s = s_next
            if ci + 1 < len(chunks):
                s_next = scores(chunks[ci + 1])
            for a in range(n_var):
                pv = _dot(jnp.exp2(s[a]).astype(BF16), v4_ref[0, start:start + size, cols[a]])
                acc[a] = pv if acc[a] is None else acc[a] + pv
        for a in range(n_var):
            acc_ref[a] = acc[a]

    @pl.when(jnp.logical_not(safe))
    def _():
        m_ref[...] = jnp.full(m_ref.shape, MASK_VALUE, F32)
        acc_ref[...] = jnp.zeros(acc_ref.shape, F32)
        q_plain = [jnp.concatenate([head_block(h) for h in heads], axis=0) for heads in _VARIANT_HEADS]
        for k4_ref, v4_ref, start, size in chunks:
            for a in range(n_var):
                s = _dot_nt(q_plain[a], k4_ref[0, start:start + size, cols[a]])
                m_prev = m_ref[a]
                m_new = jnp.maximum(m_prev, jnp.max(s, axis=-1, keepdims=True))
                p = jnp.exp2(s - jnp.tile(m_new, (1, size // LANES)))
                m_ref[a] = m_new
                acc_ref[a] = jnp.exp2(m_prev - m_new) * acc_ref[a] + _dot(p.astype(BF16), v4_ref[0, start:start + size, cols[a]])

    def head_out(h):
        a = 2 * (h // 4) + h % 2
        r = _VARIANT_HEADS[a].index(h) * tq
        rows = acc_ref[a, r:r + tq, :]
        return rows / pltpu.roll(rows, HEAD_DIM, 1)

    for i in range(N_Q_HEADS // 2):
        o_ref[0, :, i * LANES:(i + 1) * LANES] = jnp.where(lo_mask, head_out(2 * i), head_out(2 * i + 1)).astype(BF16)


def _attention(q, qmax, ksq, sources, tq, tk):
    B, L, _ = q.shape
    n_var = len(_VARIANT_HEADS)
    in_specs = [pl.BlockSpec((1, tq, N_Q_HEADS * LANES), lambda b, i: (b, i, 0)),
                pl.BlockSpec((1, LANES), lambda b, i: (0, 0)),
                pl.BlockSpec((1,) + ksq.shape[1:], lambda b, i: (b, 0, 0, 0))]
    args = [q, qmax, ksq]
    for k4, v4 in sources:
        S = k4.shape[1]
        in_specs += [pl.BlockSpec((1, S, n_var * LANES), lambda b, i: (b, 0, 0))] * 2
        args += [k4, v4]
    rows = N_Q_HEADS // n_var * tq
    return pl.pallas_call(
        functools.partial(_attn_kernel, len(sources), tk),
        grid=(B, L // tq),
        in_specs=in_specs,
        out_specs=pl.BlockSpec((1, tq, D_ATTN), lambda b, i: (b, i, 0)),
        out_shape=jax.ShapeDtypeStruct((B, L, D_ATTN), BF16),
        scratch_shapes=[pltpu.VMEM((n_var, rows, LANES), F32)] * 2,
        compiler_params=_cparams("parallel", "parallel"),
        name="attention",
    )(*args)


def _block_diag(x, bd_mask):
    return jnp.where(bd_mask, jnp.concatenate([x] * N_DELTA_HEADS, axis=0), jnp.zeros((), x.dtype))


def _dot_split_bd(a, b, bd_mask):
    (ah, al), (bh, bl) = a, b
    m = ah.shape[0]
    both = _dot(jnp.concatenate([ah, al], axis=0), _block_diag(bh, bd_mask))
    return both[0:m] + both[m:2 * m] + _dot(ah, _block_diag(bl, bd_mask))


def _delta_chunk_kernel(act_ref, bg_ref, gt_ref, pm_ref, bm_ref, ol_ref, e_ref):
    C, W, H = CHUNK, D_DELTA, N_DELTA_HEADS
    n_chunks = act_ref.shape[1] // C
    ri = lax.broadcasted_iota(jnp.int32, (C, W), 0)
    lane = lax.broadcasted_iota(jnp.int32, (C, W), 1)
    lj = lane % HEAD_DIM
    blk = lane // HEAD_DIM
    bd_mask = (lax.broadcasted_iota(jnp.int32, (W, W), 0) // HEAD_DIM) == (lax.broadcasted_iota(jnp.int32, (W, W), 1) // HEAD_DIM)
    diag = ri == lj
    eye = diag.astype(F32)

    def expand(cols):
        res = cols[H - 1]
        for h in range(H - 2, -1, -1):
            res = jnp.where(blk == h, cols[h], res)
        return res

    def seg_sum(x):
        return expand([jnp.sum(jnp.where(blk == h, x, 0.0), axis=-1, keepdims=True) for h in range(H)])

    def l2n(x):
        return x * lax.rsqrt(seg_sum(x * x) + EPS)

    def row_form(x):
        res = x[(H - 1) * C:H * C]
        for h in range(H - 2, -1, -1):
            res = jnp.where(blk == h, x[h * C:(h + 1) * C], res)
        return res

    units = []
    for c in range(n_chunks):
        rows = slice(c * C, (c + 1) * C)
        q = l2n(act_ref[0, rows, 0:W]) * (HEAD_DIM ** -0.5)
        k = l2n(act_ref[0, rows, W:2 * W])
        v = act_ref[0, rows, 2 * W:3 * W]
        kb = k.astype(BF16)
        kkqk = _dot_nt(jnp.concatenate([kb, q.astype(BF16)], axis=0), _block_diag(kb, bd_mask))
        kk, qk = kkqk[0:C], kkqk[C:2 * C]
        bg = bg_ref[0, rows, :]
        gt = gt_ref[0, c]
        for d in range(2):
            beta = expand([bg[:, d * H + h:d * H + h + 1] for h in range(H)])
            g = expand([bg[:, 2 * H + d * H + h:2 * H + d * H + h + 1] for h in range(H)])
            g_row = gt[d:d + 1, :]
            incl = (ri >= lj) if d == 0 else (ri <= lj)
            incl_t = (ri <= lj) if d == 0 else (ri >= lj)
            strict = (ri > lj) if d == 0 else (ri < lj)
            gc = seg_sum(jnp.where(incl, g_row, 0.0))
            gc_row = jnp.sum(jnp.where(incl_t, g, 0.0), axis=0, keepdims=True)
            g_tot = jnp.sum(g, axis=0, keepdims=True)
            decay = jnp.where(incl, jnp.exp(jnp.where(incl, gc - gc_row, 0.0)), 0.0)
            t = jnp.where(strict, -(beta * kk * decay), 0.0)
            units.append(dict(c=c, d=d, q=q, k=k, v=v, beta=beta, gc=gc, g_tot=g_tot,
                              attn=(qk * decay).astype(BF16), t=t, p=eye + t))

    for un in units:
        un["t"] = _dot_split_bd(_split(un["t"]), _split(un["t"]), bd_mask)
    for _ in range(4):
        for un in units:
            both = _dot_split_bd(_split(jnp.concatenate([un["t"], un["p"]], axis=0)), _split(un["t"]), bd_mask)
            un["t"] = both[0:C]
            un["p"] = un["p"] + both[C:2 * C]
    for un in units:
        un["p"] = (un["p"] + _dot(un["p"].astype(BF16), _block_diag(un["t"].astype(BF16), bd_mask))).astype(BF16)
    for un in units:
        un["egc"] = jnp.exp(un["gc"])
        un["u"] = _dot(un["p"], _block_diag((un["v"] * un["beta"]).astype(BF16), bd_mask)).astype(BF16)
        un["w"] = _dot(un["p"], _block_diag((un["k"] * (un["beta"] * un["egc"])).astype(BF16), bd_mask)).astype(BF16)
    for un in units:
        c, d = un["c"], un["d"]
        kdec = (un["k"] * jnp.exp(un["g_tot"] - un["gc"])).astype(BF16)
        bm_ref[0, d, c] = row_form(_dot_tn(kdec, un["u"])).astype(BF16)
        mp = row_form(_dot_tn(kdec, un["w"]))
        qt = un["q"] * un["egc"] - _dot(un["attn"], _block_diag(un["w"], bd_mask))
        ol_ref[0, d, c] = _dot(un["attn"], _block_diag(un["u"], bd_mask)).astype(BF16)
        pm_ref[0, d, c] = jnp.concatenate([mp, qt], axis=0).astype(BF16)
        e_ref[0, d, c] = jnp.broadcast_to(jnp.exp(un["g_tot"]), (HALO, W))


def _delta_chunks(act, bg, gt, tc):
    B, L, _ = act.shape
    N = L // CHUNK
    cb = tc // CHUNK
    W = D_DELTA
    ospec = lambda r: pl.BlockSpec((1, 2, cb, r, W), lambda b, i: (b, 0, i, 0, 0))
    return pl.pallas_call(
        _delta_chunk_kernel,
        grid=(B, L // tc),
        in_specs=[pl.BlockSpec((1, tc, 3 * W), lambda b, i: (b, i, 0)),
                  pl.BlockSpec((1, tc, LANES), lambda b, i: (b, i, 0)),
                  pl.BlockSpec((1, cb, 2, W), lambda b, i: (b, i, 0, 0))],
        out_specs=[ospec(2 * CHUNK), ospec(CHUNK), ospec(CHUNK), ospec(HALO)],
        out_shape=[jax.ShapeDtypeStruct((B, 2, N, 2 * CHUNK, W), BF16),
                   jax.ShapeDtypeStruct((B, 2, N, CHUNK, W), BF16),
                   jax.ShapeDtypeStruct((B, 2, N, CHUNK, W), BF16),
                   jax.ShapeDtypeStruct((B, 2, N, HALO, W), F32)],
        compiler_params=_cparams("parallel", "parallel"),
        name="delta_chunks",
    )(act, bg, gt)


def _delta_scan_kernel(pmf_ref, bmf_ref, olf_ref, ef_ref, pmb_ref, bmb_ref, olb_ref, eb_ref, s0_ref,
                       of_ref, ob_ref, sout_ref, s_ref):
    n = pl.program_id(1)
    bb = s_ref.shape[0]
    C, W = CHUNK, D_DELTA
    bd_mask = (lax.broadcasted_iota(jnp.int32, (W, W), 0) // HEAD_DIM) == (lax.broadcasted_iota(jnp.int32, (W, W), 1) // HEAD_DIM)

    @pl.when(n == 0)
    def _():
        s_ref[...] = s0_ref[...]

    dirs = ((pmf_ref, bmf_ref, olf_ref, ef_ref, of_ref), (pmb_ref, bmb_ref, olb_ref, eb_ref, ob_ref))
    cs = pmf_ref.shape[2]
    chains = [(b, d) for b in range(bb) for d in range(2)]
    s = [s_ref[b, d] for b, d in chains]
    for step in range(cs):
        idx = [step if d == 0 else cs - 1 - step for _, d in chains]
        r = [_dot(dirs[d][0][b, 0, idx[i]], _block_diag(s[i].astype(BF16), bd_mask)) for i, (b, d) in enumerate(chains)]
        for i, (b, d) in enumerate(chains):
            _, bm_ref, ol_ref, e_ref, o_ref = dirs[d]
            s[i] = e_ref[b, 0, idx[i]][0:1, :] * s[i] + bm_ref[b, 0, idx[i]].astype(F32) - r[i][0:C]
            o_ref[b, idx[i] * C:(idx[i] + 1) * C, :] = (r[i][C:2 * C] + ol_ref[b, 0, idx[i]].astype(F32)).astype(BF16)
    for i, (b, d) in enumerate(chains):
        s_ref[b, d] = s[i]

    @pl.when(n == pl.num_programs(1) - 1)
    def _():
        sout_ref[...] = s_ref[...]


def _delta_scan(pm, bm, ol, e, s0, bb, cs):
    B, _, N, _, W = pm.shape
    L = N * CHUNK
    steps = N // cs

    def spec(r, d):
        if d == 0:
            return pl.BlockSpec((bb, 1, cs, r, W), lambda b, n: (b, 0, n, 0, 0))
        return pl.BlockSpec((bb, 1, cs, r, W), lambda b, n: (b, 1, steps - 1 - n, 0, 0))

    in_specs = []
    for d in range(2):
        in_specs += [spec(2 * CHUNK, d), spec(CHUNK, d), spec(CHUNK, d), spec(HALO, d)]
    in_specs.append(pl.BlockSpec((bb, 2, CHUNK, W), lambda b, n: (b, 0, 0, 0)))
    return pl.pallas_call(
        _delta_scan_kernel,
        grid=(B // bb, steps),
        in_specs=in_specs,
        out_specs=[pl.BlockSpec((bb, cs * CHUNK, W), lambda b, n: (b, n, 0)),
                   pl.BlockSpec((bb, cs * CHUNK, W), lambda b, n: (b, steps - 1 - n, 0)),
                   pl.BlockSpec((bb, 2, CHUNK, W), lambda b, n: (b, 0, 0, 0))],
        out_shape=[jax.ShapeDtypeStruct((B, L, W), BF16),
                   jax.ShapeDtypeStruct((B, L, W), BF16),
                   jax.ShapeDtypeStruct((B, 2, CHUNK, W), F32)],
        scratch_shapes=[pltpu.VMEM((bb, 2, CHUNK, W), F32)],
        compiler_params=_cparams("parallel", "arbitrary"),
        name="delta_scan",
    )(pm, bm, ol, e, pm, bm, ol, e, s0)


def _merge_kernel(x_ref, mod_ref, up_ref, upp_ref, upn_ref, cnt_ref, gates_ref, oattn_ref, of_ref, ob_ref,
                  pw_ref, pscale_ref, onorm_ref, wout_ref, gpost_ref, o_ref, pad_ref):
    i = pl.program_id(1)
    tm = x_ref.shape[1]
    first = i == 0
    last = i == pl.num_programs(1) - 1
    gates = gates_ref[0].astype(F32)
    y = _dot((gates[:, D_POOL:D_POOL + D_ATTN] * oattn_ref[0].astype(F32)).astype(BF16),
             wout_ref[D_POOL:D_POOL + D_ATTN, :])
    pad_ref[0:HALO, :] = jnp.where(first, 0.0, upp_ref[0])
    pad_ref[HALO:HALO + tm, :] = up_ref[0]
    pad_ref[HALO + tm:2 * HALO + tm, :] = jnp.where(last, 0.0, upn_ref[0])

    def window_sum(col, lo, hi):
        padded = pad_ref[:, col]
        rows = padded.shape[0]
        acc = None
        for j in range(lo, hi):
            term = (padded if j == 0 else pltpu.roll(padded, (-j) % rows, 0))[HALO:HALO + tm, :]
            acc = term if acc is None else acc + term
        return acc

    lane = lax.broadcasted_iota(jnp.int32, (tm, LANES), 1)
    lo_mask = lane < HEAD_DIM

    pooled = []
    for col_blk, (w_lo, w_hi) in enumerate(((2, 4), (8, 16))):
        col = slice(col_blk * LANES, (col_blk + 1) * LANES)
        s_lo = window_sum(col, -(w_lo // 2), w_lo // 2)
        s_hi = s_lo + window_sum(col, -(w_hi // 2), -(w_lo // 2)) + window_sum(col, w_lo // 2, w_hi // 2)
        mean = jnp.where(lo_mask, s_lo, s_hi) / cnt_ref[:, col].astype(F32)
        pooled.append(mean - up_ref[0, :, col])
    pooled = jnp.concatenate(pooled, axis=1)
    o_pool = _dot(pooled.astype(BF16), pw_ref[...]) * pscale_ref[...]

    od = of_ref[0].astype(F32) + ob_ref[0].astype(F32)
    odn = []
    for j in range(D_DELTA // LANES):
        blk = od[:, j * LANES:(j + 1) * LANES]
        odn.append(blk * _half_rms_scale(blk, lo_mask) * onorm_ref[...])
    o_delta = jnp.concatenate(odn, axis=1)

    y = y + _dot((gates[:, 0:D_POOL] * o_pool).astype(BF16), wout_ref[0:D_POOL, :])
    y = y + _dot((gates[:, D_POOL + D_ATTN:] * o_delta).astype(BF16), wout_ref[D_POOL + D_ATTN:, :])
    ms = jnp.mean(y * y, axis=-1, keepdims=True)
    yn = (y * lax.rsqrt(ms + EPS)) * gpost_ref[...]
    gate = mod_ref[0][:, 2 * D_MODEL:3 * D_MODEL]
    o_ref[0] = x_ref[0] + gate * yn


def _window_counts(seq_len):
    t = jnp.arange(seq_len)[:, None]
    w = jnp.repeat(jnp.array(POOL_WINDOWS), D_POOL // len(POOL_WINDOWS))[None, :]
    return (jnp.minimum(t - w // 2 + w, seq_len) - jnp.maximum(t - w // 2, 0)).astype(BF16)


def _merge(x, mod, up, gates, oattn, o_f, o_b, pw_bd, pscale, onorm_row, wout, layer, gpost, tm):
    B, L, _ = x.shape
    nb = tm // HALO
    last_blk = L // HALO - 1
    shared_mod = mod.shape[0] == 1
    row = lambda b, i: (b, i, 0)
    const2 = lambda b, i: (0, 0)
    return pl.pallas_call(
        _merge_kernel,
        grid=(B, L // tm),
        in_specs=[pl.BlockSpec((1, tm, D_MODEL), row),
                  pl.BlockSpec((1, 1, 3 * D_MODEL), (lambda b, i: (0, 0, 0)) if shared_mod else (lambda b, i: (b, 0, 0))),
                  pl.BlockSpec((1, tm, D_POOL), row),
                  pl.BlockSpec((1, HALO, D_POOL), lambda b, i: (b, jnp.maximum(i * nb - 1, 0), 0)),
                  pl.BlockSpec((1, HALO, D_POOL), lambda b, i: (b, jnp.minimum((i + 1) * nb, last_blk), 0)),
                  pl.BlockSpec((tm, D_POOL), lambda b, i: (i, 0)),
                  pl.BlockSpec((1, tm, D_MODEL), row),
                  pl.BlockSpec((1, tm, D_ATTN), row),
                  pl.BlockSpec((1, tm, D_DELTA), row),
                  pl.BlockSpec((1, tm, D_DELTA), row),
                  pl.BlockSpec((D_POOL, D_POOL), const2),
                  pl.BlockSpec((1, D_POOL), const2),
                  pl.BlockSpec((1, LANES), const2),
                  pl.BlockSpec((None, D_MODEL, D_MODEL), lambda b, i: (layer, 0, 0)),
                  pl.BlockSpec((1, D_MODEL), const2)],
        out_specs=pl.BlockSpec((1, tm, D_MODEL), row),
        out_shape=jax.ShapeDtypeStruct((B, L, D_MODEL), F32),
        scratch_shapes=[pltpu.VMEM((tm + 2 * HALO, D_POOL), F32)],
        compiler_params=_cparams("parallel", "parallel"),
        name="merge",
    )(x, mod, up, up, up, _window_counts(L), gates, oattn, o_f, o_b, pw_bd, pscale, onorm_row, wout, gpost)


def _rope_tables(num_tokens):
    rows = num_tokens // GRID_W
    row = jnp.repeat(jnp.arange(rows, dtype=F32), GRID_W)
    col = (jnp.arange(rows * GRID_W) % GRID_W).astype(F32)
    axis_dim = HEAD_DIM // 2
    inv = ROPE_THETA ** (-jnp.arange(0, axis_dim, 2, dtype=F32) / axis_dim)
    ang = jnp.concatenate([row[:, None] * inv, col[:, None] * inv], axis=-1)
    cos = jnp.repeat(jnp.cos(ang), 2, axis=-1)
    sin = jnp.repeat(jnp.sin(ang), 2, axis=-1)
    sign = jnp.tile(jnp.array([-1.0, 1.0], F32), HEAD_DIM // 2)
    return jnp.tile(cos, (1, 2)), jnp.tile(sin * sign, (1, 2))


def _w_in_tail(w):
    ba = w[..., OFF_TAIL:OFF_TAIL + 4 * N_DELTA_HEADS]
    g_delta = w[..., OFF_TAIL + 4 * N_DELTA_HEADS:]
    pad = jnp.zeros(w.shape[:-1] + (LANES - 4 * N_DELTA_HEADS,), w.dtype)
    return jnp.concatenate([g_delta, ba, pad], axis=-1).astype(BF16)


def _lane_row(vec, offset):
    return jnp.zeros((1, LANES), F32).at[0, offset:offset + vec.size].set(vec.reshape(-1))


def _cached_kv_variants(k, v):
    lane = jnp.arange(LANES)
    pairs = _kv_variants(k, jnp.roll(k, HEAD_DIM, axis=-1), v, jnp.roll(v, HEAD_DIM, axis=-1), lane)
    return (jnp.concatenate([p[0] for p in pairs], axis=-1).astype(BF16),
            jnp.concatenate([p[1] for p in pairs], axis=-1).astype(BF16))


def _layer(x, mod, lw, rope_tabs, ctx, tiles):
    B, L, _ = x.shape
    latent = ctx is not None
    outs = _inproj(x, mod, lw["gpre"], lw["w_in"], lw["w_in_tail"], lw["layer"], lw["qkn"], lw["conv_w"],
                   lw["alog_row"], lw["dtb_row"], rope_tabs, tiles["tm"])
    up, gates, q, k4, v4, ksq, act, bg = outs[:8]
    sources = [(k4, v4)]
    if latent:
        sources.append(ctx["kv"])
    oattn = _attention(q, lw["qmax"], ksq, sources, tiles["tq"], tiles["tk"])

    gt = bg[:, :, 2 * N_DELTA_HEADS:4 * N_DELTA_HEADS].reshape(B, L // CHUNK, CHUNK, 2, N_DELTA_HEADS)
    gt = gt.transpose(0, 1, 3, 4, 2).reshape(B, L // CHUNK, 2, D_DELTA)
    pm, bm, ol, e = _delta_chunks(act, bg, gt, tiles["tc"])
    s0 = ctx["state"] if latent else jnp.zeros((B, 2, HEAD_DIM, D_DELTA), F32)
    o_f, o_b, s_out = _delta_scan(pm, bm, ol, e, s0, tiles["bb"], tiles["cs"])

    y = _merge(x, mod, up, gates, oattn, o_f, o_b, lw["pw_bd"], lw["pscale"], lw["onorm_row"],
               lw["w_out"], lw["layer"], lw["gpost"], tiles["tmm"])
    if latent:
        return y
    kn, v = outs[8], outs[9]
    s_out = s_out.reshape(B, 2, HEAD_DIM, N_DELTA_HEADS, HEAD_DIM).transpose(0, 1, 3, 2, 4)
    return y, kn, v, s_out


def kernel(x_prompt, x_sample, cache_attn_k, cache_attn_v, state_delta, c, c_ctx, w_mod, b_mod, norm_pre, norm_post,
           w_in, w_out, pool_w, pool_scale, q_norm, k_norm, conv_w, a_log, dt_bias, o_norm):
    B, L, _ = x_prompt.shape
    DB, DL, _ = x_sample.shape
    past = cache_attn_k.shape[2]

    conds = jnp.zeros((8, D_MODEL), F32).at[0].set(c_ctx).at[1:1 + DB].set(c)
    mod = _modulation(conds, w_mod, b_mod)
    rope_tabs = _rope_tables(DL)

    ctx_tiles = dict(tm=256, tmm=256, tq=256, tk=256, tc=256, bb=8, cs=4)
    lat_tiles = dict(tm=1024, tmm=512, tq=256, tk=1024, tc=512, bb=4, cs=8)

    hp, hs = x_prompt, x_sample
    new_k, new_v, new_s = [], [], []
    w_in_all = w_in.astype(BF16)
    w_in_tail = _w_in_tail(w_in)
    w_out_all = w_out.astype(BF16)
    for l in range(DEPTH):
        blocks = jnp.zeros((N_DELTA_HEADS, HEAD_DIM, N_DELTA_HEADS, HEAD_DIM), F32)
        blocks = blocks.at[jnp.arange(4), :, jnp.arange(4), :].set(pool_w[l])
        qkn = jnp.zeros((8, LANES), F32).at[0].set(jnp.tile(q_norm[l], 2)).at[1].set(jnp.tile(k_norm[l], 2))
        lw = dict(
            layer=l,
            gpre=norm_pre[l].reshape(1, D_MODEL),
            gpost=norm_post[l].reshape(1, D_MODEL),
            w_in=w_in_all,
            w_in_tail=w_in_tail,
            w_out=w_out_all,
            qkn=qkn,
            conv_w=conv_w[l],
            alog_row=_lane_row(a_log[l], 2 * N_DELTA_HEADS),
            dtb_row=_lane_row(dt_bias[l], 2 * N_DELTA_HEADS),
            pw_bd=blocks.reshape(D_POOL, D_POOL).astype(BF16),
            pscale=pool_scale[l].reshape(1, D_POOL),
            onorm_row=jnp.tile(o_norm[l], 2).reshape(1, LANES),
            qmax=jnp.full((1, LANES), HEAD_DIM ** 0.5 * Q_SCALE * 1.01, F32) * jnp.max(jnp.abs(q_norm[l])),
        )
        hp, k_l, v_l, s_l = _layer(hp, mod[l, 0:1].reshape(1, 1, 3 * D_MODEL), lw, None, None, ctx_tiles)
        ctx = dict(kv=_cached_kv_variants(cache_attn_k[:, l].reshape(DB, past, D_KV),
                                          cache_attn_v[:, l].reshape(DB, past, D_KV)),
                   state=state_delta[:, l].transpose(0, 1, 3, 2, 4).reshape(DB, 2, HEAD_DIM, D_DELTA))
        hs = _layer(hs, mod[l, 1:1 + DB].reshape(DB, 1, 3 * D_MODEL), lw, rope_tabs, ctx, lat_tiles)
        new_k.append(k_l.reshape(B, L, 2, HEAD_DIM))
        new_v.append(v_l.reshape(B, L, 2, HEAD_DIM))
        new_s.append(s_l)
    return (hp, hs, jnp.stack(new_k, axis=1), jnp.stack(new_v, axis=1), jnp.stack(new_s, axis=1))
```

```python
import functools

import jax
import jax.numpy as jnp
from jax import lax
from jax.experimental import pallas as pl
from jax.experimental.pallas import tpu as pltpu

F32 = jnp.float32
BF16 = jnp.bfloat16

D_MODEL = 1024
DEPTH = 2
GRID_W = 64
HEAD_DIM = 64
D_POOL = 256
D_ATTN = 512
D_DELTA = 256
D_KV = 128
N_Q_HEADS = 8
N_DELTA_HEADS = 4
POOL_WINDOWS = (2, 4, 8, 16)
CHUNK = 64
ROPE_THETA = 10000.0
EPS = 1e-6
LANES = 128
HALO = 8
MASK_VALUE = -1e30
LOG2E = 1.4426950408889634
Q_SCALE = HEAD_DIM ** -0.5 * LOG2E
SHIFT_LIMIT = 60.0

D_IN = 2832
OFF_UP = 0
OFF_GPOOL = 256
OFF_QKV = 512
OFF_GATTN = 1280
OFF_QKVD = 1792
OFF_TAIL = 2560
TAIL_GDELTA = 0
TAIL_BA = 256
D_TAIL = 384

VMEM_LIMIT = 56 * 1024 * 1024


def _cparams(*sem):
    return pltpu.CompilerParams(dimension_semantics=sem, vmem_limit_bytes=VMEM_LIMIT)


def _dot(a, b):
    return jnp.dot(a, b, preferred_element_type=F32)


def _dot_nt(a, b):
    return lax.dot_general(a, b, (((1,), (1,)), ((), ())), preferred_element_type=F32)


def _dot_tn(a, b):
    return lax.dot_general(a, b, (((0,), (0,)), ((), ())), preferred_element_type=F32)


def _split(x):
    hi = x.astype(BF16)
    return hi, (x - hi.astype(F32)).astype(BF16)


def _silu(x):
    return x * jax.nn.sigmoid(x)


def _softplus(x):
    return jnp.maximum(x, 0.0) + jnp.log1p(jnp.exp(-jnp.abs(x)))


def _kv_variants(k, k_sw, v, v_sw, lane):
    lo = lane < HEAD_DIM
    one_hi = (lane == HEAD_DIM).astype(k.dtype)
    one_lo = (lane == 0).astype(k.dtype)
    ones = jnp.ones_like(v)
    return ((jnp.where(lo, k, one_hi), jnp.where(lo, v, ones)),
            (jnp.where(lo, one_lo, k_sw), jnp.where(lo, ones, v_sw)),
            (jnp.where(lo, k_sw, one_hi), jnp.where(lo, v_sw, ones)),
            (jnp.where(lo, one_lo, k), jnp.where(lo, ones, v)))


def _half_rms_scale(x, lo_mask):
    sq = x * x
    ss_lo = jnp.sum(jnp.where(lo_mask, sq, 0.0), axis=-1, keepdims=True)
    ss_hi = jnp.sum(jnp.where(lo_mask, 0.0, sq), axis=-1, keepdims=True)
    r_lo = lax.rsqrt(ss_lo * (1.0 / HEAD_DIM) + EPS)
    r_hi = lax.rsqrt(ss_hi * (1.0 / HEAD_DIM) + EPS)
    return jnp.where(lo_mask, r_lo, r_hi)


def _mod_kernel(c_ref, w_ref, b_ref, o_ref):
    s = _silu(c_ref[...])
    o_ref[0] = _dot(s.astype(BF16), w_ref[0].astype(BF16)) + b_ref[0]


def _modulation(conds, w_mod, b_mod):
    tn = 768
    return pl.pallas_call(
        _mod_kernel,
        grid=(DEPTH, 3 * D_MODEL // tn),
        in_specs=[pl.BlockSpec((8, D_MODEL), lambda l, j: (0, 0)),
                  pl.BlockSpec((1, D_MODEL, tn), lambda l, j: (l, 0, j)),
                  pl.BlockSpec((1, 1, tn), lambda l, j: (l, 0, j))],
        out_specs=pl.BlockSpec((1, 8, tn), lambda l, j: (l, 0, j)),
        out_shape=jax.ShapeDtypeStruct((DEPTH, 8, 3 * D_MODEL), F32),
        compiler_params=_cparams("parallel", "parallel"),
        name="modulation",
    )(conds, w_mod, b_mod.reshape(DEPTH, 1, 3 * D_MODEL))


def _inproj_kernel(rope, x_ref, xprev_ref, xnext_ref, mod_ref, gpre_ref, w_ref, wtail_ref, qkn_ref, convw_ref, alog_ref,
                   dtb_ref, *rest):
    if rope:
        cos_ref, sin_ref, up_ref, gates_ref, q_ref, k4_ref, v4_ref, ksq_ref, act_ref, bg_ref, pad_ref = rest
    else:
        up_ref, gates_ref, q_ref, k4_ref, v4_ref, ksq_ref, act_ref, bg_ref, kn_ref, v_ref, pad_ref = rest
    i = pl.program_id(1)
    tm = x_ref.shape[1]
    mod = mod_ref[0]
    shift = mod[:, 0:D_MODEL]
    scale = mod[:, D_MODEL:2 * D_MODEL]

    def modulated_norm(x):
        ms = jnp.mean(x * x, axis=-1, keepdims=True)
        return ((x * lax.rsqrt(ms + EPS)) * gpre_ref[...] * (1.0 + scale) + shift).astype(BF16)

    hb_all = modulated_norm(jnp.concatenate([x_ref[0], xprev_ref[0], xnext_ref[0]], axis=0))
    hb = hb_all[0:tm]

    zqkv = _dot(hb, w_ref[:, OFF_QKV:OFF_QKV + D_ATTN + 2 * D_KV])
    zqk, v = zqkv[:, 0:D_ATTN + D_KV], zqkv[:, D_ATTN + D_KV:]
    zd = _dot(hb_all, w_ref[:, OFF_QKVD:OFF_QKVD + 3 * D_DELTA])
    pad_ref[0:HALO, :] = jnp.where(i == 0, 0.0, zd[tm:tm + HALO])
    pad_ref[HALO:HALO + tm, :] = zd[0:tm]
    pad_ref[HALO + tm:2 * HALO + tm, :] = jnp.where(i == pl.num_programs(1) - 1, 0.0, zd[tm + HALO:tm + 2 * HALO])
    zg_pool = _dot(hb, w_ref[:, OFF_GPOOL:OFF_GPOOL + D_POOL])
    zg_attn = _dot(hb, w_ref[:, OFF_GATTN:OFF_GATTN + D_ATTN])
    zg_delta = _dot(hb, wtail_ref[:, TAIL_GDELTA:TAIL_GDELTA + D_DELTA])

    lane = lax.broadcasted_iota(jnp.int32, (tm, LANES), 1)
    lo_mask = lane < HEAD_DIM
    even = (lane % 2) == 0

    def head_norm_rope(blk, gain):
        y = blk * _half_rms_scale(blk, lo_mask) * gain
        if rope:
            swapped = jnp.where(even, pltpu.roll(y, LANES - 1, 1), pltpu.roll(y, 1, 1))
            return y, y * cos_ref[...] + swapped * sin_ref[...]
        return y, y

    for qb in range(D_ATTN // LANES):
        _, qr = head_norm_rope(zqk[:, qb * LANES:(qb + 1) * LANES], qkn_ref[0:1, :])
        qs = qr * Q_SCALE
        q_ref[0, :, 2 * qb * LANES:(2 * qb + 1) * LANES] = jnp.where(lo_mask, qs, 0.0).astype(BF16)
        q_ref[0, :, (2 * qb + 1) * LANES:(2 * qb + 2) * LANES] = jnp.where(lo_mask, 0.0, qs).astype(BF16)
    gates_ref[0, :, 0:D_POOL] = _silu(zg_pool).astype(BF16)
    gates_ref[0, :, D_POOL:D_POOL + D_ATTN] = _silu(zg_attn).astype(BF16)
    gates_ref[0, :, D_POOL + D_ATTN:] = _silu(zg_delta).astype(BF16)
    up_ref[0] = _dot(hb, w_ref[:, OFF_UP:OFF_UP + D_POOL])
    ba = _dot(hb, wtail_ref[:, TAIL_BA:TAIL_BA + LANES])

    xp = pad_ref[...]
    rows = xp.shape[0]
    acc = None
    for j in range(4):
        shifted = xp if j == 2 else pltpu.roll(xp, (2 - j) % rows, 0)
        term = convw_ref[j:j + 1, :] * shifted[HALO:HALO + tm, :]
        acc = term if acc is None else acc + term
    act_ref[0] = _silu(acc)
    g = -jnp.exp(alog_ref[...]) * _softplus(ba + dtb_ref[...])
    bg_ref[0] = jnp.where(lane < 2 * N_DELTA_HEADS, jax.nn.sigmoid(ba), jnp.where(lane < 4 * N_DELTA_HEADS, g, 0.0))

    kn, kr = head_norm_rope(zqk[:, D_ATTN:D_ATTN + D_KV], qkn_ref[1:2, :])
    for a, (kx, vx) in enumerate(_kv_variants(kr, pltpu.roll(kr, HEAD_DIM, 1), v, pltpu.roll(v, HEAD_DIM, 1), lane)):
        k4_ref[0, :, a * LANES:(a + 1) * LANES] = kx.astype(BF16)
        v4_ref[0, :, a * LANES:(a + 1) * LANES] = vx.astype(BF16)
    ksq = kr * kr
    ksq = jnp.where(lo_mask, jnp.sum(jnp.where(lo_mask, ksq, 0.0), axis=-1, keepdims=True),
                    jnp.sum(jnp.where(lo_mask, 0.0, ksq), axis=-1, keepdims=True))
    ksq_ref[0, 0] = jnp.broadcast_to(jnp.max(ksq, axis=0, keepdims=True), (HALO, LANES))
    if not rope:
        kn_ref[0] = kn
        v_ref[0] = v


def _inproj(x, mod, gpre, w, wtail, layer, qkn, conv_w, alog_row, dtb_row, rope_tabs, tm):
    B, L, _ = x.shape
    rope = rope_tabs is not None
    shared_mod = mod.shape[0] == 1
    nb = tm // HALO
    last_blk = L // HALO - 1
    row = lambda b, i: (b, i, 0)
    const2 = lambda b, i: (0, 0)
    in_specs = [pl.BlockSpec((1, tm, D_MODEL), row),
                pl.BlockSpec((1, HALO, D_MODEL), lambda b, i: (b, jnp.maximum(i * nb - 1, 0), 0)),
                pl.BlockSpec((1, HALO, D_MODEL), lambda b, i: (b, jnp.minimum((i + 1) * nb, last_blk), 0)),
                pl.BlockSpec((1, 1, 3 * D_MODEL), (lambda b, i: (0, 0, 0)) if shared_mod else (lambda b, i: (b, 0, 0))),
                pl.BlockSpec((1, D_MODEL), const2),
                pl.BlockSpec((None, D_MODEL, D_IN), lambda b, i: (layer, 0, 0)),
                pl.BlockSpec((None, D_MODEL, D_TAIL), lambda b, i: (layer, 0, 0)),
                pl.BlockSpec((8, LANES), const2),
                pl.BlockSpec((4, 3 * D_DELTA), const2),
                pl.BlockSpec((1, LANES), const2),
                pl.BlockSpec((1, LANES), const2)]
    args = [x, x, x, mod, gpre, w, wtail, qkn, conv_w, alog_row, dtb_row]
    if rope:
        in_specs += [pl.BlockSpec((tm, LANES), lambda b, i: (i, 0))] * 2
        args += list(rope_tabs)
    widths = [(D_POOL, F32), (D_MODEL, BF16), (N_Q_HEADS * LANES, BF16), (4 * LANES, BF16), (4 * LANES, BF16), None,
              (3 * D_DELTA, F32), (LANES, F32)]
    if not rope:
        widths += [(D_KV, F32), (D_KV, F32)]
    out_specs = [pl.BlockSpec((1, tm, wd[0]), row) if wd else pl.BlockSpec((1, 1, HALO, LANES), lambda b, i: (b, i, 0, 0))
                 for wd in widths]
    out_shape = [jax.ShapeDtypeStruct((B, L, wd[0]), wd[1]) if wd else jax.ShapeDtypeStruct((B, L // tm, HALO, LANES), F32)
                 for wd in widths]
    return pl.pallas_call(
        functools.partial(_inproj_kernel, rope),
        grid=(B, L // tm),
        in_specs=in_specs,
        out_specs=out_specs,
        out_shape=out_shape,
        scratch_shapes=[pltpu.VMEM((tm + 2 * HALO, 3 * D_DELTA), F32)],
        compiler_params=_cparams("parallel", "parallel"),
        name="inproj_rope" if rope else "inproj",
    )(*args)


_VARIANT_HEADS = ((0, 2), (1, 3), (4, 6), (5, 7))


def _attn_kernel(n_src, tk, q_ref, qmax_ref, ksq_ref, *rest):
    kv_refs = rest[:2 * n_src]
    o_ref, acc_ref, m_ref = rest[2 * n_src:]
    tq = q_ref.shape[1]
    n_var = len(_VARIANT_HEADS)
    lane = lax.broadcasted_iota(jnp.int32, (tq, LANES), 1)
    lo_mask = lane < HEAD_DIM

    ksq = jnp.max(ksq_ref[0], axis=0)[0:1, :]
    lo_row = lo_mask[0:1, :]
    ksq_head = [jnp.max(jnp.where(lo_row, ksq, 0.0), axis=-1, keepdims=True),
                jnp.max(jnp.where(lo_row, 0.0, ksq), axis=-1, keepdims=True)]
    for j in range(1, n_src):
        for kv_head in range(2):
            kx = kv_refs[2 * j][0, :, 2 * kv_head * LANES:(2 * kv_head + 1) * LANES].astype(F32)
            sq = jnp.sum(jnp.where(lane[0:1, :] < HEAD_DIM, kx * kx, 0.0), axis=-1, keepdims=True)
            ksq_head[kv_head] = jnp.maximum(ksq_head[kv_head], jnp.max(sq, axis=0, keepdims=True))

    bound = [qmax_ref[0:1, 0:1] * jnp.sqrt(ksq_head[kv_head]) for kv_head in range(2)]
    safe = jnp.max(jnp.maximum(bound[0], bound[1])) <= SHIFT_LIMIT

    def head_block(h):
        return q_ref[0, :, h * LANES:(h + 1) * LANES]

    def shifted_block(h):
        blk = head_block(h)
        return jnp.where(lane == (HEAD_DIM if h % 2 == 0 else 0), (-bound[h // 4]).astype(BF16), blk)

    q_shift = [jnp.concatenate([shifted_block(h) for h in heads], axis=0) for heads in _VARIANT_HEADS]

    cols = [slice(a * LANES, (a + 1) * LANES) for a in range(n_var)]
    chunks = []
    for j in range(n_src):
        k4_ref, v4_ref = kv_refs[2 * j], kv_refs[2 * j + 1]
        S = k4_ref.shape[1]
        size = min(tk, S)
        chunks += [(k4_ref, v4_ref, c * size, size) for c in range(S // size)]

    @pl.when(safe)
    def _():
        def scores(ch):
            k4_ref, _, start, size = ch
            return [_dot_nt(q_shift[a], k4_ref[0, start:start + size, cols[a]]) for a in range(n_var)]

        acc = [None] * n_var
        s_next = scores(chunks[0])
        for ci, (_, v4_ref, start, size) in enumerate(chunks):
            s = s_next
            if ci + 1 < len(chunks):
                s_next = scores(chunks[ci + 1])
            for a in range(n_var):
                pv = _dot(jnp.exp2(s[a]).astype(BF16), v4_ref[0, start:start + size, cols[a]])
                acc[a] = pv if acc[a] is None else acc[a] + pv
        for a in range(n_var):
            acc_ref[a] = acc[a]

    @pl.when(jnp.logical_not(safe))
    def _():
        m_ref[...] = jnp.full(m_ref.shape, MASK_VALUE, F32)
        acc_ref[...] = jnp.zeros(acc_ref.shape, F32)
        q_plain = [jnp.concatenate([head_block(h) for h in heads], axis=0) for heads in _VARIANT_HEADS]
        for k4_ref, v4_ref, start, size in chunks:
            for a in range(n_var):
                s = _dot_nt(q_plain[a], k4_ref[0, start:start + size, cols[a]])
                m_prev = m_ref[a]
                m_new = jnp.maximum(m_prev, jnp.max(s, axis=-1, keepdims=True))
                p = jnp.exp2(s - jnp.tile(m_new, (1, size // LANES)))
                m_ref[a] = m_new
                acc_ref[a] = jnp.exp2(m_prev - m_new) * acc_ref[a] + _dot(p.astype(BF16), v4_ref[0, start:start + size, cols[a]])

    def head_out(h):
        a = 2 * (h // 4) + h % 2
        r = _VARIANT_HEADS[a].index(h) * tq
        rows = acc_ref[a, r:r + tq, :]
        return rows / pltpu.roll(rows, HEAD_DIM, 1)

    for i in range(N_Q_HEADS // 2):
        o_ref[0, :, i * LANES:(i + 1) * LANES] = jnp.where(lo_mask, head_out(2 * i), head_out(2 * i + 1)).astype(BF16)


def _attention(q, qmax, ksq, sources, tq, tk):
    B, L, _ = q.shape
    n_var = len(_VARIANT_HEADS)
    in_specs = [pl.BlockSpec((1, tq, N_Q_HEADS * LANES), lambda b, i: (b, i, 0)),
                pl.BlockSpec((1, LANES), lambda b, i: (0, 0)),
                pl.BlockSpec((1,) + ksq.shape[1:], lambda b, i: (b, 0, 0, 0))]
    args = [q, qmax, ksq]
    for k4, v4 in sources:
        S = k4.shape[1]
        in_specs += [pl.BlockSpec((1, S, n_var * LANES), lambda b, i: (b, 0, 0))] * 2
        args += [k4, v4]
    rows = N_Q_HEADS // n_var * tq
    return pl.pallas_call(
        functools.partial(_attn_kernel, len(sources), tk),
        grid=(B, L // tq),
        in_specs=in_specs,
        out_specs=pl.BlockSpec((1, tq, D_ATTN), lambda b, i: (b, i, 0)),
        out_shape=jax.ShapeDtypeStruct((B, L, D_ATTN), BF16),
        scratch_shapes=[pltpu.VMEM((n_var, rows, LANES), F32)] * 2,
        compiler_params=_cparams("parallel", "parallel"),
        name="attention",
    )(*args)


def _block_diag(x, bd_mask):
    return jnp.where(bd_mask, jnp.concatenate([x] * N_DELTA_HEADS, axis=0), jnp.zeros((), x.dtype))


def _dot_split_bd(a, b, bd_mask):
    (ah, al), (bh, bl) = a, b
    m = ah.shape[0]
    both = _dot(jnp.concatenate([ah, al], axis=0), _block_diag(bh, bd_mask))
    return both[0:m] + both[m:2 * m] + _dot(ah, _block_diag(bl, bd_mask))


def _delta_chunk_kernel(act_ref, bg_ref, gt_ref, pm_ref, bm_ref, ol_ref, e_ref):
    C, W, H = CHUNK, D_DELTA, N_DELTA_HEADS
    n_chunks = act_ref.shape[1] // C
    ri = lax.broadcasted_iota(jnp.int32, (C, W), 0)
    lane = lax.broadcasted_iota(jnp.int32, (C, W), 1)
    lj = lane % HEAD_DIM
    blk = lane // HEAD_DIM
    bd_mask = (lax.broadcasted_iota(jnp.int32, (W, W), 0) // HEAD_DIM) == (lax.broadcasted_iota(jnp.int32, (W, W), 1) // HEAD_DIM)
    diag = ri == lj
    eye = diag.astype(F32)

    def expand(cols):
        res = cols[H - 1]
        for h in range(H - 2, -1, -1):
            res = jnp.where(blk == h, cols[h], res)
        return res

    def seg_sum(x):
        return expand([jnp.sum(jnp.where(blk == h, x, 0.0), axis=-1, keepdims=True) for h in range(H)])

    def l2n(x):
        return x * lax.rsqrt(seg_sum(x * x) + EPS)

    def row_form(x):
        res = x[(H - 1) * C:H * C]
        for h in range(H - 2, -1, -1):
            res = jnp.where(blk == h, x[h * C:(h + 1) * C], res)
        return res

    units = []
    for c in range(n_chunks):
        rows = slice(c * C, (c + 1) * C)
        q = l2n(act_ref[0, rows, 0:W]) * (HEAD_DIM ** -0.5)
        k = l2n(act_ref[0, rows, W:2 * W])
        v = act_ref[0, rows, 2 * W:3 * W]
        kb = k.astype(BF16)
        kkqk = _dot_nt(jnp.concatenate([kb, q.astype(BF16)], axis=0), _block_diag(kb, bd_mask))
        kk, qk = kkqk[0:C], kkqk[C:2 * C]
        bg = bg_ref[0, rows, :]
        gt = gt_ref[0, c]
        for d in range(2):
            beta = expand([bg[:, d * H + h:d * H + h + 1] for h in range(H)])
            g = expand([bg[:, 2 * H + d * H + h:2 * H + d * H + h + 1] for h in range(H)])
            g_row = gt[d:d + 1, :]
            incl = (ri >= lj) if d == 0 else (ri <= lj)
            incl_t = (ri <= lj) if d == 0 else (ri >= lj)
            strict = (ri > lj) if d == 0 else (ri < lj)
            gc = seg_sum(jnp.where(incl, g_row, 0.0))
            gc_row = jnp.sum(jnp.where(incl_t, g, 0.0), axis=0, keepdims=True)
            g_tot = jnp.sum(g, axis=0, keepdims=True)
            decay = jnp.where(incl, jnp.exp(jnp.where(incl, gc - gc_row, 0.0)), 0.0)
            t = jnp.where(strict, -(beta * kk * decay), 0.0)
            units.append(dict(c=c, d=d, q=q, k=k, v=v, beta=beta, gc=gc, g_tot=g_tot,
                              attn=(qk * decay).astype(BF16), t=t, p=eye + t))

    for un in units:
        un["t"] = _dot_split_bd(_split(un["t"]), _split(un["t"]), bd_mask)
    for _ in range(4):
        for un in units:
            both = _dot_split_bd(_split(jnp.concatenate([un["t"], un["p"]], axis=0)), _split(un["t"]), bd_mask)
            un["t"] = both[0:C]
            un["p"] = un["p"] + both[C:2 * C]
    for un in units:
        un["p"] = (un["p"] + _dot(un["p"].astype(BF16), _block_diag(un["t"].astype(BF16), bd_mask))).astype(BF16)
    for un in units:
        un["egc"] = jnp.exp(un["gc"])
        un["u"] = _dot(un["p"], _block_diag((un["v"] * un["beta"]).astype(BF16), bd_mask)).astype(BF16)
        un["w"] = _dot(un["p"], _block_diag((un["k"] * (un["beta"] * un["egc"])).astype(BF16), bd_mask)).astype(BF16)
    for un in units:
        c, d = un["c"], un["d"]
        kdec = (un["k"] * jnp.exp(un["g_tot"] - un["gc"])).astype(BF16)
        bm_ref[0, d, c] = row_form(_dot_tn(kdec, un["u"])).astype(BF16)
        mp = row_form(_dot_tn(kdec, un["w"]))
        qt = un["q"] * un["egc"] - _dot(un["attn"], _block_diag(un["w"], bd_mask))
        ol_ref[0, d, c] = _dot(un["attn"], _block_diag(un["u"], bd_mask)).astype(BF16)
        pm_ref[0, d, c] = jnp.concatenate([mp, qt], axis=0).astype(BF16)
        e_ref[0, d, c] = jnp.broadcast_to(jnp.exp(un["g_tot"]), (HALO, W))


def _delta_chunks(act, bg, gt, tc):
    B, L, _ = act.shape
    N = L // CHUNK
    cb = tc // CHUNK
    W = D_DELTA
    ospec = lambda r: pl.BlockSpec((1, 2, cb, r, W), lambda b, i: (b, 0, i, 0, 0))
    return pl.pallas_call(
        _delta_chunk_kernel,
        grid=(B, L // tc),
        in_specs=[pl.BlockSpec((1, tc, 3 * W), lambda b, i: (b, i, 0)),
                  pl.BlockSpec((1, tc, LANES), lambda b, i: (b, i, 0)),
                  pl.BlockSpec((1, cb, 2, W), lambda b, i: (b, i, 0, 0))],
        out_specs=[ospec(2 * CHUNK), ospec(CHUNK), ospec(CHUNK), ospec(HALO)],
        out_shape=[jax.ShapeDtypeStruct((B, 2, N, 2 * CHUNK, W), BF16),
                   jax.ShapeDtypeStruct((B, 2, N, CHUNK, W), BF16),
                   jax.ShapeDtypeStruct((B, 2, N, CHUNK, W), BF16),
                   jax.ShapeDtypeStruct((B, 2, N, HALO, W), F32)],
        compiler_params=_cparams("parallel", "parallel"),
        name="delta_chunks",
    )(act, bg, gt)


def _delta_scan_kernel(pmf_ref, bmf_ref, olf_ref, ef_ref, pmb_ref, bmb_ref, olb_ref, eb_ref, s0_ref,
                       of_ref, ob_ref, sout_ref, s_ref):
    n = pl.program_id(1)
    bb = s_ref.shape[0]
    C, W = CHUNK, D_DELTA
    bd_mask = (lax.broadcasted_iota(jnp.int32, (W, W), 0) // HEAD_DIM) == (lax.broadcasted_iota(jnp.int32, (W, W), 1) // HEAD_DIM)

    @pl.when(n == 0)
    def _():
        s_ref[...] = s0_ref[...]

    dirs = ((pmf_ref, bmf_ref, olf_ref, ef_ref, of_ref), (pmb_ref, bmb_ref, olb_ref, eb_ref, ob_ref))
    cs = pmf_ref.shape[2]
    chains = [(b, d) for b in range(bb) for d in range(2)]
    s = [s_ref[b, d] for b, d in chains]
    for step in range(cs):
        idx = [step if d == 0 else cs - 1 - step for _, d in chains]
        r = [_dot(dirs[d][0][b, 0, idx[i]], _block_diag(s[i].astype(BF16), bd_mask)) for i, (b, d) in enumerate(chains)]
        for i, (b, d) in enumerate(chains):
            _, bm_ref, ol_ref, e_ref, o_ref = dirs[d]
            s[i] = e_ref[b, 0, idx[i]][0:1, :] * s[i] + bm_ref[b, 0, idx[i]].astype(F32) - r[i][0:C]
            o_ref[b, idx[i] * C:(idx[i] + 1) * C, :] = (r[i][C:2 * C] + ol_ref[b, 0, idx[i]].astype(F32)).astype(BF16)
    for i, (b, d) in enumerate(chains):
        s_ref[b, d] = s[i]

    @pl.when(n == pl.num_programs(1) - 1)
    def _():
        sout_ref[...] = s_ref[...]


def _delta_scan(pm, bm, ol, e, s0, bb, cs):
    B, _, N, _, W = pm.shape
    L = N * CHUNK
    steps = N // cs

    def spec(r, d):
        if d == 0:
            return pl.BlockSpec((bb, 1, cs, r, W), lambda b, n: (b, 0, n, 0, 0))
        return pl.BlockSpec((bb, 1, cs, r, W), lambda b, n: (b, 1, steps - 1 - n, 0, 0))

    in_specs = []
    for d in range(2):
        in_specs += [spec(2 * CHUNK, d), spec(CHUNK, d), spec(CHUNK, d), spec(HALO, d)]
    in_specs.append(pl.BlockSpec((bb, 2, CHUNK, W), lambda b, n: (b, 0, 0, 0)))
    return pl.pallas_call(
        _delta_scan_kernel,
        grid=(B // bb, steps),
        in_specs=in_specs,
        out_specs=[pl.BlockSpec((bb, cs * CHUNK, W), lambda b, n: (b, n, 0)),
                   pl.BlockSpec((bb, cs * CHUNK, W), lambda b, n: (b, steps - 1 - n, 0)),
                   pl.BlockSpec((bb, 2, CHUNK, W), lambda b, n: (b, 0, 0, 0))],
        out_shape=[jax.ShapeDtypeStruct((B, L, W), BF16),
                   jax.ShapeDtypeStruct((B, L, W), BF16),
                   jax.ShapeDtypeStruct((B, 2, CHUNK, W), F32)],
        scratch_shapes=[pltpu.VMEM((bb, 2, CHUNK, W), F32)],
        compiler_params=_cparams("parallel", "arbitrary"),
        name="delta_scan",
    )(pm, bm, ol, e, pm, bm, ol, e, s0)


def _merge_kernel(x_ref, mod_ref, up_ref, upp_ref, upn_ref, cnt_ref, gates_ref, oattn_ref, of_ref, ob_ref,
                  pw_ref, pscale_ref, onorm_ref, wout_ref, gpost_ref, o_ref, pad_ref):
    i = pl.program_id(1)
    tm = x_ref.shape[1]
    first = i == 0
    last = i == pl.num_programs(1) - 1
    gates = gates_ref[0].astype(F32)
    y = _dot((gates[:, D_POOL:D_POOL + D_ATTN] * oattn_ref[0].astype(F32)).astype(BF16),
             wout_ref[D_POOL:D_POOL + D_ATTN, :])
    pad_ref[0:HALO, :] = jnp.where(first, 0.0, upp_ref[0])
    pad_ref[HALO:HALO + tm, :] = up_ref[0]
    pad_ref[HALO + tm:2 * HALO + tm, :] = jnp.where(last, 0.0, upn_ref[0])

    def window_sum(col, lo, hi):
        padded = pad_ref[:, col]
        rows = padded.shape[0]
        acc = None
        for j in range(lo, hi):
            term = (padded if j == 0 else pltpu.roll(padded, (-j) % rows, 0))[HALO:HALO + tm, :]
            acc = term if acc is None else acc + term
        return acc

    lane = lax.broadcasted_iota(jnp.int32, (tm, LANES), 1)
    lo_mask = lane < HEAD_DIM

    pooled = []
    for col_blk, (w_lo, w_hi) in enumerate(((2, 4), (8, 16))):
        col = slice(col_blk * LANES, (col_blk + 1) * LANES)
        s_lo = window_sum(col, -(w_lo // 2), w_lo // 2)
        s_hi = s_lo + window_sum(col, -(w_hi // 2), -(w_lo // 2)) + window_sum(col, w_lo // 2, w_hi // 2)
        mean = jnp.where(lo_mask, s_lo, s_hi) / cnt_ref[:, col].astype(F32)
        pooled.append(mean - up_ref[0, :, col])
    pooled = jnp.concatenate(pooled, axis=1)
    o_pool = _dot(pooled.astype(BF16), pw_ref[...]) * pscale_ref[...]

    od = of_ref[0].astype(F32) + ob_ref[0].astype(F32)
    odn = []
    for j in range(D_DELTA // LANES):
        blk = od[:, j * LANES:(j + 1) * LANES]
        odn.append(blk * _half_rms_scale(blk, lo_mask) * onorm_ref[...])
    o_delta = jnp.concatenate(odn, axis=1)

    y = y + _dot((gates[:, 0:D_POOL] * o_pool).astype(BF16), wout_ref[0:D_POOL, :])
    y = y + _dot((gates[:, D_POOL + D_ATTN:] * o_delta).astype(BF16), wout_ref[D_POOL + D_ATTN:, :])
    ms = jnp.mean(y * y, axis=-1, keepdims=True)
    yn = (y * lax.rsqrt(ms + EPS)) * gpost_ref[...]
    gate = mod_ref[0][:, 2 * D_MODEL:3 * D_MODEL]
    o_ref[0] = x_ref[0] + gate * yn


def _window_counts(seq_len):
    t = jnp.arange(seq_len)[:, None]
    w = jnp.repeat(jnp.array(POOL_WINDOWS), D_POOL // len(POOL_WINDOWS))[None, :]
    return (jnp.minimum(t - w // 2 + w, seq_len) - jnp.maximum(t - w // 2, 0)).astype(BF16)


def _merge(x, mod, up, gates, oattn, o_f, o_b, pw_bd, pscale, onorm_row, wout, layer, gpost, tm):
    B, L, _ = x.shape
    nb = tm // HALO
    last_blk = L // HALO - 1
    shared_mod = mod.shape[0] == 1
    row = lambda b, i: (b, i, 0)
    const2 = lambda b, i: (0, 0)
    return pl.pallas_call(
        _merge_kernel,
        grid=(B, L // tm),
        in_specs=[pl.BlockSpec((1, tm, D_MODEL), row),
                  pl.BlockSpec((1, 1, 3 * D_MODEL), (lambda b, i: (0, 0, 0)) if shared_mod else (lambda b, i: (b, 0, 0))),
                  pl.BlockSpec((1, tm, D_POOL), row),
                  pl.BlockSpec((1, HALO, D_POOL), lambda b, i: (b, jnp.maximum(i * nb - 1, 0), 0)),
                  pl.BlockSpec((1, HALO, D_POOL), lambda b, i: (b, jnp.minimum((i + 1) * nb, last_blk), 0)),
                  pl.BlockSpec((tm, D_POOL), lambda b, i: (i, 0)),
                  pl.BlockSpec((1, tm, D_MODEL), row),
                  pl.BlockSpec((1, tm, D_ATTN), row),
                  pl.BlockSpec((1, tm, D_DELTA), row),
                  pl.BlockSpec((1, tm, D_DELTA), row),
                  pl.BlockSpec((D_POOL, D_POOL), const2),
                  pl.BlockSpec((1, D_POOL), const2),
                  pl.BlockSpec((1, LANES), const2),
                  pl.BlockSpec((None, D_MODEL, D_MODEL), lambda b, i: (layer, 0, 0)),
                  pl.BlockSpec((1, D_MODEL), const2)],
        out_specs=pl.BlockSpec((1, tm, D_MODEL), row),
        out_shape=jax.ShapeDtypeStruct((B, L, D_MODEL), F32),
        scratch_shapes=[pltpu.VMEM((tm + 2 * HALO, D_POOL), F32)],
        compiler_params=_cparams("parallel", "parallel"),
        name="merge",
    )(x, mod, up, up, up, _window_counts(L), gates, oattn, o_f, o_b, pw_bd, pscale, onorm_row, wout, gpost)


def _rope_tables(num_tokens):
    rows = num_tokens // GRID_W
    row = jnp.repeat(jnp.arange(rows, dtype=F32), GRID_W)
    col = (jnp.arange(rows * GRID_W) % GRID_W).astype(F32)
    axis_dim = HEAD_DIM // 2
    inv = ROPE_THETA ** (-jnp.arange(0, axis_dim, 2, dtype=F32) / axis_dim)
    ang = jnp.concatenate([row[:, None] * inv, col[:, None] * inv], axis=-1)
    cos = jnp.repeat(jnp.cos(ang), 2, axis=-1)
    sin = jnp.repeat(jnp.sin(ang), 2, axis=-1)
    sign = jnp.tile(jnp.array([-1.0, 1.0], F32), HEAD_DIM // 2)
    return jnp.tile(cos, (1, 2)), jnp.tile(sin * sign, (1, 2))


def _w_in_tail(w):
    ba = w[..., OFF_TAIL:OFF_TAIL + 4 * N_DELTA_HEADS]
    g_delta = w[..., OFF_TAIL + 4 * N_DELTA_HEADS:]
    pad = jnp.zeros(w.shape[:-1] + (LANES - 4 * N_DELTA_HEADS,), w.dtype)
    return jnp.concatenate([g_delta, ba, pad], axis=-1).astype(BF16)


def _lane_row(vec, offset):
    return jnp.zeros((1, LANES), F32).at[0, offset:offset + vec.size].set(vec.reshape(-1))


def _cached_kv_variants(k, v):
    lane = jnp.arange(LANES)
    pairs = _kv_variants(k, jnp.roll(k, HEAD_DIM, axis=-1), v, jnp.roll(v, HEAD_DIM, axis=-1), lane)
    return (jnp.concatenate([p[0] for p in pairs], axis=-1).astype(BF16),
            jnp.concatenate([p[1] for p in pairs], axis=-1).astype(BF16))


def _layer(x, mod, lw, rope_tabs, ctx, tiles):
    B, L, _ = x.shape
    latent = ctx is not None
    outs = _inproj(x, mod, lw["gpre"], lw["w_in"], lw["w_in_tail"], lw["layer"], lw["qkn"], lw["conv_w"],
                   lw["alog_row"], lw["dtb_row"], rope_tabs, tiles["tm"])
    up, gates, q, k4, v4, ksq, act, bg = outs[:8]
    sources = [(k4, v4)]
    if latent:
        sources.append(ctx["kv"])
    oattn = _attention(q, lw["qmax"], ksq, sources, tiles["tq"], tiles["tk"])

    gt = bg[:, :, 2 * N_DELTA_HEADS:4 * N_DELTA_HEADS].reshape(B, L // CHUNK, CHUNK, 2, N_DELTA_HEADS)
    gt = gt.transpose(0, 1, 3, 4, 2).reshape(B, L // CHUNK, 2, D_DELTA)
    pm, bm, ol, e = _delta_chunks(act, bg, gt, tiles["tc"])
    s0 = ctx["state"] if latent else jnp.zeros((B, 2, HEAD_DIM, D_DELTA), F32)
    o_f, o_b, s_out = _delta_scan(pm, bm, ol, e, s0, tiles["bb"], tiles["cs"])

    y = _merge(x, mod, up, gates, oattn, o_f, o_b, lw["pw_bd"], lw["pscale"], lw["onorm_row"],
               lw["w_out"], lw["layer"], lw["gpost"], tiles["tmm"])
    if latent:
        return y
    kn, v = outs[8], outs[9]
    s_out = s_out.reshape(B, 2, HEAD_DIM, N_DELTA_HEADS, HEAD_DIM).transpose(0, 1, 3, 2, 4)
    return y, kn, v, s_out


def kernel(x_prompt, x_sample, cache_attn_k, cache_attn_v, state_delta, c, c_ctx, w_mod, b_mod, norm_pre, norm_post,
           w_in, w_out, pool_w, pool_scale, q_norm, k_norm, conv_w, a_log, dt_bias, o_norm):
    B, L, _ = x_prompt.shape
    DB, DL, _ = x_sample.shape
    past = cache_attn_k.shape[2]

    conds = jnp.zeros((8, D_MODEL), F32).at[0].set(c_ctx).at[1:1 + DB].set(c)
    mod = _modulation(conds, w_mod, b_mod)
    rope_tabs = _rope_tables(DL)

    ctx_tiles = dict(tm=256, tmm=256, tq=256, tk=256, tc=256, bb=8, cs=4)
    lat_tiles = dict(tm=1024, tmm=512, tq=256, tk=1024, tc=512, bb=4, cs=8)

    hp, hs = x_prompt, x_sample
    new_k, new_v, new_s = [], [], []
    w_in_all = w_in.astype(BF16)
    w_in_tail = _w_in_tail(w_in)
    w_out_all = w_out.astype(BF16)
    for l in range(DEPTH):
        blocks = jnp.zeros((N_DELTA_HEADS, HEAD_DIM, N_DELTA_HEADS, HEAD_DIM), F32)
        blocks = blocks.at[jnp.arange(4), :, jnp.arange(4), :].set(pool_w[l])
        qkn = jnp.zeros((8, LANES), F32).at[0].set(jnp.tile(q_norm[l], 2)).at[1].set(jnp.tile(k_norm[l], 2))
        lw = dict(
            layer=l,
            gpre=norm_pre[l].reshape(1, D_MODEL),
            gpost=norm_post[l].reshape(1, D_MODEL),
            w_in=w_in_all,
            w_in_tail=w_in_tail,
            w_out=w_out_all,
            qkn=qkn,
            conv_w=conv_w[l],
            alog_row=_lane_row(a_log[l], 2 * N_DELTA_HEADS),
            dtb_row=_lane_row(dt_bias[l], 2 * N_DELTA_HEADS),
            pw_bd=blocks.reshape(D_POOL, D_POOL).astype(BF16),
            pscale=pool_scale[l].reshape(1, D_POOL),
            onorm_row=jnp.tile(o_norm[l], 2).reshape(1, LANES),
            qmax=jnp.full((1, LANES), HEAD_DIM ** 0.5 * Q_SCALE * 1.01, F32) * jnp.max(jnp.abs(q_norm[l])),
        )
        hp, k_l, v_l, s_l = _layer(hp, mod[l, 0:1].reshape(1, 1, 3 * D_MODEL), lw, None, None, ctx_tiles)
        ctx = dict(kv=_cached_kv_variants(cache_attn_k[:, l].reshape(DB, past, D_KV),
                                          cache_attn_v[:, l].reshape(DB, past, D_KV)),
                   state=state_delta[:, l].transpose(0, 1, 3, 2, 4).reshape(DB, 2, HEAD_DIM, D_DELTA))
        hs = _layer(hs, mod[l, 1:1 + DB].reshape(DB, 1, 3 * D_MODEL), lw, rope_tabs, ctx, lat_tiles)
        new_k.append(k_l.reshape(B, L, 2, HEAD_DIM))
        new_v.append(v_l.reshape(B, L, 2, HEAD_DIM))
        new_s.append(s_l)
    return (hp, hs, jnp.stack(new_k, axis=1), jnp.stack(new_v, axis=1), jnp.stack(new_s, axis=1))
```
